```python
import math
import jax
import jax.numpy as jnp
from jax import lax
import numpy as np

D_MODEL = 2048
BATCH = 2
SEQ = 8192
DEPTH = 4

GRID_W = 64
CTX_LEN = 256
N_MIXERS = 3
NORM_EPS = 1e-6
ROPE_THETA = 10000.0

NA_HEADS = 16
NA_HEAD_DIM = D_MODEL // NA_HEADS
NA_WIN_ROWS = 8
NA_WIN_COLS = 16
NA_QBLOCK_COLS = 16
NA_KBLOCK_COLS = NA_QBLOCK_COLS + NA_WIN_COLS

DIFF_HEADS = 8
DIFF_HEAD_DIM = D_MODEL // (2 * DIFF_HEADS)
DIFF_QBLOCK = 128

HGRN_EXPAND = 128
HGRN_HEADS = D_MODEL // HGRN_EXPAND
HGRN_HEAD_DIM = D_MODEL // HGRN_HEADS
HGRN_CHUNK = 64

N_EXPERTS = 16
N_GROUPS = 4
EXPERTS_PER_GROUP = N_EXPERTS // N_GROUPS
TOP_K = 2
D_EXPERT = D_MODEL // 2
MOE_BLOCK = 128

kernel_name = 'hybrid_natten_diffattn_hgrn2_grouped_moe_dit'


def rms_norm(x, g):
    x32 = x.astype(jnp.float32)
    y = x32 * lax.rsqrt(jnp.mean(x32 * x32, axis=-1, keepdims=True) + NORM_EPS)
    return (y * g.astype(jnp.float32)).astype(x.dtype)


def axial_rope(x, n_tokens):
    d = x.shape[-1]
    half = d // 2
    quarter = half // 2
    t = jnp.arange(n_tokens)
    row = (t // GRID_W).astype(jnp.float32)
    col = (t % GRID_W).astype(jnp.float32)
    inv = ROPE_THETA ** (-jnp.arange(quarter, dtype=jnp.float32) / quarter)
    shape = (1, n_tokens) + (1,) * (x.ndim - 3) + (quarter,)

    def rot(xa, pos):
        ang = pos[:, None] * inv[None, :]
        cos = jnp.cos(ang).reshape(shape)
        sin = jnp.sin(ang).reshape(shape)
        x1, x2 = xa[..., :quarter], xa[..., quarter:]
        return jnp.concatenate([x1 * cos - x2 * sin, x2 * cos + x1 * sin], axis=-1)

    xf = x.astype(jnp.float32)
    out = jnp.concatenate([rot(xf[..., :half], row), rot(xf[..., half:], col)], axis=-1)
    return out.astype(x.dtype)


def neighborhood_attention(hx, hc, w_qkv, rpb, w_o, need_ctx):
    Bn, S, D = hx.shape
    H, hd = NA_HEADS, NA_HEAD_DIM
    rows = S // GRID_W
    kr = min(NA_WIN_ROWS, rows)
    qc_, kbc = NA_QBLOCK_COLS, NA_KBLOCK_COLS
    scale = hd ** -0.5
    qkv_x = (hx @ w_qkv).reshape(Bn, S, 3, H, hd)
    qkv_c = (hc @ w_qkv).reshape(Bn, CTX_LEN, 3, H, hd)
    q_grid = qkv_x[:, :, 0].reshape(Bn, rows, GRID_W, H, hd)
    k_grid = qkv_x[:, :, 1].reshape(Bn, rows, GRID_W, H, hd)
    v_grid = qkv_x[:, :, 2].reshape(Bn, rows, GRID_W, H, hd)
    q_c, k_c, v_c = qkv_c[:, :, 0], qkv_c[:, :, 1], qkv_c[:, :, 2]
    n_cb = GRID_W // qc_

    def one_block(b):
        r = b // n_cb
        c0 = (b % n_cb) * qc_
        rs = jnp.clip(r - kr // 2, 0, rows - kr)
        kb = jnp.clip(c0 - NA_WIN_COLS // 2, 0, GRID_W - kbc)
        q = lax.dynamic_slice(q_grid, (0, r, c0, 0, 0), (Bn, 1, qc_, H, hd))[:, 0]
        k = lax.dynamic_slice(k_grid, (0, rs, kb, 0, 0), (Bn, kr, kbc, H, hd)).reshape(Bn, kr * kbc, H, hd)
        v = lax.dynamic_slice(v_grid, (0, rs, kb, 0, 0), (Bn, kr, kbc, H, hd)).reshape(Bn, kr * kbc, H, hd)
        qcol = c0 + jnp.arange(qc_)
        kcol = kb + jnp.arange(kbc)
        krow = rs + jnp.arange(kr)
        cs = jnp.clip(qcol - NA_WIN_COLS // 2, 0, GRID_W - NA_WIN_COLS)
        col_ok = (kcol[None, :] >= cs[:, None]) & (kcol[None, :] < cs[:, None] + NA_WIN_COLS)
        dr = krow - r + (NA_WIN_ROWS - 1)
        dc = jnp.clip(kcol[None, :] - qcol[:, None] + (NA_WIN_COLS - 1), 0, 2 * NA_WIN_COLS - 2)
        bias = rpb[:, dr[None, :, None], dc[:, None, :]].astype(jnp.float32)
        bias = jnp.where(col_ok[:, None, :], bias, -jnp.inf).reshape(H, qc_, kr * kbc)
        s_lat = jnp.einsum('bqhd,bkhd->bhqk', q, k, preferred_element_type=jnp.float32) * scale + bias[None]
        s_ctx = jnp.einsum('bqhd,bkhd->bhqk', q, k_c, preferred_element_type=jnp.float32) * scale
        p = jax.nn.softmax(jnp.concatenate([s_ctx, s_lat], axis=-1), axis=-1)
        v_all = jnp.concatenate([v_c, v], axis=1)
        return jnp.einsum('bhqk,bkhd->bqhd', p.astype(v_all.dtype), v_all)

    o = lax.map(one_block, jnp.arange(rows * n_cb))
    o = o.reshape(rows, n_cb, Bn, qc_, H, hd).transpose(2, 0, 1, 3, 4, 5).reshape(Bn, S, D)
    yx = o @ w_o
    yc = None
    if need_ctx:
        s = jnp.einsum('bqhd,bkhd->bhqk', q_c, k_c, preferred_element_type=jnp.float32) * scale
        p = jax.nn.softmax(s, axis=-1)
        yc = jnp.einsum('bhqk,bkhd->bqhd', p.astype(v_c.dtype), v_c).reshape(Bn, CTX_LEN, D) @ w_o
    return yx, yc


def diff_attention(hx, hc, w_qkv, lam, subln_g, w_o, layer_idx, need_ctx):
    Bn, S, D = hx.shape
    H, d = DIFF_HEADS, DIFF_HEAD_DIM
    lam_init = 0.8 - 0.6 * math.exp(-0.3 * layer_idx)
    l32 = lam.astype(jnp.float32)
    lam_full = jnp.exp(jnp.sum(l32[0] * l32[1])) - jnp.exp(jnp.sum(l32[2] * l32[3])) + lam_init
    scale = d ** -0.5

    def proj(h):
        T = h.shape[1]
        qkv = h @ w_qkv
        q = qkv[..., :D].reshape(Bn, T, H, 2, d)
        k = qkv[..., D:2 * D].reshape(Bn, T, H, 2, d)
        v = qkv[..., 2 * D:].reshape(Bn, T, H, 2 * d)
        return q, k, v

    def attend(q, k, v):
        s = jnp.einsum('bqhcd,bkhcd->bhcqk', q, k, preferred_element_type=jnp.float32) * scale
        p = jax.nn.softmax(s, axis=-1)
        a = p[:, :, 0] - lam_full * p[:, :, 1]
        return jnp.einsum('bhqk,bkhe->bqhe', a.astype(v.dtype), v)

    def finish(o):
        T = o.shape[1]
        o = rms_norm(o, subln_g) * (1.0 - lam_init)
        return o.reshape(Bn, T, D) @ w_o

    qx, kx, vx = proj(hx)
    qc, kc, vc = proj(hc)
    qx = axial_rope(qx, S)
    kx = axial_rope(kx, S)
    k_all = jnp.concatenate([kc, kx], axis=1)
    v_all = jnp.concatenate([vc, vx], axis=1)
    n_qb = S // DIFF_QBLOCK
    q_blocks = qx.reshape(Bn, n_qb, DIFF_QBLOCK, H, 2, d).transpose(1, 0, 2, 3, 4, 5)
    o = lax.map(lambda qb: attend(qb, k_all, v_all), q_blocks)
    o = o.transpose(1, 0, 2, 3, 4).reshape(Bn, S, H, 2 * d)
    yx = finish(o)
    yc = finish(attend(qc, kc, vc)) if need_ctx else None
    return yx, yc


def gla_chunk_scan(q, k, v, log_f, state0):
    Bn, T, H, _ = q.shape
    dv = v.shape[-1]
    n = T // HGRN_CHUNK
    f32 = jnp.float32

    def chunks(a):
        return a.astype(f32).reshape(Bn, n, HGRN_CHUNK, H, a.shape[-1]).transpose(1, 0, 3, 2, 4)

    lower = jnp.tril(jnp.ones((HGRN_CHUNK, HGRN_CHUNK), dtype=bool))[:, :, None]

    def step(state, inp):
        qc, kc, vc, gc = inp
        bcum = jnp.cumsum(gc, axis=2)
        o_inter = jnp.einsum('bhtd,bhde->bhte', qc * jnp.exp(bcum), state)
        decay = jnp.exp(jnp.where(lower, bcum[:, :, :, None, :] - bcum[:, :, None, :, :], -jnp.inf))
        scores = jnp.einsum('bhtd,bhsd,bhtsd->bhts', qc, kc, decay)
        o_intra = jnp.einsum('bhts,bhse->bhte', scores, vc)
        b_end = bcum[:, :, -1, :]
        k_to_end = kc * jnp.exp(b_end[:, :, None, :] - bcum)
        state = jnp.exp(b_end)[..., None] * state + jnp.einsum('bhsd,bhse->bhde', k_to_end, vc)
        return state, o_inter + o_intra

    state, o = lax.scan(step, state0, (chunks(q), chunks(k), chunks(v), chunks(log_f)))
    o = o.transpose(1, 0, 3, 2, 4).reshape(Bn, T, H, dv)
    return o, state


def hgrn2(hx, hc, w_in, lb_fw, lb_bw, norm_g, w_o, need_ctx):
    Bn, S, D = hx.shape
    H, dk, dv = HGRN_HEADS, HGRN_EXPAND, HGRN_HEAD_DIM
    f32 = jnp.float32

    def proj(h):
        T = h.shape[1]
        q, i, g, f_fw, f_bw = jnp.split(h @ w_in, 5, axis=-1)
        q = (jax.nn.silu(q.astype(f32)) * dk ** -0.5).reshape(Bn, T, H, dk)
        i = i.astype(f32).reshape(Bn, T, H, dv)
        return q, i, g, f_fw, f_bw

    def gate(f_raw, lb):
        T = f_raw.shape[1]
        fg = lb + (1.0 - lb) * jax.nn.sigmoid(f_raw.astype(f32))
        return (1.0 - fg).reshape(Bn, T, H, dk), jnp.log(fg).reshape(Bn, T, H, dk)

    def flip(a):
        return jnp.flip(a, axis=1)

    qx, ix, gx, fx_fw, fx_bw = proj(hx)
    qc, ic, gc, fc_fw, fc_bw = proj(hc)
    zero = jnp.zeros((Bn, H, dk, dv), f32)
    kc_f, lc_f = gate(fc_fw, lb_fw)
    kx_f, lx_f = gate(fx_fw, lb_fw)
    oc_f, st_cf = gla_chunk_scan(qc, kc_f, ic, lc_f, zero)
    ox_f, _ = gla_chunk_scan(qx, kx_f, ix, lx_f, st_cf)
    kc_b, lc_b = gate(fc_bw, lb_bw)
    kx_b, lx_b = gate(fx_bw, lb_bw)
    oc_b, st_cb = gla_chunk_scan(flip(qc), flip(kc_b), flip(ic), flip(lc_b), zero)
    ox_b, _ = gla_chunk_scan(flip(qx), flip(kx_b), flip(ix), flip(lx_b), st_cb)

    def finish(o, g):
        T = o.shape[1]
        o = rms_norm(o.reshape(Bn, T, D), norm_g) * jax.nn.silu(g.astype(f32))
        return o.astype(hx.dtype) @ w_o

    yx = finish(ox_f + flip(ox_b), gx)
    yc = finish(oc_f + flip(oc_b), gc) if need_ctx else None
    return yx, yc


def moe_ffn(h, w_router, b_router, w_gate, w_up, w_down):
    T, D = h.shape
    s = jax.nn.sigmoid((h @ w_router).astype(jnp.float32))
    sel = s + b_router.astype(jnp.float32)
    grp_score = lax.top_k(sel.reshape(T, N_GROUPS, EXPERTS_PER_GROUP), 2)[0].sum(-1)
    g_best = jnp.argmax(grp_score, axis=-1)
    in_grp = (jnp.arange(N_EXPERTS) // EXPERTS_PER_GROUP)[None, :] == g_best[:, None]
    _, e_idx = lax.top_k(jnp.where(in_grp, sel, -jnp.inf), TOP_K)
    w = jnp.take_along_axis(s, e_idx, axis=1)
    w = w / jnp.sum(w, axis=-1, keepdims=True)
    n_assign = T * TOP_K
    flat_e = e_idx.reshape(-1)
    flat_t = jnp.repeat(jnp.arange(T, dtype=jnp.int32), TOP_K)
    flat_w = w.reshape(-1)
    order = jnp.argsort(flat_e)
    se = flat_e[order]
    counts = jnp.bincount(flat_e, length=N_EXPERTS)
    starts = jnp.cumsum(counts) - counts
    pcounts = (counts + MOE_BLOCK - 1) // MOE_BLOCK * MOE_BLOCK
    pends = jnp.cumsum(pcounts)
    pstarts = pends - pcounts
    dest = pstarts[se] + jnp.arange(n_assign) - starts[se]
    n_blocks = -(-n_assign // MOE_BLOCK) + N_EXPERTS
    n_slots = n_blocks * MOE_BLOCK
    slot_tok = jnp.full((n_slots,), T, jnp.int32).at[dest].set(flat_t[order])
    slot_w = jnp.zeros((n_slots,), jnp.float32).at[dest].set(flat_w[order])
    blk_expert = jnp.minimum(jnp.searchsorted(pends, jnp.arange(n_blocks) * MOE_BLOCK, side='right'), N_EXPERTS - 1)
    h_pad = jnp.concatenate([h, jnp.zeros((1, D), h.dtype)], axis=0)
    xb = h_pad[slot_tok].reshape(n_blocks, MOE_BLOCK, D)

    def expert_block(args):
        xk, e = args
        return (jax.nn.silu(xk @ w_gate[e]) * (xk @ w_up[e])) @ w_down[e]

    yb = lax.map(expert_block, (xb, blk_expert)).reshape(n_slots, D)
    y = jnp.zeros((T + 1, D), jnp.float32).at[slot_tok].add(yb.astype(jnp.float32) * slot_w[:, None])
    return y[:T].astype(h.dtype)


def setup_inputs(seed: int = 0) -> dict:
    key = jax.random.key(seed)
    keys = iter(jax.random.split(key, 32))
    D = D_MODEL
    n_a = len(range(0, DEPTH, N_MIXERS))
    n_b = len(range(1, DEPTH, N_MIXERS))
    n_c = len(range(2, DEPTH, N_MIXERS))

    def nrm(shape, scale):
        return jax.random.normal(next(keys), shape, jnp.float32) * scale

    return {
        'x': nrm((BATCH, SEQ, D), 1.0),
        'c': nrm((BATCH, D), 1.0),
        'ctx': nrm((BATCH, CTX_LEN, D), 1.0),
        'c_ctx': nrm((D,), 1.0),
        'ada_w': nrm((DEPTH, D, 6 * D), 0.5 * D ** -0.5),
        'ada_b': nrm((DEPTH, 6 * D), 0.02),
        'norm_g': 1.0 + nrm((DEPTH, 2, D), 0.02),
        'na_w_qkv': nrm((n_a, D, 3 * D), D ** -0.5),
        'na_rpb': nrm((n_a, NA_HEADS, 2 * NA_WIN_ROWS - 1, 2 * NA_WIN_COLS - 1), 0.5),
        'na_w_o': nrm((n_a, D, D), D ** -0.5),
        'da_w_qkv': nrm((n_b, D, 3 * D), D ** -0.5),
        'da_lambda': nrm((n_b, 4, DIFF_HEAD_DIM), 0.1),
        'da_subln_g': 1.0 + nrm((n_b, 2 * DIFF_HEAD_DIM), 0.02),
        'da_w_o': nrm((n_b, D, D), D ** -0.5),
        'hg_w_in': nrm((n_c, D, 5 * D), D ** -0.5),
        'hg_lb': nrm((2, DEPTH, D), 0.1),
        'hg_norm_g': 1.0 + nrm((n_c, D), 0.02),
        'hg_w_o': nrm((n_c, D, D), D ** -0.5),
        'moe_w_router': nrm((D, N_EXPERTS), D ** -0.5),
        'moe_b_router': nrm((N_EXPERTS,), 0.01),
        'moe_w_gate': nrm((DEPTH, N_EXPERTS, D, D_EXPERT), D ** -0.5),
        'moe_w_up': nrm((DEPTH, N_EXPERTS, D, D_EXPERT), D ** -0.5),
        'moe_w_down': nrm((DEPTH, N_EXPERTS, D_EXPERT, D), D_EXPERT ** -0.5),
        'final_g': 1.0 + nrm((D,), 0.02),
    }


def reference(x, c, ctx, c_ctx, ada_w, ada_b, norm_g, na_w_qkv, na_rpb, na_w_o, da_w_qkv, da_lambda, da_subln_g, da_w_o, hg_w_in, hg_lb, hg_norm_g, hg_w_o, moe_w_router, moe_b_router, moe_w_gate, moe_w_up, moe_w_down, final_g):
    Bn, S, D = x.shape
    lb_p = jax.nn.softmax(hg_lb.astype(jnp.float32), axis=1)
    lb_all = jnp.cumsum(lb_p, axis=1) - lb_p[:, :1]
    silu_c = jax.nn.silu(c)
    silu_cc = jax.nn.silu(c_ctx)
    counters = [0, 0, 0]
    for i in range(DEPTH):
        kind = i % N_MIXERS
        j = counters[kind]
        counters[kind] += 1
        need_ctx = i < DEPTH - 1
        mx = (silu_c @ ada_w[i] + ada_b[i]).reshape(Bn, 6, 1, D)
        mc = (silu_cc @ ada_w[i] + ada_b[i]).reshape(6, D)
        hx = rms_norm(x, norm_g[i, 0]) * (1.0 + mx[:, 1]) + mx[:, 0]
        hc = rms_norm(ctx, norm_g[i, 0]) * (1.0 + mc[1]) + mc[0]
        if kind == 0:
            yx, yc = neighborhood_attention(hx, hc, na_w_qkv[j], na_rpb[j], na_w_o[j], need_ctx)
        elif kind == 1:
            yx, yc = diff_attention(hx, hc, da_w_qkv[j], da_lambda[j], da_subln_g[j], da_w_o[j], i, need_ctx)
        else:
            yx, yc = hgrn2(hx, hc, hg_w_in[j], lb_all[0, i], lb_all[1, i], hg_norm_g[j], hg_w_o[j], need_ctx)
        x = x + mx[:, 2] * yx
        if need_ctx:
            ctx = ctx + mc[2] * yc
        hx = rms_norm(x, norm_g[i, 1]) * (1.0 + mx[:, 4]) + mx[:, 3]
        if need_ctx:
            hc = rms_norm(ctx, norm_g[i, 1]) * (1.0 + mc[4]) + mc[3]
            tokens = jnp.concatenate([hx.reshape(-1, D), hc.reshape(-1, D)], axis=0)
        else:
            tokens = hx.reshape(-1, D)
        y = moe_ffn(tokens, moe_w_router, moe_b_router, moe_w_gate[i], moe_w_up[i], moe_w_down[i])
        x = x + mx[:, 5] * y[:Bn * S].reshape(Bn, S, D)
        if need_ctx:
            ctx = ctx + mc[5] * y[Bn * S:].reshape(Bn, CTX_LEN, D)
    return rms_norm(x, final_g)
```

```python
import functools
import math

import numpy as np
import jax
import jax.numpy as jnp
from jax import lax
from jax.experimental import pallas as pl
from jax.experimental.pallas import tpu as pltpu

F32 = jnp.float32
BF16 = jnp.bfloat16

D_MODEL = 2048
DEPTH = 4
GRID_W = 64
CTX_LEN = 256
N_MIXERS = 3
NORM_EPS = 1e-6
ROPE_THETA = 10000.0

NA_HEADS = 16
NA_HEAD_DIM = 128
NA_WIN_ROWS = 8
NA_WIN_COLS = 16

DIFF_HEADS = 8
DIFF_HEAD_DIM = 128

HGRN_HEADS = 16
HGRN_DK = 128
HGRN_CHUNK = 64

N_EXPERTS = 16
N_GROUPS = 4
EXPERTS_PER_GROUP = 4
TOP_K = 2
D_EXPERT = 1024

LANES = 128
ROW_TILE = 256
MM_TM = 512
MM_TN = 1024
MOE_TM = 256
DA_TQ = 512
VMEM_LIMIT = 56 * 1024 * 1024

NEG_BIG = -1e30


def _cparams(sem):
    return pltpu.CompilerParams(dimension_semantics=sem, vmem_limit_bytes=VMEM_LIMIT)


def _dot(a, b):
    return jnp.dot(a, b, preferred_element_type=F32)


def _dot_nt(a, b):
    return lax.dot_general(a, b, (((1,), (1,)), ((), ())), preferred_element_type=F32)


def _dot_tn(a, b):
    return lax.dot_general(a, b, (((0,), (0,)), ((), ())), preferred_element_type=F32)


def _row_sel(sub, tiles_per_batch, n_batch):
    return jnp.where(sub % tiles_per_batch == 0, n_batch, sub // tiles_per_batch)


def _ada_kernel(c_ref, w_ref, b_ref, o_ref):
    c = c_ref[...]
    a = c * jax.nn.sigmoid(c)
    o_ref[...] = jnp.dot(a, w_ref[...], preferred_element_type=F32,
                         precision=lax.Precision.HIGHEST) + b_ref[...]


def _ada_mods(c_rows, ada_w, ada_b):
    depth, d, n = ada_w.shape
    tn = 1024
    return pl.pallas_call(
        _ada_kernel,
        grid=(depth, n // tn),
        in_specs=[
            pl.BlockSpec((8, d), lambda l, j: (0, 0)),
            pl.BlockSpec((None, d, tn), lambda l, j: (l, 0, j)),
            pl.BlockSpec((None, 1, tn), lambda l, j: (l, 0, j)),
        ],
        out_specs=pl.BlockSpec((None, 8, tn), lambda l, j: (l, 0, j)),
        out_shape=jax.ShapeDtypeStruct((depth, 8, n), F32),
        compiler_params=_cparams(("arbitrary", "arbitrary")),
        name="ada_mods",
    )(c_rows, ada_w, ada_b.reshape(depth, 1, n))


def _norm_mod_kernel(x_ref, g_ref, shift_ref, scale_ref, h_ref):
    x = x_ref[...]
    y = x * lax.rsqrt(jnp.mean(x * x, axis=-1, keepdims=True) + NORM_EPS)
    h = y * g_ref[...] * (1.0 + scale_ref[...]) + shift_ref[...]
    h_ref[...] = h.astype(h_ref.dtype)


def _norm_mod_router_kernel(x_ref, g_ref, shift_ref, scale_ref, whi_ref, wlo_ref, h_ref, lg_ref):
    x = x_ref[...]
    y = x * lax.rsqrt(jnp.mean(x * x, axis=-1, keepdims=True) + NORM_EPS)
    h = y * g_ref[...] * (1.0 + scale_ref[...]) + shift_ref[...]
    h_hi = h.astype(BF16)
    h_lo = (h - h_hi.astype(F32)).astype(BF16)
    h_ref[...] = h_hi
    lg_ref[...] = _dot(h_hi, whi_ref[...]) + _dot(h_hi, wlo_ref[...]) + _dot(h_lo, whi_ref[...])


def _norm_mod(xs, mods3, g3, layer, which, n_batch, tpb, router=None):
    t, d = xs.shape
    shift_k, scale_k = (0, 1) if which == 0 else (3, 4)
    nb = n_batch

    def mod_spec(k):
        return pl.BlockSpec((None, 1, d), lambda i: (layer * 8 + _row_sel(i, tpb, nb), 0, k))

    in_specs = [
        pl.BlockSpec((ROW_TILE, d), lambda i: (i, 0)),
        pl.BlockSpec((None, 1, d), lambda i: (layer * 2 + which, 0, 0)),
        mod_spec(shift_k),
        mod_spec(scale_k),
    ]
    args = [xs, g3, mods3, mods3]
    out_specs = [pl.BlockSpec((ROW_TILE, d), lambda i: (i, 0))]
    out_shape = [jax.ShapeDtypeStruct((t, d), BF16)]
    kern = _norm_mod_kernel
    if router is not None:
        in_specs += [pl.BlockSpec((d, LANES), lambda i: (0, 0))] * 2
        args += list(router)
        out_specs.append(pl.BlockSpec((ROW_TILE, LANES), lambda i: (i, 0)))
        out_shape.append(jax.ShapeDtypeStruct((t, LANES), F32))
        kern = _norm_mod_router_kernel
    return pl.pallas_call(
        kern,
        grid=(t // ROW_TILE,),
        in_specs=in_specs,
        out_specs=out_specs,
        out_shape=out_shape,
        compiler_params=_cparams(("arbitrary",)),
        name="norm_mod",
    )(*args)


def _rope_store(acc, cos_ref, sin_ref, o_ref):
    lane = lax.broadcasted_iota(jnp.int32, (1, LANES), 1)
    first = (lane % 64) < 32
    cos = cos_ref[...]
    sin = sin_ref[...]
    for c in range(acc.shape[1] // LANES):
        xc = acc[:, c * LANES:(c + 1) * LANES]
        sw = jnp.where(first, pltpu.roll(xc, 96, 1), pltpu.roll(xc, 32, 1))
        o_ref[:, c * LANES:(c + 1) * LANES] = (xc * cos + sw * sin).astype(o_ref.dtype)


def _mm_kernel(x_ref, w_ref, o_ref, wb_ref):
    @pl.when(pl.program_id(1) == 0)
    def _():
        wb_ref[...] = w_ref[...].astype(BF16)

    o_ref[...] = _dot(x_ref[...], wb_ref[...]).astype(o_ref.dtype)


def _mm_rope_kernel(x_ref, w_ref, cos_ref, sin_ref, o_ref, wb_ref, *, n_rope):
    @pl.when(pl.program_id(1) == 0)
    def _():
        wb_ref[...] = w_ref[...].astype(BF16)

    acc = _dot(x_ref[...], wb_ref[...])
    j = pl.program_id(0)

    @pl.when(j < n_rope)
    def _():
        _rope_store(acc, cos_ref, sin_ref, o_ref)

    @pl.when(j >= n_rope)
    def _():
        o_ref[...] = acc.astype(o_ref.dtype)


def _mm_res_kernel(x_ref, w_ref, xs_ref, ga_ref, gb_ref, o_ref, wb_ref):
    @pl.when(pl.program_id(1) == 0)
    def _():
        wb_ref[...] = w_ref[...].astype(BF16)

    acc = _dot(x_ref[...], wb_ref[...])
    half = ROW_TILE
    o_ref[:half, :] = xs_ref[:half, :] + ga_ref[...] * acc[:half, :]
    o_ref[half:, :] = xs_ref[half:, :] + gb_ref[...] * acc[half:, :]


def _matmul(x, w, rope=None, n_rope_cols=0):
    t, k = x.shape
    n = w.shape[1]
    in_specs = [
        pl.BlockSpec((MM_TM, k), lambda j, i: (i, 0)),
        pl.BlockSpec((k, MM_TN), lambda j, i: (0, j)),
    ]
    args = [x, w]
    kern = _mm_kernel
    if rope is not None:
        in_specs += [pl.BlockSpec((MM_TM, LANES), lambda j, i: (i, 0))] * 2
        args += list(rope)
        kern = functools.partial(_mm_rope_kernel, n_rope=n_rope_cols // MM_TN)
    return pl.pallas_call(
        kern,
        grid=(n // MM_TN, t // MM_TM),
        in_specs=in_specs,
        out_specs=pl.BlockSpec((MM_TM, MM_TN), lambda j, i: (i, j)),
        out_shape=jax.ShapeDtypeStruct((t, n), BF16),
        scratch_shapes=[pltpu.VMEM((k, MM_TN), BF16)],
        compiler_params=_cparams(("arbitrary", "arbitrary")),
        name="matmul",
    )(*args)


def _matmul_residual(x, w, xs, mods3, layer, gate_k, n_batch, tpb):
    t, k = x.shape
    n = w.shape[1]
    ncol = n // MM_TN
    nb = n_batch

    def gate_spec(half):
        return pl.BlockSpec(
            (None, 1, MM_TN),
            lambda j, i: (layer * 8 + _row_sel(2 * i + half, tpb, nb), 0, gate_k * ncol + j))

    return pl.pallas_call(
        _mm_res_kernel,
        grid=(ncol, t // MM_TM),
        in_specs=[
            pl.BlockSpec((MM_TM, k), lambda j, i: (i, 0)),
            pl.BlockSpec((k, MM_TN), lambda j, i: (0, j)),
            pl.BlockSpec((MM_TM, MM_TN), lambda j, i: (i, j)),
            gate_spec(0),
            gate_spec(1),
        ],
        out_specs=pl.BlockSpec((MM_TM, MM_TN), lambda j, i: (i, j)),
        out_shape=jax.ShapeDtypeStruct((t, n), F32),
        scratch_shapes=[pltpu.VMEM((k, MM_TN), BF16)],
        input_output_aliases={2: 0},
        compiler_params=_cparams(("arbitrary", "arbitrary")),
        name="matmul_residual",
    )(x, w, xs, mods3, mods3)


def _softmax_rows(s):
    m = jnp.max(s, axis=-1, keepdims=True)
    p = jnp.exp(s - m)
    return p, jnp.sum(p, axis=-1, keepdims=True)


def _na_kernel(q_ref, k_ref, v_ref, bias_ref, o_ref, *, rows):
    scale = NA_HEAD_DIM ** -0.5
    win = NA_WIN_ROWS * GRID_W

    qc = q_ref[0:CTX_LEN, :]
    p, l = _softmax_rows(_dot_nt(qc, k_ref[0:CTX_LEN, :]) * scale)
    o_ref[0:CTX_LEN, :] = (_dot(p.astype(BF16), v_ref[0:CTX_LEN, :]) / l).astype(o_ref.dtype)

    def row_body(r, carry):
        rs = jnp.clip(r - NA_WIN_ROWS // 2, 0, rows - NA_WIN_ROWS)
        var = rs - r + (NA_WIN_ROWS - 1)
        q0 = pl.multiple_of(CTX_LEN + r * GRID_W, GRID_W)
        k0 = pl.multiple_of(CTX_LEN + rs * GRID_W, GRID_W)
        q = q_ref[pl.ds(q0, GRID_W), :]
        s_lat = _dot_nt(q, k_ref[pl.ds(k0, win), :]) * scale + bias_ref[var]
        s_ctx = _dot_nt(q, k_ref[0:CTX_LEN, :]) * scale
        m = jnp.maximum(jnp.max(s_lat, axis=-1, keepdims=True), jnp.max(s_ctx, axis=-1, keepdims=True))
        p_lat = jnp.exp(s_lat - m)
        p_ctx = jnp.exp(s_ctx - m)
        l = jnp.sum(p_lat, axis=-1, keepdims=True) + jnp.sum(p_ctx, axis=-1, keepdims=True)
        o = _dot(p_lat.astype(BF16), v_ref[pl.ds(k0, win), :]) + _dot(p_ctx.astype(BF16), v_ref[0:CTX_LEN, :])
        o_ref[pl.ds(q0, GRID_W), :] = (o / l).astype(o_ref.dtype)
        return carry

    lax.fori_loop(0, rows, row_body, 0, unroll=4)


def _na_bias_table(rpb):
    h = rpb.shape[0]
    qc = np.arange(GRID_W)[:, None]
    kc = np.arange(GRID_W)[None, :]
    cs = np.clip(qc - NA_WIN_COLS // 2, 0, GRID_W - NA_WIN_COLS)
    col_ok = (kc >= cs) & (kc < cs + NA_WIN_COLS)
    dc = np.clip(kc - qc + (NA_WIN_COLS - 1), 0, 2 * NA_WIN_COLS - 2)
    a = jnp.take(rpb.astype(F32), jnp.asarray(dc.reshape(-1)), axis=2)
    a = a.reshape(h, 2 * NA_WIN_ROWS - 1, GRID_W, GRID_W)
    a = jnp.where(jnp.asarray(col_ok)[None, None], a, NEG_BIG)
    tab = jnp.stack([a[:, v:v + NA_WIN_ROWS] for v in range(NA_WIN_ROWS)], axis=1)
    return tab.transpose(0, 1, 3, 2, 4).reshape(h, NA_WIN_ROWS, GRID_W, NA_WIN_ROWS * GRID_W)


def _na_attention(qkv, rpb, n_batch, tb):
    t = qkv.shape[0]
    rows = (tb - CTX_LEN) // GRID_W
    table = _na_bias_table(rpb)
    hh = NA_HEADS
    return pl.pallas_call(
        functools.partial(_na_kernel, rows=rows),
        grid=(n_batch, hh),
        in_specs=[
            pl.BlockSpec((tb, LANES), lambda b, h: (b, h)),
            pl.BlockSpec((tb, LANES), lambda b, h: (b, hh + h)),
            pl.BlockSpec((tb, LANES), lambda b, h: (b, 2 * hh + h)),
            pl.BlockSpec((None, NA_WIN_ROWS, GRID_W, NA_WIN_ROWS * GRID_W), lambda b, h: (h, 0, 0, 0)),
        ],
        out_specs=pl.BlockSpec((tb, LANES), lambda b, h: (b, h)),
        out_shape=jax.ShapeDtypeStruct((t, D_MODEL), BF16),
        compiler_params=_cparams(("arbitrary", "arbitrary")),
        name="na_attention",
    )(qkv, qkv, qkv, table)


def _da_kernel(lam_ref, g_ref, q_ref, k_ref, v_ref, o_ref, acc_ref, m_ref, l_ref, *, lam_init, n_lat_chunks):
    d = DIFF_HEAD_DIM
    scale = d ** -0.5
    i = pl.program_id(2)
    lam = lam_ref[...]
    lam_full = (jnp.exp(jnp.sum(lam[0:1] * lam[1:2], axis=-1, keepdims=True))
                - jnp.exp(jnp.sum(lam[2:3] * lam[3:4], axis=-1, keepdims=True)) + lam_init)

    def init(nq):
        acc_ref[:, 0:nq, :] = jnp.zeros((2, nq, 2 * d), F32)
        m_ref[:, 0:nq, :] = jnp.full((2, nq, 1), NEG_BIG, F32)
        l_ref[:, 0:nq, :] = jnp.zeros((2, nq, 1), F32)

    def chunk(q, nq, k0, nk):
        k = k_ref[pl.ds(k0, nk), :]
        v = v_ref[pl.ds(k0, nk), :]
        for c in range(2):
            s = _dot_nt(q[:, c * d:(c + 1) * d], k[:, c * d:(c + 1) * d]) * scale
            m_prev = m_ref[c, 0:nq, :]
            m_new = jnp.maximum(m_prev, jnp.max(s, axis=-1, keepdims=True))
            alpha = jnp.exp(m_prev - m_new)
            p = jnp.exp(s - m_new)
            l_ref[c, 0:nq, :] = alpha * l_ref[c, 0:nq, :] + jnp.sum(p, axis=-1, keepdims=True)
            acc_ref[c, 0:nq, :] = alpha * acc_ref[c, 0:nq, :] + _dot(p.astype(BF16), v)
            m_ref[c, 0:nq, :] = m_new

    def finish(q0, nq):
        o = (acc_ref[0, 0:nq, :] / l_ref[0, 0:nq, :]
             - lam_full * (acc_ref[1, 0:nq, :] / l_ref[1, 0:nq, :]))
        y = o * lax.rsqrt(jnp.mean(o * o, axis=-1, keepdims=True) + NORM_EPS)
        o_ref[pl.ds(q0, nq), :] = (y * g_ref[...] * (1.0 - lam_init)).astype(o_ref.dtype)

    @pl.when(i == 0)
    def _():
        init(CTX_LEN)
        chunk(q_ref[0:CTX_LEN, :], CTX_LEN, 0, CTX_LEN)
        finish(0, CTX_LEN)

    @pl.when(i > 0)
    def _():
        q0 = pl.multiple_of(CTX_LEN + (i - 1) * DA_TQ, ROW_TILE)
        q = q_ref[pl.ds(q0, DA_TQ), :]
        init(DA_TQ)
        chunk(q, DA_TQ, 0, CTX_LEN)

        def body(n, carry):
            chunk(q, DA_TQ, pl.multiple_of(CTX_LEN + n * DA_TQ, ROW_TILE), DA_TQ)
            return carry

        lax.fori_loop(0, n_lat_chunks, body, 0)
        finish(q0, DA_TQ)


def _diff_attention(qkv, lam, subln_g, layer_idx, n_batch, tb):
    t = qkv.shape[0]
    hh = DIFF_HEADS
    w = 2 * DIFF_HEAD_DIM
    n_lat = (tb - CTX_LEN) // DA_TQ
    lam_init = 0.8 - 0.6 * math.exp(-0.3 * layer_idx)
    return pl.pallas_call(
        functools.partial(_da_kernel, lam_init=lam_init, n_lat_chunks=n_lat),
        grid=(n_batch, hh, 1 + n_lat),
        in_specs=[
            pl.BlockSpec((4, DIFF_HEAD_DIM), lambda b, h, i: (0, 0)),
            pl.BlockSpec((1, w), lambda b, h, i: (0, 0)),
            pl.BlockSpec((tb, w), lambda b, h, i: (b, h)),
            pl.BlockSpec((tb, w), lambda b, h, i: (b, hh + h)),
            pl.BlockSpec((tb, w), lambda b, h, i: (b, 2 * hh + h)),
        ],
        out_specs=pl.BlockSpec((tb, w), lambda b, h, i: (b, h)),
        out_shape=jax.ShapeDtypeStruct((t, D_MODEL), BF16),
        scratch_shapes=[
            pltpu.VMEM((2, DA_TQ, w), F32),
            pltpu.VMEM((2, DA_TQ, 1), F32),
            pltpu.VMEM((2, DA_TQ, 1), F32),
        ],
        compiler_params=_cparams(("arbitrary", "arbitrary", "arbitrary")),
        name="diff_attention",
    )(lam.astype(F32), subln_g.astype(F32).reshape(1, w), qkv, qkv, qkv)


def _hg_kernel(q_ref, i_ref, ff_ref, fb_ref, lb_ref, o_ref, stf_ref, stb_ref, *, n_chunks, n_ctx_chunks):
    c = HGRN_CHUNK
    o_ref[...] = jnp.zeros(o_ref.shape, F32)
    stf_ref[...] = jnp.zeros(stf_ref.shape, F32)
    stb_ref[...] = jnp.zeros(stb_ref.shape, F32)
    row = lax.broadcasted_iota(jnp.int32, (c, c), 0)
    col = lax.broadcasted_iota(jnp.int32, (c, c), 1)
    lower = row >= col
    upper = col >= row
    lb_all = lb_ref[...]

    def chunk(cidx, f_ref, lb, mask, st_ref, end_row):
        r0 = pl.multiple_of(cidx * c, c)
        qr = q_ref[pl.ds(r0, c), :].astype(F32)
        q = qr * jax.nn.sigmoid(qr) * (HGRN_DK ** -0.5)
        v = i_ref[pl.ds(r0, c), :]
        fr = f_ref[pl.ds(r0, c), :].astype(F32)
        fg = lb + (1.0 - lb) * jax.nn.sigmoid(fr)
        k = 1.0 - fg
        lf = jnp.log(fg)
        bcum = jnp.dot(mask.astype(F32), lf, preferred_element_type=F32, precision=lax.Precision.HIGHEST)
        b_end = bcum[end_row:end_row + 1, :]
        b_mid = bcum[c // 2:c // 2 + 1, :]
        st = st_ref[...]
        o_inter = _dot_nt((q * jnp.exp(bcum)).astype(BF16), st.astype(BF16))
        qt = (q * jnp.exp(bcum - b_mid)).astype(BF16)
        kt = (k * jnp.exp(b_mid - bcum)).astype(BF16)
        a = jnp.where(mask, _dot_nt(qt, kt), 0.0)
        o = o_inter + _dot(a.astype(BF16), v)
        k_end = (k * jnp.exp(b_end - bcum)).astype(BF16)
        st_ref[...] = st * jnp.exp(b_end) + _dot_tn(v, k_end)
        o_ref[pl.ds(r0, c), :] += o

    def body(n, carry):
        chunk(n, ff_ref, lb_all[0:1, :], lower, stf_ref, c - 1)
        nb = jnp.where(n < n_ctx_chunks, n_ctx_chunks - 1 - n, n_chunks + n_ctx_chunks - 1 - n)
        chunk(nb, fb_ref, lb_all[1:2, :], upper, stb_ref, 0)
        return carry

    lax.fori_loop(0, n_chunks, body, 0)


def _hgrn_scan(proj, lb2, n_batch, tb):
    t = proj.shape[0]
    hh = HGRN_HEADS
    nch = tb // HGRN_CHUNK

    def col(k):
        return pl.BlockSpec((tb, LANES), lambda b, h: (b, k * hh + h))

    return pl.pallas_call(
        functools.partial(_hg_kernel, n_chunks=nch, n_ctx_chunks=CTX_LEN // HGRN_CHUNK),
        grid=(n_batch, hh),
        in_specs=[col(0), col(1), col(3), col(4), pl.BlockSpec((2, LANES), lambda b, h: (0, h))],
        out_specs=pl.BlockSpec((tb, LANES), lambda b, h: (b, h)),
        out_shape=jax.ShapeDtypeStruct((t, D_MODEL), F32),
        scratch_shapes=[pltpu.VMEM((LANES, LANES), F32), pltpu.VMEM((LANES, LANES), F32)],
        compiler_params=_cparams(("arbitrary", "arbitrary")),
        name="hgrn_scan",
    )(proj, proj, proj, proj, lb2)


def _hg_finish_kernel(o_ref, g_ref, ng_ref, h_ref):
    o = o_ref[...]
    y = o * lax.rsqrt(jnp.mean(o * o, axis=-1, keepdims=True) + NORM_EPS) * ng_ref[...]
    g = g_ref[...].astype(F32)
    h_ref[...] = (y * (g * jax.nn.sigmoid(g))).astype(h_ref.dtype)


def _hgrn_finish(o, proj, norm_g):
    t, d = o.shape
    return pl.pallas_call(
        _hg_finish_kernel,
        grid=(t // ROW_TILE,),
        in_specs=[
            pl.BlockSpec((ROW_TILE, d), lambda i: (i, 0)),
            pl.BlockSpec((ROW_TILE, d), lambda i: (i, 2)),
            pl.BlockSpec((1, d), lambda i: (0, 0)),
        ],
        out_specs=pl.BlockSpec((ROW_TILE, d), lambda i: (i, 0)),
        out_shape=jax.ShapeDtypeStruct((t, d), BF16),
        compiler_params=_cparams(("arbitrary",)),
        name="hgrn_finish",
    )(o, proj, norm_g.astype(F32).reshape(1, d))


def _moe_kernel(be_ref, nu_ref, x_ref, wg_ref, wu_ref, wd_ref, o_ref):
    b = pl.program_id(0)

    @pl.when(b < nu_ref[0])
    def _():
        x = x_ref[...]
        g = _dot(x, wg_ref[...])
        u = _dot(x, wu_ref[...])
        h = (g * jax.nn.sigmoid(g) * u).astype(BF16)
        o_ref[...] = _dot(h, wd_ref[...]).astype(o_ref.dtype)

    @pl.when(b >= nu_ref[0])
    def _():
        o_ref[...] = jnp.zeros(o_ref.shape, o_ref.dtype)


def _moe_experts(xb, blk_expert, n_used, w_gate, w_up, w_down):
    n_slots, d = xb.shape
    f = w_gate.shape[2]
    n_blocks = n_slots // MOE_TM
    grid_spec = pltpu.PrefetchScalarGridSpec(
        num_scalar_prefetch=2,
        grid=(n_blocks,),
        in_specs=[
            pl.BlockSpec((MOE_TM, d), lambda b, be, nu: (b, 0)),
            pl.BlockSpec((None, d, f), lambda b, be, nu: (be[b], 0, 0)),
            pl.BlockSpec((None, d, f), lambda b, be, nu: (be[b], 0, 0)),
            pl.BlockSpec((None, f, d), lambda b, be, nu: (be[b], 0, 0)),
        ],
        out_specs=pl.BlockSpec((MOE_TM, d), lambda b, be, nu: (b, 0)),
    )
    return pl.pallas_call(
        _moe_kernel,
        grid_spec=grid_spec,
        out_shape=jax.ShapeDtypeStruct((n_slots, d), BF16),
        compiler_params=_cparams(("arbitrary",)),
        name="moe_experts",
    )(blk_expert, n_used, xb, w_gate, w_up, w_down)


def _route(logits, b_router, n_blocks):
    t = logits.shape[0]
    s = jax.nn.sigmoid(logits)
    sel = s + b_router.astype(F32)
    grp_score = lax.top_k(sel.reshape(t, N_GROUPS, EXPERTS_PER_GROUP), 2)[0].sum(-1)
    g_best = jnp.argmax(grp_score, axis=-1)
    in_grp = (jnp.arange(N_EXPERTS) // EXPERTS_PER_GROUP)[None, :] == g_best[:, None]
    _, e_idx = lax.top_k(jnp.where(in_grp, sel, -jnp.inf), TOP_K)
    w = jnp.take_along_axis(s, e_idx, axis=1)
    w = w / jnp.sum(w, axis=-1, keepdims=True)
    flat_e = e_idx.reshape(-1).astype(jnp.int32)
    onehot = (flat_e[:, None] == jnp.arange(N_EXPERTS, dtype=jnp.int32)[None, :]).astype(jnp.int32)
    csum = jnp.cumsum(onehot, axis=0)
    rank = jnp.sum((csum - onehot) * onehot, axis=1)
    counts = csum[-1]
    pcounts = (counts + MOE_TM - 1) // MOE_TM * MOE_TM
    pends = jnp.cumsum(pcounts)
    pstarts = pends - pcounts
    dest = (pstarts[flat_e] + rank).astype(jnp.int32)
    flat_t = jnp.arange(t * TOP_K, dtype=jnp.int32) // TOP_K
    slot_tok = jnp.zeros((n_blocks * MOE_TM,), jnp.int32).at[dest].set(flat_t)
    blk_expert = jnp.minimum(
        jnp.searchsorted(pends, jnp.arange(n_blocks, dtype=jnp.int32) * MOE_TM, side='right'),
        N_EXPERTS - 1).astype(jnp.int32)
    n_used = (pends[-1] // MOE_TM).astype(jnp.int32).reshape(1)
    return w, dest.reshape(t, TOP_K), slot_tok, blk_expert, n_used


def _moe_res_kernel(xs_ref, y0_ref, y1_ref, w_ref, gate_ref, o_ref):
    w = w_ref[...]
    y = w[:, 0:1] * y0_ref[...].astype(F32) + w[:, 1:2] * y1_ref[...].astype(F32)
    o_ref[...] = xs_ref[...] + gate_ref[...] * y


def _moe_res_final_kernel(xs_ref, y0_ref, y1_ref, w_ref, gate_ref, fg_ref, o_ref):
    w = w_ref[...]
    y = w[:, 0:1] * y0_ref[...].astype(F32) + w[:, 1:2] * y1_ref[...].astype(F32)
    x = xs_ref[...] + gate_ref[...] * y
    o_ref[...] = x * lax.rsqrt(jnp.mean(x * x, axis=-1, keepdims=True) + NORM_EPS) * fg_ref[...]


def _moe_residual(xs, y0, y1, w, mods3, layer, n_batch, tpb):
    t, d = xs.shape
    nb = n_batch
    row = lambda i: (i, 0)
    return pl.pallas_call(
        _moe_res_kernel,
        grid=(t // ROW_TILE,),
        in_specs=[
            pl.BlockSpec((ROW_TILE, d), row),
            pl.BlockSpec((ROW_TILE, d), row),
            pl.BlockSpec((ROW_TILE, d), row),
            pl.BlockSpec((ROW_TILE, TOP_K), row),
            pl.BlockSpec((None, 1, d), lambda i: (layer * 8 + _row_sel(i, tpb, nb), 0, 5)),
        ],
        out_specs=pl.BlockSpec((ROW_TILE, d), row),
        out_shape=jax.ShapeDtypeStruct((t, d), F32),
        input_output_aliases={0: 0},
        compiler_params=_cparams(("arbitrary",)),
        name="moe_residual",
    )(xs, y0, y1, w, mods3)


def _moe_residual_final(xs, y0, y1, w, mods3, layer, final_g, n_batch, tpb):
    t, d = xs.shape
    lat = tpb - CTX_LEN // ROW_TILE
    src = lambda i: ((i // lat) * tpb + CTX_LEN // ROW_TILE + i % lat, 0)
    return pl.pallas_call(
        _moe_res_final_kernel,
        grid=(n_batch * lat,),
        in_specs=[
            pl.BlockSpec((ROW_TILE, d), src),
            pl.BlockSpec((ROW_TILE, d), src),
            pl.BlockSpec((ROW_TILE, d), src),
            pl.BlockSpec((ROW_TILE, TOP_K), src),
            pl.BlockSpec((None, 1, d), lambda i: (layer * 8 + i // lat, 0, 5)),
            pl.BlockSpec((1, d), lambda i: (0, 0)),
        ],
        out_specs=pl.BlockSpec((ROW_TILE, d), lambda i: (i, 0)),
        out_shape=jax.ShapeDtypeStruct((n_batch * lat * ROW_TILE, d), F32),
        compiler_params=_cparams(("arbitrary",)),
        name="moe_residual_final",
    )(xs, y0, y1, w, mods3, final_g.astype(F32).reshape(1, d))


def _rope_tables(n_batch, seq):
    quarter = DIFF_HEAD_DIM // 4
    tpos = jnp.arange(seq)
    rowp = (tpos // GRID_W).astype(F32)
    colp = (tpos % GRID_W).astype(F32)
    inv = ROPE_THETA ** (-jnp.arange(quarter, dtype=F32) / quarter)
    ang_r = rowp[:, None] * inv[None, :]
    ang_c = colp[:, None] * inv[None, :]
    cos = jnp.concatenate([jnp.cos(ang_r)] * 2 + [jnp.cos(ang_c)] * 2, axis=-1)
    sin = jnp.concatenate([-jnp.sin(ang_r), jnp.sin(ang_r), -jnp.sin(ang_c), jnp.sin(ang_c)], axis=-1)
    cos = jnp.concatenate([jnp.ones((CTX_LEN, LANES), F32), cos], axis=0)
    sin = jnp.concatenate([jnp.zeros((CTX_LEN, LANES), F32), sin], axis=0)
    return jnp.tile(cos, (n_batch, 1)), jnp.tile(sin, (n_batch, 1))


def kernel(x, c, ctx, c_ctx, ada_w, ada_b, norm_g, na_w_qkv, na_rpb, na_w_o, da_w_qkv, da_lambda, da_subln_g, da_w_o, hg_w_in, hg_lb, hg_norm_g, hg_w_o, moe_w_router, moe_b_router, moe_w_gate, moe_w_up, moe_w_down, final_g):
    n_batch, seq, d = x.shape
    tb = CTX_LEN + seq
    t = n_batch * tb
    tpb = tb // ROW_TILE
    assert d == D_MODEL and ctx.shape[1] == CTX_LEN and n_batch + 1 <= 8
    assert seq % DA_TQ == 0 and t % MM_TM == 0 and seq // GRID_W >= NA_WIN_ROWS

    xs = jnp.concatenate([ctx, x], axis=1).reshape(t, d).astype(F32)

    c_rows = jnp.zeros((8, d), F32).at[:n_batch].set(c.astype(F32)).at[n_batch].set(c_ctx.astype(F32))
    mods = _ada_mods(c_rows, ada_w.astype(F32), ada_b.astype(F32))
    mods3 = mods.reshape(DEPTH * 8, 1, 6 * d)
    g3 = norm_g.astype(F32).reshape(DEPTH * 2, 1, d)

    lb_p = jax.nn.softmax(hg_lb.astype(F32), axis=1)
    lb_all = jnp.cumsum(lb_p, axis=1) - lb_p[:, :1]

    wr = jnp.zeros((d, LANES), F32).at[:, :N_EXPERTS].set(moe_w_router.astype(F32))
    wr_hi = wr.astype(BF16)
    wr_lo = (wr - wr_hi.astype(F32)).astype(BF16)

    n_blocks = -(-(t * TOP_K) // MOE_TM) + N_EXPERTS
    rope = _rope_tables(n_batch, seq)

    counters = [0, 0, 0]
    out = None
    for i in range(DEPTH):
        kind = i % N_MIXERS
        j = counters[kind]
        counters[kind] += 1

        h = _norm_mod(xs, mods3, g3, i, 0, n_batch, tpb)[0]
        if kind == 0:
            qkv = _matmul(h, na_w_qkv[j])
            o = _na_attention(qkv, na_rpb[j], n_batch, tb)
            w_o = na_w_o[j]
        elif kind == 1:
            qkv = _matmul(h, da_w_qkv[j], rope=rope, n_rope_cols=2 * d)
            o = _diff_attention(qkv, da_lambda[j], da_subln_g[j], i, n_batch, tb)
            w_o = da_w_o[j]
        else:
            proj = _matmul(h, hg_w_in[j])
            o32 = _hgrn_scan(proj, lb_all[:, i, :], n_batch, tb)
            o = _hgrn_finish(o32, proj, hg_norm_g[j])
            w_o = hg_w_o[j]
        xs = _matmul_residual(o, w_o, xs, mods3, i, 2, n_batch, tpb)

        h2, logits = _norm_mod(xs, mods3, g3, i, 1, n_batch, tpb, router=(wr_hi, wr_lo))
        w, dest, slot_tok, blk_expert, n_used = _route(logits[:, :N_EXPERTS], moe_b_router, n_blocks)
        xb = jnp.take(h2, slot_tok, axis=0)
        yb = _moe_experts(xb, blk_expert, n_used, moe_w_gate[i].astype(BF16), moe_w_up[i].astype(BF16),
                          moe_w_down[i].astype(BF16))
        y0 = jnp.take(yb, dest[:, 0], axis=0)
        y1 = jnp.take(yb, dest[:, 1], axis=0)
        if i < DEPTH - 1:
            xs = _moe_residual(xs, y0, y1, w, mods3, i, n_batch, tpb)
        else:
            out = _moe_residual_final(xs, y0, y1, w, mods3, i, final_g, n_batch, tpb)
    return out.reshape(n_batch, seq, d).astype(x.dtype)
```

```python
import functools
import math

import numpy as np
import jax
import jax.numpy as jnp
from jax import lax
from jax.experimental import pallas as pl
from jax.experimental.pallas import tpu as pltpu

F32 = jnp.float32
BF16 = jnp.bfloat16

D_MODEL = 2048
DEPTH = 4
GRID_W = 64
CTX_LEN = 256
N_MIXERS = 3
NORM_EPS = 1e-6
ROPE_THETA = 10000.0

NA_HEADS = 16
NA_HEAD_DIM = 128
NA_WIN_ROWS = 8
NA_WIN_COLS = 16

DIFF_HEADS = 8
DIFF_HEAD_DIM = 128

HGRN_HEADS = 16
HGRN_DK = 128
HGRN_CHUNK = 64

N_EXPERTS = 16
N_GROUPS = 4
EXPERTS_PER_GROUP = 4
TOP_K = 2
D_EXPERT = 1024

LANES = 128
ROW_TILE = 256
MM_TM = 512
MM_TN = 1024
MOE_TM = 256
DA_TQ = 512
VMEM_LIMIT = 56 * 1024 * 1024

NA_GROUP = 4
NA_KROWS = 12
DA_SUB = 64
DA_TK = 512
HG_GROUP = 4

NEG_BIG = -1e30
LOG2E = 1.4426950408889634


def _cparams(sem):
    return pltpu.CompilerParams(dimension_semantics=sem, vmem_limit_bytes=VMEM_LIMIT)


def _dot(a, b):
    return jnp.dot(a, b, preferred_element_type=F32)


def _dot_nt(a, b):
    return lax.dot_general(a, b, (((1,), (1,)), ((), ())), preferred_element_type=F32)


def _dot_tn(a, b):
    return lax.dot_general(a, b, (((0,), (0,)), ((), ())), preferred_element_type=F32)


def _row_sel(sub, tiles_per_batch, n_batch):
    return jnp.where(sub % tiles_per_batch == 0, n_batch, sub // tiles_per_batch)


def _ada_kernel(c_ref, w_ref, b_ref, o_ref):
    c = c_ref[...]
    a = c * jax.nn.sigmoid(c)
    o_ref[...] = jnp.dot(a, w_ref[...], preferred_element_type=F32,
                         precision=lax.Precision.HIGHEST) + b_ref[...]


def _ada_mods(c_rows, ada_w, ada_b):
    depth, d, n = ada_w.shape
    tn = 1024
    return pl.pallas_call(
        _ada_kernel,
        grid=(depth, n // tn),
        in_specs=[
            pl.BlockSpec((8, d), lambda l, j: (0, 0)),
            pl.BlockSpec((None, d, tn), lambda l, j: (l, 0, j)),
            pl.BlockSpec((None, 1, tn), lambda l, j: (l, 0, j)),
        ],
        out_specs=pl.BlockSpec((None, 8, tn), lambda l, j: (l, 0, j)),
        out_shape=jax.ShapeDtypeStruct((depth, 8, n), F32),
        compiler_params=_cparams(("arbitrary", "arbitrary")),
        name="ada_mods",
    )(c_rows, ada_w, ada_b.reshape(depth, 1, n))


def _norm_mod_kernel(x_ref, g_ref, shift_ref, scale_ref, h_ref):
    x = x_ref[...]
    y = x * lax.rsqrt(jnp.mean(x * x, axis=-1, keepdims=True) + NORM_EPS)
    h = y * g_ref[...] * (1.0 + scale_ref[...]) + shift_ref[...]
    h_ref[...] = h.astype(h_ref.dtype)


def _norm_mod_router_kernel(x_ref, g_ref, shift_ref, scale_ref, whi_ref, wlo_ref, h_ref, lg_ref):
    x = x_ref[...]
    y = x * lax.rsqrt(jnp.mean(x * x, axis=-1, keepdims=True) + NORM_EPS)
    h = y * g_ref[...] * (1.0 + scale_ref[...]) + shift_ref[...]
    h_hi = h.astype(BF16)
    h_lo = (h - h_hi.astype(F32)).astype(BF16)
    h_ref[...] = h_hi
    lg_ref[...] = _dot(h_hi, whi_ref[...]) + _dot(h_hi, wlo_ref[...]) + _dot(h_lo, whi_ref[...])


def _norm_mod(xs, mods3, g3, layer, which, n_batch, tpb, router=None):
    t, d = xs.shape
    shift_k, scale_k = (0, 1) if which == 0 else (3, 4)
    nb = n_batch

    def mod_spec(k):
        return pl.BlockSpec((None, 1, d), lambda i: (layer * 8 + _row_sel(i, tpb, nb), 0, k))

    in_specs = [
        pl.BlockSpec((ROW_TILE, d), lambda i: (i, 0)),
        pl.BlockSpec((None, 1, d), lambda i: (layer * 2 + which, 0, 0)),
        mod_spec(shift_k),
        mod_spec(scale_k),
    ]
    args = [xs, g3, mods3, mods3]
    out_specs = [pl.BlockSpec((ROW_TILE, d), lambda i: (i, 0))]
    out_shape = [jax.ShapeDtypeStruct((t, d), BF16)]
    kern = _norm_mod_kernel
    if router is not None:
        in_specs += [pl.BlockSpec((d, LANES), lambda i: (0, 0))] * 2
        args += list(router)
        out_specs.append(pl.BlockSpec((ROW_TILE, LANES), lambda i: (i, 0)))
        out_shape.append(jax.ShapeDtypeStruct((t, LANES), F32))
        kern = _norm_mod_router_kernel
    return pl.pallas_call(
        kern,
        grid=(t // ROW_TILE,),
        in_specs=in_specs,
        out_specs=out_specs,
        out_shape=out_shape,
        compiler_params=_cparams(("arbitrary",)),
        name="norm_mod",
    )(*args)


def _rope_store(acc, cos_ref, sin_ref, o_ref):
    lane = lax.broadcasted_iota(jnp.int32, (1, LANES), 1)
    first = (lane % 64) < 32
    cos = cos_ref[...]
    sin = sin_ref[...]
    for c in range(acc.shape[1] // LANES):
        xc = acc[:, c * LANES:(c + 1) * LANES]
        sw = jnp.where(first, pltpu.roll(xc, 96, 1), pltpu.roll(xc, 32, 1))
        o_ref[:, c * LANES:(c + 1) * LANES] = (xc * cos + sw * sin).astype(o_ref.dtype)


def _mm_kernel(x_ref, w_ref, o_ref, wb_ref, *, n_q, q_scale):
    @pl.when(pl.program_id(1) == 0)
    def _():
        wb_ref[...] = w_ref[...].astype(BF16)

    acc = _dot(x_ref[...], wb_ref[...])
    j = pl.program_id(0)

    @pl.when(j < n_q)
    def _():
        o_ref[...] = (acc * q_scale).astype(o_ref.dtype)

    @pl.when(j >= n_q)
    def _():
        o_ref[...] = acc.astype(o_ref.dtype)


def _mm_rope_kernel(x_ref, w_ref, cos_ref, sin_ref, o_ref, wb_ref, *, n_q, q_scale, n_rope):
    @pl.when(pl.program_id(1) == 0)
    def _():
        wb_ref[...] = w_ref[...].astype(BF16)

    acc = _dot(x_ref[...], wb_ref[...])
    j = pl.program_id(0)

    @pl.when(j < n_q)
    def _():
        _rope_store(acc * q_scale, cos_ref, sin_ref, o_ref)

    @pl.when((j >= n_q) & (j < n_rope))
    def _():
        _rope_store(acc, cos_ref, sin_ref, o_ref)

    @pl.when(j >= n_rope)
    def _():
        o_ref[...] = acc.astype(o_ref.dtype)


def _mm_res_kernel(x_ref, w_ref, xs_ref, ga_ref, gb_ref, o_ref, wb_ref):
    @pl.when(pl.program_id(1) == 0)
    def _():
        wb_ref[...] = w_ref[...].astype(BF16)

    acc = _dot(x_ref[...], wb_ref[...])
    half = ROW_TILE
    o_ref[:half, :] = xs_ref[:half, :] + ga_ref[...] * acc[:half, :]
    o_ref[half:, :] = xs_ref[half:, :] + gb_ref[...] * acc[half:, :]


def _matmul(x, w, q_cols=0, q_scale=1.0, rope=None, n_rope_cols=0):
    t, k = x.shape
    n = w.shape[1]
    in_specs = [
        pl.BlockSpec((MM_TM, k), lambda j, i: (i, 0)),
        pl.BlockSpec((k, MM_TN), lambda j, i: (0, j)),
    ]
    args = [x, w]
    kern = functools.partial(_mm_kernel, n_q=q_cols // MM_TN, q_scale=q_scale)
    if rope is not None:
        in_specs += [pl.BlockSpec((MM_TM, LANES), lambda j, i: (i, 0))] * 2
        args += list(rope)
        kern = functools.partial(_mm_rope_kernel, n_q=q_cols // MM_TN, q_scale=q_scale,
                                 n_rope=n_rope_cols // MM_TN)
    return pl.pallas_call(
        kern,
        grid=(n // MM_TN, t // MM_TM),
        in_specs=in_specs,
        out_specs=pl.BlockSpec((MM_TM, MM_TN), lambda j, i: (i, j)),
        out_shape=jax.ShapeDtypeStruct((t, n), BF16),
        scratch_shapes=[pltpu.VMEM((k, MM_TN), BF16)],
        compiler_params=_cparams(("arbitrary", "arbitrary")),
        name="matmul",
    )(*args)


def _matmul_residual(x, w, xs, mods3, layer, gate_k, n_batch, tpb):
    t, k = x.shape
    n = w.shape[1]
    ncol = n // MM_TN
    nb = n_batch

    def gate_spec(half):
        return pl.BlockSpec(
            (None, 1, MM_TN),
            lambda j, i: (layer * 8 + _row_sel(2 * i + half, tpb, nb), 0, gate_k * ncol + j))

    return pl.pallas_call(
        _mm_res_kernel,
        grid=(ncol, t // MM_TM),
        in_specs=[
            pl.BlockSpec((MM_TM, k), lambda j, i: (i, 0)),
            pl.BlockSpec((k, MM_TN), lambda j, i: (0, j)),
            pl.BlockSpec((MM_TM, MM_TN), lambda j, i: (i, j)),
            gate_spec(0),
            gate_spec(1),
        ],
        out_specs=pl.BlockSpec((MM_TM, MM_TN), lambda j, i: (i, j)),
        out_shape=jax.ShapeDtypeStruct((t, n), F32),
        scratch_shapes=[pltpu.VMEM((k, MM_TN), BF16)],
        input_output_aliases={2: 0},
        compiler_params=_cparams(("arbitrary", "arbitrary")),
        name="matmul_residual",
    )(x, w, xs, mods3, mods3)


def _na_kernel(q_ref, k_ref, v_ref, tab_ref, o_ref, *, rows):
    nq = NA_GROUP * GRID_W
    nk = NA_KROWS * GRID_W
    half_win = NA_WIN_ROWS // 2
    left = lax.broadcasted_iota(jnp.int32, (1, LANES), 1) < GRID_W

    s = _dot_nt(q_ref[0:CTX_LEN, :], k_ref[0:CTX_LEN, :])
    p = jnp.exp2(s - jnp.max(s, axis=-1, keepdims=True))
    l = jnp.sum(p, axis=-1, keepdims=True)
    o_ref[0:CTX_LEN, :] = (_dot(p.astype(BF16), v_ref[0:CTX_LEN, :]) / l).astype(o_ref.dtype)

    def group_body(g, carry):
        r0 = g * NA_GROUP
        start = jnp.clip(r0 - half_win, 0, rows - NA_KROWS)
        q0 = pl.multiple_of(CTX_LEN + r0 * GRID_W, nq)
        k0 = pl.multiple_of(CTX_LEN + start * GRID_W, GRID_W)
        q = q_ref[pl.ds(q0, nq), :]
        s_lat = _dot_nt(q, k_ref[pl.ds(k0, nk), :])
        s_ctx = _dot_nt(q, k_ref[0:CTX_LEN, :])
        p_lat, p_ctx, l_all = [], [], []
        for i in range(NA_GROUP):
            r = r0 + i
            rs = jnp.clip(r - half_win, 0, rows - NA_WIN_ROWS)
            tiles = []
            for jp in range(NA_KROWS // 2):
                kr = start + 2 * jp
                ok0 = ((kr >= rs) & (kr < rs + NA_WIN_ROWS)).astype(jnp.int32)
                ok1 = ((kr + 1 >= rs) & (kr + 1 < rs + NA_WIN_ROWS)).astype(jnp.int32)
                e = jnp.clip(kr - r + (NA_WIN_ROWS - 1) + 2, 0, 2 * NA_WIN_ROWS + 1)
                ok = jnp.where(left, ok0, ok1) > 0
                tiles.append(jnp.where(ok, tab_ref[e], NEG_BIG))
            sl = s_lat[i * GRID_W:(i + 1) * GRID_W, :] + jnp.concatenate(tiles, axis=1)
            sc = s_ctx[i * GRID_W:(i + 1) * GRID_W, :]
            m = jnp.maximum(jnp.max(sl, axis=-1, keepdims=True), jnp.max(sc, axis=-1, keepdims=True))
            pl_i = jnp.exp2(sl - m)
            pc_i = jnp.exp2(sc - m)
            l_all.append(jnp.sum(pl_i, axis=-1, keepdims=True) + jnp.sum(pc_i, axis=-1, keepdims=True))
            p_lat.append(pl_i.astype(BF16))
            p_ctx.append(pc_i.astype(BF16))
        o = (_dot(jnp.concatenate(p_lat, axis=0), v_ref[pl.ds(k0, nk), :])
             + _dot(jnp.concatenate(p_ctx, axis=0), v_ref[0:CTX_LEN, :]))
        o_ref[pl.ds(q0, nq), :] = (o / jnp.concatenate(l_all, axis=0)).astype(o_ref.dtype)
        return carry

    lax.fori_loop(0, rows // NA_GROUP, group_body, 0)


def _na_bias_table(rpb):
    h = rpb.shape[0]
    qc = np.arange(GRID_W)[:, None]
    kc = np.arange(GRID_W)[None, :]
    cs = np.clip(qc - NA_WIN_COLS // 2, 0, GRID_W - NA_WIN_COLS)
    col_ok = (kc >= cs) & (kc < cs + NA_WIN_COLS)
    dc = np.clip(kc - qc + (NA_WIN_COLS - 1), 0, 2 * NA_WIN_COLS - 2)
    a = jnp.take(rpb.astype(F32), jnp.asarray(dc.reshape(-1)), axis=2)
    a = a.reshape(h, 2 * NA_WIN_ROWS - 1, GRID_W, GRID_W) * LOG2E
    a = jnp.where(jnp.asarray(col_ok)[None, None], a, NEG_BIG)
    a = jnp.pad(a, ((0, 0), (2, 2), (0, 0), (0, 0)), constant_values=NEG_BIG)
    return jnp.concatenate([a[:, :-1], a[:, 1:]], axis=-1)


def _na_attention(qkv, rpb, n_batch, tb):
    t = qkv.shape[0]
    rows = (tb - CTX_LEN) // GRID_W
    assert rows % NA_GROUP == 0 and rows >= NA_KROWS
    table = _na_bias_table(rpb)
    hh = NA_HEADS
    return pl.pallas_call(
        functools.partial(_na_kernel, rows=rows),
        grid=(hh, n_batch),
        in_specs=[
            pl.BlockSpec((tb, LANES), lambda h, b: (b, h)),
            pl.BlockSpec((tb, LANES), lambda h, b: (b, hh + h)),
            pl.BlockSpec((tb, LANES), lambda h, b: (b, 2 * hh + h)),
            pl.BlockSpec((None, 2 * NA_WIN_ROWS + 2, GRID_W, LANES), lambda h, b: (h, 0, 0, 0)),
        ],
        out_specs=pl.BlockSpec((tb, LANES), lambda h, b: (b, h)),
        out_shape=jax.ShapeDtypeStruct((t, D_MODEL), BF16),
        compiler_params=_cparams(("arbitrary", "arbitrary")),
        name="na_attention",
    )(qkv, qkv, qkv, table)


def _da_kernel(lam_ref, g_ref, q_ref, k_ref, v_ref, o_ref, acc_ref, m_ref, l_ref, *, lam_init, n_kv_chunks):
    d = DIFF_HEAD_DIM
    i = pl.program_id(2)
    lam = lam_ref[...]
    lam_full = (jnp.exp(jnp.sum(lam[0:1] * lam[1:2], axis=-1, keepdims=True))
                - jnp.exp(jnp.sum(lam[2:3] * lam[3:4], axis=-1, keepdims=True)) + lam_init)

    def tile(q0, n_sub, n_lat_chunks):
        nq = n_sub * DA_SUB
        acc_ref[:, 0:nq, :] = jnp.zeros((2, nq, 2 * d), F32)
        m_ref[:, 0:nq, :] = jnp.full((2, nq, LANES), NEG_BIG, F32)
        l_ref[:, 0:nq, :] = jnp.zeros((2, nq, LANES), F32)

        def chunk(k0, nk):
            k = k_ref[pl.ds(k0, nk), :]
            v = v_ref[pl.ds(k0, nk), :]
            for sb in range(n_sub):
                rows = slice(sb * DA_SUB, (sb + 1) * DA_SUB)
                qs = q0 + sb * DA_SUB
                q = q_ref[pl.ds(qs if isinstance(qs, int) else pl.multiple_of(qs, DA_SUB), DA_SUB), :]
                for c in range(2):
                    s = _dot_nt(q[:, c * d:(c + 1) * d], k[:, c * d:(c + 1) * d])
                    blocks = [s[:, j * LANES:(j + 1) * LANES] for j in range(nk // LANES)]
                    m_cur = functools.reduce(jnp.maximum, blocks)
                    m_prev = m_ref[c, rows, :]
                    m_new = jnp.maximum(m_prev, jnp.max(m_cur, axis=-1, keepdims=True))
                    alpha = jnp.exp2(m_prev - m_new)
                    p = [jnp.exp2(blk - m_new) for blk in blocks]
                    l_ref[c, rows, :] = alpha * l_ref[c, rows, :] + functools.reduce(jnp.add, p)
                    pv = _dot(jnp.concatenate(p, axis=1).astype(BF16), v)
                    acc_ref[c, rows, 0:LANES] = alpha * acc_ref[c, rows, 0:LANES] + pv[:, :LANES]
                    acc_ref[c, rows, LANES:] = alpha * acc_ref[c, rows, LANES:] + pv[:, LANES:]
                    m_ref[c, rows, :] = m_new

        chunk(0, CTX_LEN)
        if n_lat_chunks:
            def body(n, carry):
                chunk(pl.multiple_of(CTX_LEN + n * DA_TK, ROW_TILE), DA_TK)
                return carry

            lax.fori_loop(0, n_lat_chunks, body, 0)

        r0 = 1.0 / jnp.sum(l_ref[0, 0:nq, :], axis=-1, keepdims=True)
        r1 = lam_full / jnp.sum(l_ref[1, 0:nq, :], axis=-1, keepdims=True)
        o_a = acc_ref[0, 0:nq, 0:LANES] * r0 - acc_ref[1, 0:nq, 0:LANES] * r1
        o_b = acc_ref[0, 0:nq, LANES:] * r0 - acc_ref[1, 0:nq, LANES:] * r1
        ms = (jnp.sum(o_a * o_a, axis=-1, keepdims=True) + jnp.sum(o_b * o_b, axis=-1, keepdims=True)) / (2 * d)
        inv = lax.rsqrt(ms + NORM_EPS) * (1.0 - lam_init)
        g = g_ref[...]
        o_ref[pl.ds(q0, nq), 0:LANES] = (o_a * inv * g[:, :LANES]).astype(o_ref.dtype)
        o_ref[pl.ds(q0, nq), LANES:] = (o_b * inv * g[:, LANES:]).astype(o_ref.dtype)

    @pl.when(i == 0)
    def _():
        tile(0, CTX_LEN // DA_SUB, 0)

    @pl.when(i > 0)
    def _():
        tile(pl.multiple_of(CTX_LEN + (i - 1) * DA_TQ, ROW_TILE), DA_TQ // DA_SUB, n_kv_chunks)


def _diff_attention(qkv, lam, subln_g, layer_idx, n_batch, tb):
    t = qkv.shape[0]
    hh = DIFF_HEADS
    w = 2 * DIFF_HEAD_DIM
    n_lat = (tb - CTX_LEN) // DA_TQ
    lam_init = 0.8 - 0.6 * math.exp(-0.3 * layer_idx)
    return pl.pallas_call(
        functools.partial(_da_kernel, lam_init=lam_init, n_kv_chunks=(tb - CTX_LEN) // DA_TK),
        grid=(n_batch, hh, 1 + n_lat),
        in_specs=[
            pl.BlockSpec((4, DIFF_HEAD_DIM), lambda b, h, i: (0, 0)),
            pl.BlockSpec((1, w), lambda b, h, i: (0, 0)),
            pl.BlockSpec((tb, w), lambda b, h, i: (b, h)),
            pl.BlockSpec((tb, w), lambda b, h, i: (b, hh + h)),
            pl.BlockSpec((tb, w), lambda b, h, i: (b, 2 * hh + h)),
        ],
        out_specs=pl.BlockSpec((tb, w), lambda b, h, i: (b, h)),
        out_shape=jax.ShapeDtypeStruct((t, D_MODEL), BF16),
        scratch_shapes=[
            pltpu.VMEM((2, DA_TQ, w), F32),
            pltpu.VMEM((2, DA_TQ, LANES), F32),
            pltpu.VMEM((2, DA_TQ, LANES), F32),
        ],
        compiler_params=_cparams(("arbitrary", "arbitrary", "arbitrary")),
        name="diff_attention",
    )(lam.astype(F32), subln_g.astype(F32).reshape(1, w), qkv, qkv, qkv)


def _hg_kernel(q_ref, i_ref, ff_ref, fb_ref, lb_ref, o_ref, stf_ref, stb_ref, *, n_groups):
    c = HGRN_CHUNK
    rg = HG_GROUP * c
    dk = HGRN_DK
    o_ref[...] = jnp.zeros(o_ref.shape, F32)
    stf_ref[...] = jnp.zeros(stf_ref.shape, F32)
    stb_ref[...] = jnp.zeros(stb_ref.shape, F32)
    row = lax.broadcasted_iota(jnp.int32, (rg, rg), 0)
    col = lax.broadcasted_iota(jnp.int32, (rg, rg), 1)
    same = (row // c) == (col // c)
    lower = same & (row >= col)
    upper = same & (col >= row)
    lower_b = lower.astype(BF16)
    upper_b = upper.astype(BF16)
    chunk_of_row = lax.broadcasted_iota(jnp.int32, (rg, 1), 0) // c
    lb_all = lb_ref[...]

    def per_chunk_rows(b, r):
        return jnp.concatenate(
            [jnp.broadcast_to(b[g * c + r:g * c + r + 1, :], (c, dk)) for g in range(HG_GROUP)], axis=0)

    def group(gidx, f_ref, lb, mask, mask_b, st_ref, end_row, order):
        r0 = pl.multiple_of(gidx * rg, rg)
        qr = q_ref[pl.ds(r0, rg), :].astype(F32)
        q = qr * jax.nn.sigmoid(qr) * (dk ** -0.5)
        v = i_ref[pl.ds(r0, rg), :]
        fr = f_ref[pl.ds(r0, rg), :].astype(F32)
        fg = lb + (1.0 - lb) * jax.nn.sigmoid(fr)
        k = 1.0 - fg
        lf = jnp.log(fg)
        hi = lf.astype(BF16)
        rem = lf - hi.astype(F32)
        mid = rem.astype(BF16)
        lo = (rem - mid.astype(F32)).astype(BF16)
        cs = _dot(mask_b, jnp.concatenate([hi, mid, lo], axis=1))
        bcum = cs[:, 0:dk] + cs[:, dk:2 * dk] + cs[:, 2 * dk:]
        b_end = per_chunk_rows(bcum, end_row)
        b_mid = per_chunk_rows(bcum, c // 2)
        qt = (q * jnp.exp(bcum - b_mid)).astype(BF16)
        kt = (k * jnp.exp(b_mid - bcum)).astype(BF16)
        a = jnp.where(mask, _dot_nt(qt, kt), 0.0).astype(BF16)
        o_intra = _dot(a, v)
        k_end = k * jnp.exp(b_end - bcum)
        k_exp = jnp.concatenate(
            [jnp.where(chunk_of_row == g, k_end, 0.0) for g in range(HG_GROUP)], axis=1).astype(BF16)
        u = _dot_tn(v, k_exp)
        st = st_ref[...]
        st_in = [None] * HG_GROUP
        for g in order:
            st_in[g] = st.astype(BF16)
            st = st * jnp.exp(bcum[g * c + end_row:g * c + end_row + 1, :]) + u[:, g * dk:(g + 1) * dk]
        st_ref[...] = st
        oi = _dot_nt((q * jnp.exp(bcum)).astype(BF16), jnp.concatenate(st_in, axis=0))
        o_inter = jnp.concatenate(
            [oi[g * c:(g + 1) * c, g * dk:(g + 1) * dk] for g in range(HG_GROUP)], axis=0)
        o_ref[pl.ds(r0, rg), :] += o_intra + o_inter

    fwd_order = list(range(HG_GROUP))

    def body(n, carry):
        group(n, ff_ref, lb_all[0:1, :], lower, lower_b, stf_ref, c - 1, fwd_order)
        nb = jnp.where(n == 0, 0, n_groups - n)
        group(nb, fb_ref, lb_all[1:2, :], upper, upper_b, stb_ref, 0, fwd_order[::-1])
        return carry

    lax.fori_loop(0, n_groups, body, 0)


def _hgrn_scan(proj, lb2, n_batch, tb):
    t = proj.shape[0]
    hh = HGRN_HEADS
    rg = HG_GROUP * HGRN_CHUNK
    assert CTX_LEN == rg and tb % rg == 0

    def col(k):
        return pl.BlockSpec((tb, LANES), lambda b, h: (b, k * hh + h))

    return pl.pallas_call(
        functools.partial(_hg_kernel, n_groups=tb // rg),
        grid=(n_batch, hh),
        in_specs=[col(0), col(1), col(3), col(4), pl.BlockSpec((2, LANES), lambda b, h: (0, h))],
        out_specs=pl.BlockSpec((tb, LANES), lambda b, h: (b, h)),
        out_shape=jax.ShapeDtypeStruct((t, D_MODEL), F32),
        scratch_shapes=[pltpu.VMEM((LANES, LANES), F32), pltpu.VMEM((LANES, LANES), F32)],
        compiler_params=_cparams(("arbitrary", "arbitrary")),
        name="hgrn_scan",
    )(proj, proj, proj, proj, lb2)


def _hg_finish_kernel(o_ref, g_ref, ng_ref, h_ref):
    o = o_ref[...]
    y = o * lax.rsqrt(jnp.mean(o * o, axis=-1, keepdims=True) + NORM_EPS) * ng_ref[...]
    g = g_ref[...].astype(F32)
    h_ref[...] = (y * (g * jax.nn.sigmoid(g))).astype(h_ref.dtype)


def _hgrn_finish(o, proj, norm_g):
    t, d = o.shape
    return pl.pallas_call(
        _hg_finish_kernel,
        grid=(t // ROW_TILE,),
        in_specs=[
            pl.BlockSpec((ROW_TILE, d), lambda i: (i, 0)),
            pl.BlockSpec((ROW_TILE, d), lambda i: (i, 2)),
            pl.BlockSpec((1, d), lambda i: (0, 0)),
        ],
        out_specs=pl.BlockSpec((ROW_TILE, d), lambda i: (i, 0)),
        out_shape=jax.ShapeDtypeStruct((t, d), BF16),
        compiler_params=_cparams(("arbitrary",)),
        name="hgrn_finish",
    )(o, proj, norm_g.astype(F32).reshape(1, d))


def _moe_kernel(be_ref, nu_ref, x_ref, wg_ref, wu_ref, wd_ref, o_ref):
    b = pl.program_id(0)

    @pl.when(b < nu_ref[0])
    def _():
        x = x_ref[...]
        g = _dot(x, wg_ref[...])
        u = _dot(x, wu_ref[...])
        h = (g * jax.nn.sigmoid(g) * u).astype(BF16)
        o_ref[...] = _dot(h, wd_ref[...]).astype(o_ref.dtype)

    @pl.when(b >= nu_ref[0])
    def _():
        o_ref[...] = jnp.zeros(o_ref.shape, o_ref.dtype)


def _moe_experts(xb, blk_expert, n_used, w_gate, w_up, w_down, layer):
    n_slots, d = xb.shape
    f = w_gate.shape[3]
    n_blocks = n_slots // MOE_TM
    grid_spec = pltpu.PrefetchScalarGridSpec(
        num_scalar_prefetch=2,
        grid=(n_blocks,),
        in_specs=[
            pl.BlockSpec((MOE_TM, d), lambda b, be, nu: (b, 0)),
            pl.BlockSpec((None, None, d, f), lambda b, be, nu: (layer, be[b], 0, 0)),
            pl.BlockSpec((None, None, d, f), lambda b, be, nu: (layer, be[b], 0, 0)),
            pl.BlockSpec((None, None, f, d), lambda b, be, nu: (layer, be[b], 0, 0)),
        ],
        out_specs=pl.BlockSpec((MOE_TM, d), lambda b, be, nu: (b, 0)),
    )
    return pl.pallas_call(
        _moe_kernel,
        grid_spec=grid_spec,
        out_shape=jax.ShapeDtypeStruct((n_slots, d), BF16),
        compiler_params=_cparams(("arbitrary",)),
        name="moe_experts",
    )(blk_expert, n_used, xb, w_gate, w_up, w_down)


def _route(logits, b_router, n_blocks):
    t = logits.shape[0]
    s = jax.nn.sigmoid(logits)
    sel = s + b_router.astype(F32)

    def top2(vals):
        ids = jnp.arange(vals.shape[-1], dtype=jnp.int32)
        i1 = jnp.argmax(vals, axis=-1).astype(jnp.int32)
        v1 = jnp.max(vals, axis=-1)
        rest = jnp.where(ids == i1[..., None], -jnp.inf, vals)
        i2 = jnp.argmax(rest, axis=-1).astype(jnp.int32)
        return v1, jnp.max(rest, axis=-1), i1, i2

    g1, g2, _, _ = top2(sel.reshape(t, N_GROUPS, EXPERTS_PER_GROUP))
    g_best = jnp.argmax(g1 + g2, axis=-1)
    in_grp = (jnp.arange(N_EXPERTS) // EXPERTS_PER_GROUP)[None, :] == g_best[:, None]
    _, _, e1, e2 = top2(jnp.where(in_grp, sel, -jnp.inf))
    e_idx = jnp.stack([e1, e2], axis=-1)
    w = jnp.take_along_axis(s, e_idx, axis=1)
    w = w / jnp.sum(w, axis=-1, keepdims=True)
    flat_e = e_idx.reshape(-1).astype(jnp.int32)
    onehot = (flat_e[:, None] == jnp.arange(N_EXPERTS, dtype=jnp.int32)[None, :]).astype(jnp.int32)
    csum = jnp.cumsum(onehot, axis=0)
    rank = jnp.sum((csum - onehot) * onehot, axis=1)
    counts = csum[-1]
    pcounts = (counts + MOE_TM - 1) // MOE_TM * MOE_TM
    pends = jnp.cumsum(pcounts)
    pstarts = pends - pcounts
    dest = (pstarts[flat_e] + rank).astype(jnp.int32)
    flat_t = jnp.arange(t * TOP_K, dtype=jnp.int32) // TOP_K
    slot_tok = jnp.zeros((n_blocks * MOE_TM,), jnp.int32).at[dest].set(flat_t)
    blk_expert = jnp.minimum(
        jnp.searchsorted(pends, jnp.arange(n_blocks, dtype=jnp.int32) * MOE_TM, side='right'),
        N_EXPERTS - 1).astype(jnp.int32)
    n_used = (pends[-1] // MOE_TM).astype(jnp.int32).reshape(1)
    return w, dest.reshape(t, TOP_K), slot_tok, blk_expert, n_used


def _moe_res_kernel(xs_ref, y0_ref, y1_ref, w_ref, gate_ref, o_ref):
    w = w_ref[...]
    y = w[:, 0:1] * y0_ref[...].astype(F32) + w[:, 1:2] * y1_ref[...].astype(F32)
    o_ref[...] = xs_ref[...] + gate_ref[...] * y


def _moe_res_final_kernel(xs_ref, y0_ref, y1_ref, w_ref, gate_ref, fg_ref, o_ref):
    w = w_ref[...]
    y = w[:, 0:1] * y0_ref[...].astype(F32) + w[:, 1:2] * y1_ref[...].astype(F32)
    x = xs_ref[...] + gate_ref[...] * y
    o_ref[...] = x * lax.rsqrt(jnp.mean(x * x, axis=-1, keepdims=True) + NORM_EPS) * fg_ref[...]


def _moe_residual(xs, y0, y1, w, mods3, layer, n_batch, tpb):
    t, d = xs.shape
    nb = n_batch
    row = lambda i: (i, 0)
    return pl.pallas_call(
        _moe_res_kernel,
        grid=(t // ROW_TILE,),
        in_specs=[
            pl.BlockSpec((ROW_TILE, d), row),
            pl.BlockSpec((ROW_TILE, d), row),
            pl.BlockSpec((ROW_TILE, d), row),
            pl.BlockSpec((ROW_TILE, TOP_K), row),
            pl.BlockSpec((None, 1, d), lambda i: (layer * 8 + _row_sel(i, tpb, nb), 0, 5)),
        ],
        out_specs=pl.BlockSpec((ROW_TILE, d), row),
        out_shape=jax.ShapeDtypeStruct((t, d), F32),
        input_output_aliases={0: 0},
        compiler_params=_cparams(("arbitrary",)),
        name="moe_residual",
    )(xs, y0, y1, w, mods3)


def _moe_residual_final(xs, y0, y1, w, mods3, layer, final_g, n_batch, tpb):
    t, d = xs.shape
    lat = tpb - CTX_LEN // ROW_TILE
    src = lambda i: ((i // lat) * tpb + CTX_LEN // ROW_TILE + i % lat, 0)
    return pl.pallas_call(
        _moe_res_final_kernel,
        grid=(n_batch * lat,),
        in_specs=[
            pl.BlockSpec((ROW_TILE, d), src),
            pl.BlockSpec((ROW_TILE, d), src),
            pl.BlockSpec((ROW_TILE, d), src),
            pl.BlockSpec((ROW_TILE, TOP_K), src),
            pl.BlockSpec((None, 1, d), lambda i: (layer * 8 + i // lat, 0, 5)),
            pl.BlockSpec((1, d), lambda i: (0, 0)),
        ],
        out_specs=pl.BlockSpec((ROW_TILE, d), lambda i: (i, 0)),
        out_shape=jax.ShapeDtypeStruct((n_batch * lat * ROW_TILE, d), F32),
        compiler_params=_cparams(("arbitrary",)),
        name="moe_residual_final",
    )(xs, y0, y1, w, mods3, final_g.astype(F32).reshape(1, d))


def _rope_tables(n_batch, seq):
    quarter = DIFF_HEAD_DIM // 4
    tpos = jnp.arange(seq)
    rowp = (tpos // GRID_W).astype(F32)
    colp = (tpos % GRID_W).astype(F32)
    inv = ROPE_THETA ** (-jnp.arange(quarter, dtype=F32) / quarter)
    ang_r = rowp[:, None] * inv[None, :]
    ang_c = colp[:, None] * inv[None, :]
    cos = jnp.concatenate([jnp.cos(ang_r)] * 2 + [jnp.cos(ang_c)] * 2, axis=-1)
    sin = jnp.concatenate([-jnp.sin(ang_r), jnp.sin(ang_r), -jnp.sin(ang_c), jnp.sin(ang_c)], axis=-1)
    cos = jnp.concatenate([jnp.ones((CTX_LEN, LANES), F32), cos], axis=0)
    sin = jnp.concatenate([jnp.zeros((CTX_LEN, LANES), F32), sin], axis=0)
    return jnp.tile(cos, (n_batch, 1)), jnp.tile(sin, (n_batch, 1))


def kernel(x, c, ctx, c_ctx, ada_w, ada_b, norm_g, na_w_qkv, na_rpb, na_w_o, da_w_qkv, da_lambda, da_subln_g, da_w_o, hg_w_in, hg_lb, hg_norm_g, hg_w_o, moe_w_router, moe_b_router, moe_w_gate, moe_w_up, moe_w_down, final_g):
    n_batch, seq, d = x.shape
    tb = CTX_LEN + seq
    t = n_batch * tb
    tpb = tb // ROW_TILE
    assert d == D_MODEL and ctx.shape[1] == CTX_LEN and n_batch + 1 <= 8
    assert seq % DA_TQ == 0 and t % MM_TM == 0 and seq // GRID_W >= NA_WIN_ROWS

    xs = jnp.concatenate([ctx, x], axis=1).reshape(t, d).astype(F32)

    c_rows = jnp.zeros((8, d), F32).at[:n_batch].set(c.astype(F32)).at[n_batch].set(c_ctx.astype(F32))
    mods = _ada_mods(c_rows, ada_w.astype(F32), ada_b.astype(F32))
    mods3 = mods.reshape(DEPTH * 8, 1, 6 * d)
    g3 = norm_g.astype(F32).reshape(DEPTH * 2, 1, d)

    lb_p = jax.nn.softmax(hg_lb.astype(F32), axis=1)
    lb_all = jnp.cumsum(lb_p, axis=1) - lb_p[:, :1]

    wr = jnp.zeros((d, LANES), F32).at[:, :N_EXPERTS].set(moe_w_router.astype(F32))
    wr_hi = wr.astype(BF16)
    wr_lo = (wr - wr_hi.astype(F32)).astype(BF16)

    n_blocks = -(-(t * TOP_K) // MOE_TM) + N_EXPERTS
    rope = _rope_tables(n_batch, seq)
    wg_b = moe_w_gate.astype(BF16)
    wu_b = moe_w_up.astype(BF16)
    wd_b = moe_w_down.astype(BF16)
    na_qs = NA_HEAD_DIM ** -0.5 * LOG2E
    da_qs = DIFF_HEAD_DIM ** -0.5 * LOG2E

    counters = [0, 0, 0]
    out = None
    for i in range(DEPTH):
        kind = i % N_MIXERS
        j = counters[kind]
        counters[kind] += 1

        h = _norm_mod(xs, mods3, g3, i, 0, n_batch, tpb)[0]
        if kind == 0:
            qkv = _matmul(h, na_w_qkv[j], q_cols=d, q_scale=na_qs)
            o = _na_attention(qkv, na_rpb[j], n_batch, tb)
            w_o = na_w_o[j]
        elif kind == 1:
            qkv = _matmul(h, da_w_qkv[j], q_cols=d, q_scale=da_qs, rope=rope, n_rope_cols=2 * d)
            o = _diff_attention(qkv, da_lambda[j], da_subln_g[j], i, n_batch, tb)
            w_o = da_w_o[j]
        else:
            proj = _matmul(h, hg_w_in[j])
            o32 = _hgrn_scan(proj, lb_all[:, i, :], n_batch, tb)
            o = _hgrn_finish(o32, proj, hg_norm_g[j])
            w_o = hg_w_o[j]
        xs = _matmul_residual(o, w_o, xs, mods3, i, 2, n_batch, tpb)

        h2, logits = _norm_mod(xs, mods3, g3, i, 1, n_batch, tpb, router=(wr_hi, wr_lo))
        w, dest, slot_tok, blk_expert, n_used = _route(logits[:, :N_EXPERTS], moe_b_router, n_blocks)
        xb = h2.at[slot_tok].get(mode="promise_in_bounds")
        yb = _moe_experts(xb, blk_expert, n_used, wg_b, wu_b, wd_b, i)
        y0 = yb.at[dest[:, 0]].get(mode="promise_in_bounds")
        y1 = yb.at[dest[:, 1]].get(mode="promise_in_bounds")
        if i < DEPTH - 1:
            xs = _moe_residual(xs, y0, y1, w, mods3, i, n_batch, tpb)
        else:
            out = _moe_residual_final(xs, y0, y1, w, mods3, i, final_g, n_batch, tpb)
    return out.reshape(n_batch, seq, d).astype(x.dtype)
```

```python
import functools
import math

import numpy as np
import jax
import jax.numpy as jnp
from jax import lax
from jax.experimental import pallas as pl
from jax.experimental.pallas import tpu as pltpu

F32 = jnp.float32
BF16 = jnp.bfloat16

D_MODEL = 2048
DEPTH = 4
GRID_W = 64
CTX_LEN = 256
N_MIXERS = 3
NORM_EPS = 1e-6
ROPE_THETA = 10000.0

NA_HEADS = 16
NA_HEAD_DIM = 128
NA_WIN_ROWS = 8
NA_WIN_COLS = 16

DIFF_HEADS = 8
DIFF_HEAD_DIM = 128

HGRN_HEADS = 16
HGRN_DK = 128
HGRN_CHUNK = 64

N_EXPERTS = 16
N_GROUPS = 4
EXPERTS_PER_GROUP = 4
TOP_K = 2
D_EXPERT = 1024

LANES = 128
ROW_TILE = 256
MM_TM = 512
MM_TN = 1024
MOE_TM = 256
DA_TQ = 512
VMEM_LIMIT = 56 * 1024 * 1024

NA_GROUP = 4
NA_KROWS = 12
DA_SUB = 64
DA_TK = 512
HG_GROUP = 4
MOE_CAST_SLABS = 8

NEG_BIG = -1e30
LOG2E = 1.4426950408889634


def _cparams(sem):
    return pltpu.CompilerParams(dimension_semantics=sem, vmem_limit_bytes=VMEM_LIMIT)


def _dot(a, b):
    return jnp.dot(a, b, preferred_element_type=F32)


def _dot_nt(a, b):
    return lax.dot_general(a, b, (((1,), (1,)), ((), ())), preferred_element_type=F32)


def _dot_tn(a, b):
    return lax.dot_general(a, b, (((0,), (0,)), ((), ())), preferred_element_type=F32)


def _row_sel(sub, tiles_per_batch, n_batch):
    return jnp.where(sub % tiles_per_batch == 0, n_batch, sub // tiles_per_batch)


def _ada_kernel(c_ref, w_ref, b_ref, o_ref):
    c = c_ref[...]
    a = c * jax.nn.sigmoid(c)
    o_ref[...] = jnp.dot(a, w_ref[...], preferred_element_type=F32,
                         precision=lax.Precision.HIGHEST) + b_ref[...]


def _ada_mods(c_rows, ada_w, ada_b):
    depth, d, n = ada_w.shape
    tn = 1024
    return pl.pallas_call(
        _ada_kernel,
        grid=(depth, n // tn),
        in_specs=[
            pl.BlockSpec((8, d), lambda l, j: (0, 0)),
            pl.BlockSpec((None, d, tn), lambda l, j: (l, 0, j)),
            pl.BlockSpec((None, 1, tn), lambda l, j: (l, 0, j)),
        ],
        out_specs=pl.BlockSpec((None, 8, tn), lambda l, j: (l, 0, j)),
        out_shape=jax.ShapeDtypeStruct((depth, 8, n), F32),
        compiler_params=_cparams(("arbitrary", "arbitrary")),
        name="ada_mods",
    )(c_rows, ada_w, ada_b.reshape(depth, 1, n))


def _norm_mod_kernel(x_ref, g_ref, shift_ref, scale_ref, h_ref):
    x = x_ref[...]
    y = x * lax.rsqrt(jnp.mean(x * x, axis=-1, keepdims=True) + NORM_EPS)
    h = y * g_ref[...] * (1.0 + scale_ref[...]) + shift_ref[...]
    h_ref[...] = h.astype(h_ref.dtype)


def _norm_mod_router_kernel(x_ref, g_ref, shift_ref, scale_ref, whi_ref, wlo_ref, h_ref, lg_ref):
    x = x_ref[...]
    y = x * lax.rsqrt(jnp.mean(x * x, axis=-1, keepdims=True) + NORM_EPS)
    h = y * g_ref[...] * (1.0 + scale_ref[...]) + shift_ref[...]
    h_hi = h.astype(BF16)
    h_lo = (h - h_hi.astype(F32)).astype(BF16)
    h_ref[...] = h_hi
    lg_ref[...] = _dot(h_hi, whi_ref[...]) + _dot(h_hi, wlo_ref[...]) + _dot(h_lo, whi_ref[...])


def _norm_mod(xs, mods3, g3, layer, which, n_batch, tpb, router=None):
    t, d = xs.shape
    shift_k, scale_k = (0, 1) if which == 0 else (3, 4)
    nb = n_batch

    def mod_spec(k):
        return pl.BlockSpec((None, 1, d), lambda i: (layer * 8 + _row_sel(i, tpb, nb), 0, k))

    in_specs = [
        pl.BlockSpec((ROW_TILE, d), lambda i: (i, 0)),
        pl.BlockSpec((None, 1, d), lambda i: (layer * 2 + which, 0, 0)),
        mod_spec(shift_k),
        mod_spec(scale_k),
    ]
    args = [xs, g3, mods3, mods3]
    out_specs = [pl.BlockSpec((ROW_TILE, d), lambda i: (i, 0))]
    out_shape = [jax.ShapeDtypeStruct((t, d), BF16)]
    kern = _norm_mod_kernel
    if router is not None:
        in_specs += [pl.BlockSpec((d, LANES), lambda i: (0, 0))] * 2
        args += list(router)
        out_specs.append(pl.BlockSpec((ROW_TILE, LANES), lambda i: (i, 0)))
        out_shape.append(jax.ShapeDtypeStruct((t, LANES), F32))
        kern = _norm_mod_router_kernel
    return pl.pallas_call(
        kern,
        grid=(t // ROW_TILE,),
        in_specs=in_specs,
        out_specs=out_specs,
        out_shape=out_shape,
        compiler_params=_cparams(("arbitrary",)),
        name="norm_mod",
    )(*args)


def _rope_store(acc, cos_ref, sin_ref, o_ref):
    lane = lax.broadcasted_iota(jnp.int32, (1, LANES), 1)
    first = (lane % 64) < 32
    cos = cos_ref[...]
    sin = sin_ref[...]
    for c in range(acc.shape[1] // LANES):
        xc = acc[:, c * LANES:(c + 1) * LANES]
        sw = jnp.where(first, pltpu.roll(xc, 96, 1), pltpu.roll(xc, 32, 1))
        o_ref[:, c * LANES:(c + 1) * LANES] = (xc * cos + sw * sin).astype(o_ref.dtype)


def _mm_kernel(x_ref, w_ref, o_ref, wb_ref, *, n_q, q_scale):
    @pl.when(pl.program_id(1) == 0)
    def _():
        wb_ref[...] = w_ref[...].astype(BF16)

    acc = _dot(x_ref[...], wb_ref[...])
    j = pl.program_id(0)

    @pl.when(j < n_q)
    def _():
        o_ref[...] = (acc * q_scale).astype(o_ref.dtype)

    @pl.when(j >= n_q)
    def _():
        o_ref[...] = acc.astype(o_ref.dtype)


def _mm_rope_kernel(x_ref, w_ref, cos_ref, sin_ref, o_ref, wb_ref, *, n_q, q_scale, n_rope):
    @pl.when(pl.program_id(1) == 0)
    def _():
        wb_ref[...] = w_ref[...].astype(BF16)

    acc = _dot(x_ref[...], wb_ref[...])
    j = pl.program_id(0)

    @pl.when(j < n_q)
    def _():
        _rope_store(acc * q_scale, cos_ref, sin_ref, o_ref)

    @pl.when((j >= n_q) & (j < n_rope))
    def _():
        _rope_store(acc, cos_ref, sin_ref, o_ref)

    @pl.when(j >= n_rope)
    def _():
        o_ref[...] = acc.astype(o_ref.dtype)


def _mm_res_kernel(x_ref, w_ref, xs_ref, ga_ref, gb_ref, o_ref, wb_ref):
    @pl.when(pl.program_id(1) == 0)
    def _():
        wb_ref[...] = w_ref[...].astype(BF16)

    acc = _dot(x_ref[...], wb_ref[...])
    half = ROW_TILE
    o_ref[:half, :] = xs_ref[:half, :] + ga_ref[...] * acc[:half, :]
    o_ref[half:, :] = xs_ref[half:, :] + gb_ref[...] * acc[half:, :]


def _matmul(x, w, q_cols=0, q_scale=1.0, rope=None, n_rope_cols=0):
    t, k = x.shape
    n = w.shape[1]
    in_specs = [
        pl.BlockSpec((MM_TM, k), lambda j, i: (i, 0)),
        pl.BlockSpec((k, MM_TN), lambda j, i: (0, j)),
    ]
    args = [x, w]
    kern = functools.partial(_mm_kernel, n_q=q_cols // MM_TN, q_scale=q_scale)
    if rope is not None:
        in_specs += [pl.BlockSpec((MM_TM, LANES), lambda j, i: (i, 0))] * 2
        args += list(rope)
        kern = functools.partial(_mm_rope_kernel, n_q=q_cols // MM_TN, q_scale=q_scale,
                                 n_rope=n_rope_cols // MM_TN)
    return pl.pallas_call(
        kern,
        grid=(n // MM_TN, t // MM_TM),
        in_specs=in_specs,
        out_specs=pl.BlockSpec((MM_TM, MM_TN), lambda j, i: (i, j)),
        out_shape=jax.ShapeDtypeStruct((t, n), BF16),
        scratch_shapes=[pltpu.VMEM((k, MM_TN), BF16)],
        compiler_params=_cparams(("arbitrary", "arbitrary")),
        name="matmul",
    )(*args)


def _matmul_residual(x, w, xs, mods3, layer, gate_k, n_batch, tpb):
    t, k = x.shape
    n = w.shape[1]
    ncol = n // MM_TN
    nb = n_batch

    def gate_spec(half):
        return pl.BlockSpec(
            (None, 1, MM_TN),
            lambda j, i: (layer * 8 + _row_sel(2 * i + half, tpb, nb), 0, gate_k * ncol + j))

    return pl.pallas_call(
        _mm_res_kernel,
        grid=(ncol, t // MM_TM),
        in_specs=[
            pl.BlockSpec((MM_TM, k), lambda j, i: (i, 0)),
            pl.BlockSpec((k, MM_TN), lambda j, i: (0, j)),
            pl.BlockSpec((MM_TM, MM_TN), lambda j, i: (i, j)),
            gate_spec(0),
            gate_spec(1),
        ],
        out_specs=pl.BlockSpec((MM_TM, MM_TN), lambda j, i: (i, j)),
        out_shape=jax.ShapeDtypeStruct((t, n), F32),
        scratch_shapes=[pltpu.VMEM((k, MM_TN), BF16)],
        input_output_aliases={2: 0},
        compiler_params=_cparams(("arbitrary", "arbitrary")),
        name="matmul_residual",
    )(x, w, xs, mods3, mods3)


def _na_kernel(q_ref, k_ref, v_ref, tab_ref, o_ref, sl_ref, sc_ref, pl_ref, pc_ref, li_ref, *, rows):
    nq = NA_GROUP * GRID_W
    nk = NA_KROWS * GRID_W
    half_win = NA_WIN_ROWS // 2
    left = lax.broadcasted_iota(jnp.int32, (1, LANES), 1) < GRID_W

    s = _dot_nt(q_ref[0:CTX_LEN, :], k_ref[0:CTX_LEN, :])
    p = jnp.exp2(s - jnp.max(s, axis=-1, keepdims=True))
    l = jnp.sum(p, axis=-1, keepdims=True)
    o_ref[0:CTX_LEN, :] = (_dot(p.astype(BF16), v_ref[0:CTX_LEN, :]) / l).astype(o_ref.dtype)

    def geometry(g):
        r0 = jnp.asarray(g, jnp.int32) * NA_GROUP
        start = jnp.clip(r0 - half_win, 0, rows - NA_KROWS)
        q0 = pl.multiple_of(CTX_LEN + r0 * GRID_W, nq)
        k0 = pl.multiple_of(CTX_LEN + start * GRID_W, GRID_W)
        return r0, start, q0, k0

    def stage_a(g, par):
        _, _, q0, k0 = geometry(g)
        q = q_ref[pl.ds(q0, nq), :]
        sl_ref[par] = _dot_nt(q, k_ref[pl.ds(k0, nk), :])
        sc_ref[par] = _dot_nt(q, k_ref[0:CTX_LEN, :])

    def stage_b(g, par):
        r0, start, _, _ = geometry(g)
        for i in range(NA_GROUP):
            r = r0 + i
            rs = jnp.clip(r - half_win, 0, rows - NA_WIN_ROWS)
            rsl = slice(i * GRID_W, (i + 1) * GRID_W)
            tiles = []
            for jp in range(NA_KROWS // 2):
                kr = start + 2 * jp
                ok0 = ((kr >= rs) & (kr < rs + NA_WIN_ROWS)).astype(jnp.int32)
                ok1 = ((kr + 1 >= rs) & (kr + 1 < rs + NA_WIN_ROWS)).astype(jnp.int32)
                e = jnp.clip(kr - r + (NA_WIN_ROWS - 1) + 2, 0, 2 * NA_WIN_ROWS + 1)
                ok = jnp.where(left, ok0, ok1) > 0
                tiles.append(jnp.where(ok, tab_ref[e], NEG_BIG))
            sl = sl_ref[par, rsl, :] + jnp.concatenate(tiles, axis=1)
            sc = sc_ref[par, rsl, :]
            m = jnp.maximum(jnp.max(sl, axis=-1, keepdims=True), jnp.max(sc, axis=-1, keepdims=True))
            pl_i = jnp.exp2(sl - m)
            pc_i = jnp.exp2(sc - m)
            l_i = jnp.sum(pl_i, axis=-1, keepdims=True) + jnp.sum(pc_i, axis=-1, keepdims=True)
            pl_ref[par, rsl, :] = pl_i.astype(BF16)
            pc_ref[par, rsl, :] = pc_i.astype(BF16)
            li_ref[par, rsl, :] = jnp.broadcast_to(1.0 / l_i, (GRID_W, LANES))

    def stage_c(g, par):
        _, _, q0, k0 = geometry(g)
        o = _dot(pl_ref[par], v_ref[pl.ds(k0, nk), :]) + _dot(pc_ref[par], v_ref[0:CTX_LEN, :])
        o_ref[pl.ds(q0, nq), :] = (o * li_ref[par]).astype(o_ref.dtype)

    n_groups = rows // NA_GROUP
    stage_a(0, 0)
    stage_a(1, 1)
    stage_b(0, 0)

    def pair(u, carry):
        t = 2 + 2 * u
        stage_a(t, 0)
        stage_b(t - 1, 1)
        stage_c(t - 2, 0)
        stage_a(t + 1, 1)
        stage_b(t, 0)
        stage_c(t - 1, 1)
        return carry

    lax.fori_loop(0, (n_groups - 2) // 2, pair, 0)
    stage_b(n_groups - 1, 1)
    stage_c(n_groups - 2, 0)
    stage_c(n_groups - 1, 1)


def _na_bias_table(rpb):
    h = rpb.shape[0]
    qc = np.arange(GRID_W)[:, None]
    kc = np.arange(GRID_W)[None, :]
    cs = np.clip(qc - NA_WIN_COLS // 2, 0, GRID_W - NA_WIN_COLS)
    col_ok = (kc >= cs) & (kc < cs + NA_WIN_COLS)
    dc = np.clip(kc - qc + (NA_WIN_COLS - 1), 0, 2 * NA_WIN_COLS - 2)
    a = jnp.take(rpb.astype(F32), jnp.asarray(dc.reshape(-1)), axis=2)
    a = a.reshape(h, 2 * NA_WIN_ROWS - 1, GRID_W, GRID_W) * LOG2E
    a = jnp.where(jnp.asarray(col_ok)[None, None], a, NEG_BIG)
    a = jnp.pad(a, ((0, 0), (2, 2), (0, 0), (0, 0)), constant_values=NEG_BIG)
    return jnp.concatenate([a[:, :-1], a[:, 1:]], axis=-1)


def _na_attention(qkv, rpb, n_batch, tb):
    t = qkv.shape[0]
    rows = (tb - CTX_LEN) // GRID_W
    assert rows % (2 * NA_GROUP) == 0 and rows >= NA_KROWS
    nq, nk = NA_GROUP * GRID_W, NA_KROWS * GRID_W
    table = _na_bias_table(rpb)
    hh = NA_HEADS
    return pl.pallas_call(
        functools.partial(_na_kernel, rows=rows),
        grid=(hh, n_batch),
        in_specs=[
            pl.BlockSpec((tb, LANES), lambda h, b: (b, h)),
            pl.BlockSpec((tb, LANES), lambda h, b: (b, hh + h)),
            pl.BlockSpec((tb, LANES), lambda h, b: (b, 2 * hh + h)),
            pl.BlockSpec((None, 2 * NA_WIN_ROWS + 2, GRID_W, LANES), lambda h, b: (h, 0, 0, 0)),
        ],
        out_specs=pl.BlockSpec((tb, LANES), lambda h, b: (b, h)),
        out_shape=jax.ShapeDtypeStruct((t, D_MODEL), BF16),
        scratch_shapes=[
            pltpu.VMEM((2, nq, nk), F32),
            pltpu.VMEM((2, nq, CTX_LEN), F32),
            pltpu.VMEM((2, nq, nk), BF16),
            pltpu.VMEM((2, nq, CTX_LEN), BF16),
            pltpu.VMEM((2, nq, LANES), F32),
        ],
        compiler_params=_cparams(("arbitrary", "arbitrary")),
        name="na_attention",
    )(qkv, qkv, qkv, table)


def _da_kernel(lam_ref, g_ref, q_ref, k_ref, v_ref, o_ref, acc_ref, m_ref, l_ref,
               s0_ref, s1_ref, p0_ref, p1_ref, a0_ref, a1_ref, *, lam_init, n_kv_chunks):
    d = DIFF_HEAD_DIM
    i = pl.program_id(2)
    lam = lam_ref[...]
    lam_full = (jnp.exp(jnp.sum(lam[0:1] * lam[1:2], axis=-1, keepdims=True))
                - jnp.exp(jnp.sum(lam[2:3] * lam[3:4], axis=-1, keepdims=True)) + lam_init)
    s_bufs, p_bufs, a_bufs = (s0_ref, s1_ref), (p0_ref, p1_ref), (a0_ref, a1_ref)

    def tile(q0, nq, n_lat_chunks):
        acc_ref[:, 0:nq, :] = jnp.zeros((2, nq, 2 * d), F32)
        m_ref[:, 0:nq, :] = jnp.full((2, nq, LANES), NEG_BIG, F32)
        l_ref[:, 0:nq, :] = jnp.zeros((2, nq, LANES), F32)

        def key_rows(e):
            if isinstance(e, int):
                return (0, CTX_LEN) if e == 0 else (CTX_LEN + (e - 1) * DA_TK, DA_TK)
            return pl.multiple_of(CTX_LEN + (e - 1) * DA_TK, ROW_TILE), DA_TK

        def stage_a(e, par):
            k0, nk = key_rows(e)
            q = q_ref[pl.ds(q0, nq), :]
            k = k_ref[pl.ds(k0, nk), :]
            for c in range(2):
                s_bufs[par][c, 0:nq, 0:nk] = _dot_nt(q[:, c * d:(c + 1) * d], k[:, c * d:(c + 1) * d])

        def stage_b(e, par):
            _, nk = key_rows(e)
            s_ref, p_ref, a_ref = s_bufs[par], p_bufs[par], a_bufs[par]
            for c in range(2):
                for sb in range(nq // DA_SUB):
                    rows = slice(sb * DA_SUB, (sb + 1) * DA_SUB)
                    blocks = [s_ref[c, rows, j * LANES:(j + 1) * LANES] for j in range(nk // LANES)]
                    m_prev = m_ref[c, rows, :]
                    m_new = jnp.maximum(
                        m_prev, jnp.max(functools.reduce(jnp.maximum, blocks), axis=-1, keepdims=True))
                    alpha = jnp.exp2(m_prev - m_new)
                    p = [jnp.exp2(blk - m_new) for blk in blocks]
                    l_ref[c, rows, :] = alpha * l_ref[c, rows, :] + functools.reduce(jnp.add, p)
                    for j, pj in enumerate(p):
                        p_ref[c, rows, j * LANES:(j + 1) * LANES] = pj.astype(BF16)
                    a_ref[c, rows, :] = alpha
                    m_ref[c, rows, :] = m_new

        def stage_c(e, par):
            k0, nk = key_rows(e)
            v = v_ref[pl.ds(k0, nk), :]
            for c in range(2):
                pv = _dot(p_bufs[par][c, 0:nq, 0:nk], v)
                alpha = a_bufs[par][c, 0:nq, :]
                acc_ref[c, 0:nq, 0:LANES] = alpha * acc_ref[c, 0:nq, 0:LANES] + pv[:, :LANES]
                acc_ref[c, 0:nq, LANES:] = alpha * acc_ref[c, 0:nq, LANES:] + pv[:, LANES:]

        n = n_lat_chunks
        if n == 0:
            stage_a(0, 0)
            stage_b(0, 0)
            stage_c(0, 0)
        else:
            assert n % 2 == 0
            stage_a(0, 0)
            stage_a(1, 1)
            stage_b(0, 0)
            stage_a(2, 0)
            stage_b(1, 1)
            stage_c(0, 0)

            def pair(u, carry):
                t = 3 + 2 * u
                stage_a(t, 1)
                stage_b(t - 1, 0)
                stage_c(t - 2, 1)
                stage_a(t + 1, 0)
                stage_b(t, 1)
                stage_c(t - 1, 0)
                return carry

            lax.fori_loop(0, (n - 2) // 2, pair, 0)
            stage_b(n, 0)
            stage_c(n - 1, 1)
            stage_c(n, 0)

        r0 = 1.0 / jnp.sum(l_ref[0, 0:nq, :], axis=-1, keepdims=True)
        r1 = lam_full / jnp.sum(l_ref[1, 0:nq, :], axis=-1, keepdims=True)
        o_a = acc_ref[0, 0:nq, 0:LANES] * r0 - acc_ref[1, 0:nq, 0:LANES] * r1
        o_b = acc_ref[0, 0:nq, LANES:] * r0 - acc_ref[1, 0:nq, LANES:] * r1
        ms = (jnp.sum(o_a * o_a, axis=-1, keepdims=True) + jnp.sum(o_b * o_b, axis=-1, keepdims=True)) / (2 * d)
        inv = lax.rsqrt(ms + NORM_EPS) * (1.0 - lam_init)
        g = g_ref[...]
        o_ref[pl.ds(q0, nq), 0:LANES] = (o_a * inv * g[:, :LANES]).astype(o_ref.dtype)
        o_ref[pl.ds(q0, nq), LANES:] = (o_b * inv * g[:, LANES:]).astype(o_ref.dtype)

    @pl.when(i == 0)
    def _():
        tile(0, CTX_LEN, 0)

    @pl.when(i > 0)
    def _():
        tile(pl.multiple_of(CTX_LEN + (i - 1) * DA_TQ, ROW_TILE), DA_TQ, n_kv_chunks)


def _diff_attention(qkv, lam, subln_g, layer_idx, n_batch, tb):
    t = qkv.shape[0]
    hh = DIFF_HEADS
    w = 2 * DIFF_HEAD_DIM
    n_lat = (tb - CTX_LEN) // DA_TQ
    lam_init = 0.8 - 0.6 * math.exp(-0.3 * layer_idx)
    return pl.pallas_call(
        functools.partial(_da_kernel, lam_init=lam_init, n_kv_chunks=(tb - CTX_LEN) // DA_TK),
        grid=(n_batch, hh, 1 + n_lat),
        in_specs=[
            pl.BlockSpec((4, DIFF_HEAD_DIM), lambda b, h, i: (0, 0)),
            pl.BlockSpec((1, w), lambda b, h, i: (0, 0)),
            pl.BlockSpec((tb, w), lambda b, h, i: (b, h)),
            pl.BlockSpec((tb, w), lambda b, h, i: (b, hh + h)),
            pl.BlockSpec((tb, w), lambda b, h, i: (b, 2 * hh + h)),
        ],
        out_specs=pl.BlockSpec((tb, w), lambda b, h, i: (b, h)),
        out_shape=jax.ShapeDtypeStruct((t, D_MODEL), BF16),
        scratch_shapes=[
            pltpu.VMEM((2, DA_TQ, w), F32),
            pltpu.VMEM((2, DA_TQ, LANES), F32),
            pltpu.VMEM((2, DA_TQ, LANES), F32),
            pltpu.VMEM((2, DA_TQ, DA_TK), F32),
            pltpu.VMEM((2, DA_TQ, DA_TK), F32),
            pltpu.VMEM((2, DA_TQ, DA_TK), BF16),
            pltpu.VMEM((2, DA_TQ, DA_TK), BF16),
            pltpu.VMEM((2, DA_TQ, LANES), F32),
            pltpu.VMEM((2, DA_TQ, LANES), F32),
        ],
        compiler_params=_cparams(("arbitrary", "arbitrary", "arbitrary")),
        name="diff_attention",
    )(lam.astype(F32), subln_g.astype(F32).reshape(1, w), qkv, qkv, qkv)


def _hg_kernel(q_ref, i_ref, ff_ref, fb_ref, lb_ref, o_ref, stf_ref, stb_ref, *, n_groups):
    c = HGRN_CHUNK
    rg = HG_GROUP * c
    dk = HGRN_DK
    o_ref[...] = jnp.zeros(o_ref.shape, F32)
    stf_ref[...] = jnp.zeros(stf_ref.shape, F32)
    stb_ref[...] = jnp.zeros(stb_ref.shape, F32)
    row = lax.broadcasted_iota(jnp.int32, (rg, rg), 0)
    col = lax.broadcasted_iota(jnp.int32, (rg, rg), 1)
    same = (row // c) == (col // c)
    lower = same & (row >= col)
    upper = same & (col >= row)
    lower_b = lower.astype(BF16)
    upper_b = upper.astype(BF16)
    chunk_of_row = lax.broadcasted_iota(jnp.int32, (rg, 1), 0) // c
    lb_all = lb_ref[...]

    def per_chunk_rows(b, r):
        return jnp.concatenate(
            [jnp.broadcast_to(b[g * c + r:g * c + r + 1, :], (c, dk)) for g in range(HG_GROUP)], axis=0)

    def group(gidx, f_ref, lb, mask, mask_b, st_ref, end_row, order):
        r0 = pl.multiple_of(gidx * rg, rg)
        qr = q_ref[pl.ds(r0, rg), :].astype(F32)
        q = qr * jax.nn.sigmoid(qr) * (dk ** -0.5)
        v = i_ref[pl.ds(r0, rg), :]
        fr = f_ref[pl.ds(r0, rg), :].astype(F32)
        fg = lb + (1.0 - lb) * jax.nn.sigmoid(fr)
        k = 1.0 - fg
        lf = jnp.log(fg)
        hi = lf.astype(BF16)
        rem = lf - hi.astype(F32)
        mid = rem.astype(BF16)
        lo = (rem - mid.astype(F32)).astype(BF16)
        cs = _dot(mask_b, jnp.concatenate([hi, mid, lo], axis=1))
        bcum = cs[:, 0:dk] + cs[:, dk:2 * dk] + cs[:, 2 * dk:]
        b_end = per_chunk_rows(bcum, end_row)
        b_mid = per_chunk_rows(bcum, c // 2)
        qt = (q * jnp.exp(bcum - b_mid)).astype(BF16)
        kt = (k * jnp.exp(b_mid - bcum)).astype(BF16)
        a = jnp.where(mask, _dot_nt(qt, kt), 0.0).astype(BF16)
        o_intra = _dot(a, v)
        k_end = k * jnp.exp(b_end - bcum)
        k_exp = jnp.concatenate(
            [jnp.where(chunk_of_row == g, k_end, 0.0) for g in range(HG_GROUP)], axis=1).astype(BF16)
        u = _dot_tn(v, k_exp)
        st = st_ref[...]
        st_in = [None] * HG_GROUP
        for g in order:
            st_in[g] = st.astype(BF16)
            st = st * jnp.exp(bcum[g * c + end_row:g * c + end_row + 1, :]) + u[:, g * dk:(g + 1) * dk]
        st_ref[...] = st
        oi = _dot_nt((q * jnp.exp(bcum)).astype(BF16), jnp.concatenate(st_in, axis=0))
        o_inter = jnp.concatenate(
            [oi[g * c:(g + 1) * c, g * dk:(g + 1) * dk] for g in range(HG_GROUP)], axis=0)
        o_ref[pl.ds(r0, rg), :] += o_intra + o_inter

    fwd_order = list(range(HG_GROUP))

    def body(n, carry):
        group(n, ff_ref, lb_all[0:1, :], lower, lower_b, stf_ref, c - 1, fwd_order)
        nb = jnp.where(n == 0, 0, n_groups - n)
        group(nb, fb_ref, lb_all[1:2, :], upper, upper_b, stb_ref, 0, fwd_order[::-1])
        return carry

    lax.fori_loop(0, n_groups, body, 0)


def _hgrn_scan(proj, lb2, n_batch, tb):
    t = proj.shape[0]
    hh = HGRN_HEADS
    rg = HG_GROUP * HGRN_CHUNK
    assert CTX_LEN == rg and tb % rg == 0

    def col(k):
        return pl.BlockSpec((tb, LANES), lambda b, h: (b, k * hh + h))

    return pl.pallas_call(
        functools.partial(_hg_kernel, n_groups=tb // rg),
        grid=(n_batch, hh),
        in_specs=[col(0), col(1), col(3), col(4), pl.BlockSpec((2, LANES), lambda b, h: (0, h))],
        out_specs=pl.BlockSpec((tb, LANES), lambda b, h: (b, h)),
        out_shape=jax.ShapeDtypeStruct((t, D_MODEL), F32),
        scratch_shapes=[pltpu.VMEM((LANES, LANES), F32), pltpu.VMEM((LANES, LANES), F32)],
        compiler_params=_cparams(("arbitrary", "arbitrary")),
        name="hgrn_scan",
    )(proj, proj, proj, proj, lb2)


def _hg_finish_kernel(o_ref, g_ref, ng_ref, h_ref):
    o = o_ref[...]
    y = o * lax.rsqrt(jnp.mean(o * o, axis=-1, keepdims=True) + NORM_EPS) * ng_ref[...]
    g = g_ref[...].astype(F32)
    h_ref[...] = (y * (g * jax.nn.sigmoid(g))).astype(h_ref.dtype)


def _hgrn_finish(o, proj, norm_g):
    t, d = o.shape
    return pl.pallas_call(
        _hg_finish_kernel,
        grid=(t // ROW_TILE,),
        in_specs=[
            pl.BlockSpec((ROW_TILE, d), lambda i: (i, 0)),
            pl.BlockSpec((ROW_TILE, d), lambda i: (i, 2)),
            pl.BlockSpec((1, d), lambda i: (0, 0)),
        ],
        out_specs=pl.BlockSpec((ROW_TILE, d), lambda i: (i, 0)),
        out_shape=jax.ShapeDtypeStruct((t, d), BF16),
        compiler_params=_cparams(("arbitrary",)),
        name="hgrn_finish",
    )(o, proj, norm_g.astype(F32).reshape(1, d))


def _moe_kernel(be_ref, first_ref, nxt_ref, nu_ref, x_ref, wg_hbm, wu_hbm, wd_hbm, o_ref,
                sg_ref, su_ref, sd_ref, bg_ref, bu_ref, bd_ref, sem, *, layer):
    b = pl.program_id(0)

    def weight_copies(e):
        return (pltpu.make_async_copy(wg_hbm.at[layer, e], sg_ref, sem.at[0]),
                pltpu.make_async_copy(wu_hbm.at[layer, e], su_ref, sem.at[1]),
                pltpu.make_async_copy(wd_hbm.at[layer, e], sd_ref, sem.at[2]))

    @pl.when(b == 0)
    def _():
        for cp in weight_copies(be_ref[0]):
            cp.start()

    @pl.when(first_ref[b] == 1)
    def _():
        for cp in weight_copies(be_ref[b]):
            cp.wait()
        for src, dst in ((sg_ref, bg_ref), (su_ref, bu_ref), (sd_ref, bd_ref)):
            slab = src.shape[0] // MOE_CAST_SLABS

            def cast(r, carry, src=src, dst=dst, slab=slab):
                r0 = pl.multiple_of(r * slab, slab)
                dst[pl.ds(r0, slab), :] = src[pl.ds(r0, slab), :].astype(BF16)
                return carry

            lax.fori_loop(0, MOE_CAST_SLABS, cast, 0)

        @pl.when(nxt_ref[b] >= 0)
        def _():
            for cp in weight_copies(nxt_ref[b]):
                cp.start()

    @pl.when(b < nu_ref[0])
    def _():
        x = x_ref[...]
        g = _dot(x, bg_ref[...])
        u = _dot(x, bu_ref[...])
        h = (g * jax.nn.sigmoid(g) * u).astype(BF16)
        o_ref[...] = _dot(h, bd_ref[...]).astype(o_ref.dtype)

    @pl.when(b >= nu_ref[0])
    def _():
        o_ref[...] = jnp.zeros(o_ref.shape, o_ref.dtype)


def _moe_experts(xb, blk_expert, blk_first, blk_next, n_used, w_gate, w_up, w_down, layer):
    n_slots, d = xb.shape
    f = w_gate.shape[3]
    n_blocks = n_slots // MOE_TM
    row = lambda b, *_: (b, 0)
    grid_spec = pltpu.PrefetchScalarGridSpec(
        num_scalar_prefetch=4,
        grid=(n_blocks,),
        in_specs=[
            pl.BlockSpec((MOE_TM, d), row),
            pl.BlockSpec(memory_space=pl.ANY),
            pl.BlockSpec(memory_space=pl.ANY),
            pl.BlockSpec(memory_space=pl.ANY),
        ],
        out_specs=pl.BlockSpec((MOE_TM, d), row),
        scratch_shapes=[
            pltpu.VMEM((d, f), F32), pltpu.VMEM((d, f), F32), pltpu.VMEM((f, d), F32),
            pltpu.VMEM((d, f), BF16), pltpu.VMEM((d, f), BF16), pltpu.VMEM((f, d), BF16),
            pltpu.SemaphoreType.DMA((3,)),
        ],
    )
    return pl.pallas_call(
        functools.partial(_moe_kernel, layer=layer),
        grid_spec=grid_spec,
        out_shape=jax.ShapeDtypeStruct((n_slots, d), BF16),
        compiler_params=_cparams(("arbitrary",)),
        name="moe_experts",
    )(blk_expert, blk_first, blk_next, n_used, xb, w_gate, w_up, w_down)


def _route(logits, b_router, n_blocks):
    t = logits.shape[0]
    s = jax.nn.sigmoid(logits)
    sel = s + b_router.astype(F32)

    def top2(vals):
        ids = jnp.arange(vals.shape[-1], dtype=jnp.int32)
        i1 = jnp.argmax(vals, axis=-1).astype(jnp.int32)
        v1 = jnp.max(vals, axis=-1)
        rest = jnp.where(ids == i1[..., None], -jnp.inf, vals)
        i2 = jnp.argmax(rest, axis=-1).astype(jnp.int32)
        return v1, jnp.max(rest, axis=-1), i1, i2

    g1, g2, _, _ = top2(sel.reshape(t, N_GROUPS, EXPERTS_PER_GROUP))
    g_best = jnp.argmax(g1 + g2, axis=-1)
    in_grp = (jnp.arange(N_EXPERTS) // EXPERTS_PER_GROUP)[None, :] == g_best[:, None]
    _, _, e1, e2 = top2(jnp.where(in_grp, sel, -jnp.inf))
    e_idx = jnp.stack([e1, e2], axis=-1)
    w = jnp.take_along_axis(s, e_idx, axis=1)
    w = w / jnp.sum(w, axis=-1, keepdims=True)
    flat_e = e_idx.reshape(-1).astype(jnp.int32)
    onehot = (flat_e[:, None] == jnp.arange(N_EXPERTS, dtype=jnp.int32)[None, :]).astype(jnp.int32)
    csum = jnp.cumsum(onehot, axis=0)
    rank = jnp.sum((csum - onehot) * onehot, axis=1)
    counts = csum[-1]
    pcounts = (counts + MOE_TM - 1) // MOE_TM * MOE_TM
    pends = jnp.cumsum(pcounts)
    pstarts = pends - pcounts
    dest = (pstarts[flat_e] + rank).astype(jnp.int32)
    flat_t = jnp.arange(t * TOP_K, dtype=jnp.int32) // TOP_K
    slot_tok = jnp.zeros((n_blocks * MOE_TM,), jnp.int32).at[dest].set(flat_t)
    blk_expert = jnp.minimum(
        jnp.searchsorted(pends, jnp.arange(n_blocks, dtype=jnp.int32) * MOE_TM, side='right'),
        N_EXPERTS - 1).astype(jnp.int32)
    n_used = (pends[-1] // MOE_TM).astype(jnp.int32).reshape(1)
    blk_ids = jnp.arange(n_blocks, dtype=jnp.int32)
    prev_expert = jnp.concatenate([jnp.full((1,), -1, jnp.int32), blk_expert[:-1]])
    blk_first = ((blk_ids < n_used[0]) & (blk_expert != prev_expert)).astype(jnp.int32)
    e_ids = jnp.arange(N_EXPERTS, dtype=jnp.int32)
    later_used = (e_ids[None, :] > e_ids[:, None]) & (counts[None, :] > 0)
    next_used = jnp.min(jnp.where(later_used, e_ids[None, :], N_EXPERTS), axis=1)
    next_used = jnp.where(next_used == N_EXPERTS, -1, next_used).astype(jnp.int32)
    blk_next = next_used[blk_expert]
    return w, dest.reshape(t, TOP_K), slot_tok, (blk_expert, blk_first, blk_next, n_used)


def _moe_res_kernel(xs_ref, y0_ref, y1_ref, w_ref, gate_ref, o_ref):
    w = w_ref[...]
    y = w[:, 0:1] * y0_ref[...].astype(F32) + w[:, 1:2] * y1_ref[...].astype(F32)
    o_ref[...] = xs_ref[...] + gate_ref[...] * y


def _moe_res_final_kernel(xs_ref, y0_ref, y1_ref, w_ref, gate_ref, fg_ref, o_ref):
    w = w_ref[...]
    y = w[:, 0:1] * y0_ref[...].astype(F32) + w[:, 1:2] * y1_ref[...].astype(F32)
    x = xs_ref[...] + gate_ref[...] * y
    o_ref[...] = x * lax.rsqrt(jnp.mean(x * x, axis=-1, keepdims=True) + NORM_EPS) * fg_ref[...]


def _moe_residual(xs, y0, y1, w, mods3, layer, n_batch, tpb):
    t, d = xs.shape
    nb = n_batch
    row = lambda i: (i, 0)
    return pl.pallas_call(
        _moe_res_kernel,
        grid=(t // ROW_TILE,),
        in_specs=[
            pl.BlockSpec((ROW_TILE, d), row),
            pl.BlockSpec((ROW_TILE, d), row),
            pl.BlockSpec((ROW_TILE, d), row),
            pl.BlockSpec((ROW_TILE, TOP_K), row),
            pl.BlockSpec((None, 1, d), lambda i: (layer * 8 + _row_sel(i, tpb, nb), 0, 5)),
        ],
        out_specs=pl.BlockSpec((ROW_TILE, d), row),
        out_shape=jax.ShapeDtypeStruct((t, d), F32),
        input_output_aliases={0: 0},
        compiler_params=_cparams(("arbitrary",)),
        name="moe_residual",
    )(xs, y0, y1, w, mods3)


def _moe_residual_final(xs, y0, y1, w, mods3, layer, final_g, n_batch, tpb):
    t, d = xs.shape
    lat = tpb - CTX_LEN // ROW_TILE
    src = lambda i: ((i // lat) * tpb + CTX_LEN // ROW_TILE + i % lat, 0)
    return pl.pallas_call(
        _moe_res_final_kernel,
        grid=(n_batch * lat,),
        in_specs=[
            pl.BlockSpec((ROW_TILE, d), src),
            pl.BlockSpec((ROW_TILE, d), src),
            pl.BlockSpec((ROW_TILE, d), src),
            pl.BlockSpec((ROW_TILE, TOP_K), src),
            pl.BlockSpec((None, 1, d), lambda i: (layer * 8 + i // lat, 0, 5)),
            pl.BlockSpec((1, d), lambda i: (0, 0)),
        ],
        out_specs=pl.BlockSpec((ROW_TILE, d), lambda i: (i, 0)),
        out_shape=jax.ShapeDtypeStruct((n_batch * lat * ROW_TILE, d), F32),
        compiler_params=_cparams(("arbitrary",)),
        name="moe_residual_final",
    )(xs, y0, y1, w, mods3, final_g.astype(F32).reshape(1, d))


def _rope_tables(n_batch, seq):
    quarter = DIFF_HEAD_DIM // 4
    tpos = jnp.arange(seq)
    rowp = (tpos // GRID_W).astype(F32)
    colp = (tpos % GRID_W).astype(F32)
    inv = ROPE_THETA ** (-jnp.arange(quarter, dtype=F32) / quarter)
    ang_r = rowp[:, None] * inv[None, :]
    ang_c = colp[:, None] * inv[None, :]
    cos = jnp.concatenate([jnp.cos(ang_r)] * 2 + [jnp.cos(ang_c)] * 2, axis=-1)
    sin = jnp.concatenate([-jnp.sin(ang_r), jnp.sin(ang_r), -jnp.sin(ang_c), jnp.sin(ang_c)], axis=-1)
    cos = jnp.concatenate([jnp.ones((CTX_LEN, LANES), F32), cos], axis=0)
    sin = jnp.concatenate([jnp.zeros((CTX_LEN, LANES), F32), sin], axis=0)
    return jnp.tile(cos, (n_batch, 1)), jnp.tile(sin, (n_batch, 1))


def kernel(x, c, ctx, c_ctx, ada_w, ada_b, norm_g, na_w_qkv, na_rpb, na_w_o, da_w_qkv, da_lambda, da_subln_g, da_w_o, hg_w_in, hg_lb, hg_norm_g, hg_w_o, moe_w_router, moe_b_router, moe_w_gate, moe_w_up, moe_w_down, final_g):
    n_batch, seq, d = x.shape
    tb = CTX_LEN + seq
    t = n_batch * tb
    tpb = tb // ROW_TILE
    assert d == D_MODEL and ctx.shape[1] == CTX_LEN and n_batch + 1 <= 8
    assert seq % DA_TQ == 0 and t % MM_TM == 0 and seq // GRID_W >= NA_WIN_ROWS

    xs = jnp.concatenate([ctx, x], axis=1).reshape(t, d).astype(F32)

    c_rows = jnp.zeros((8, d), F32).at[:n_batch].set(c.astype(F32)).at[n_batch].set(c_ctx.astype(F32))
    mods = _ada_mods(c_rows, ada_w.astype(F32), ada_b.astype(F32))
    mods3 = mods.reshape(DEPTH * 8, 1, 6 * d)
    g3 = norm_g.astype(F32).reshape(DEPTH * 2, 1, d)

    lb_p = jax.nn.softmax(hg_lb.astype(F32), axis=1)
    lb_all = jnp.cumsum(lb_p, axis=1) - lb_p[:, :1]

    wr = jnp.zeros((d, LANES), F32).at[:, :N_EXPERTS].set(moe_w_router.astype(F32))
    wr_hi = wr.astype(BF16)
    wr_lo = (wr - wr_hi.astype(F32)).astype(BF16)

    n_blocks = -(-(t * TOP_K) // MOE_TM) + N_EXPERTS
    rope = _rope_tables(n_batch, seq)
    wg = moe_w_gate.astype(F32)
    wu = moe_w_up.astype(F32)
    wd = moe_w_down.astype(F32)
    na_qs = NA_HEAD_DIM ** -0.5 * LOG2E
    da_qs = DIFF_HEAD_DIM ** -0.5 * LOG2E

    counters = [0, 0, 0]
    out = None
    for i in range(DEPTH):
        kind = i % N_MIXERS
        j = counters[kind]
        counters[kind] += 1

        h = _norm_mod(xs, mods3, g3, i, 0, n_batch, tpb)[0]
        if kind == 0:
            qkv = _matmul(h, na_w_qkv[j], q_cols=d, q_scale=na_qs)
            o = _na_attention(qkv, na_rpb[j], n_batch, tb)
            w_o = na_w_o[j]
        elif kind == 1:
            qkv = _matmul(h, da_w_qkv[j], q_cols=d, q_scale=da_qs, rope=rope, n_rope_cols=2 * d)
            o = _diff_attention(qkv, da_lambda[j], da_subln_g[j], i, n_batch, tb)
            w_o = da_w_o[j]
        else:
            proj = _matmul(h, hg_w_in[j])
            o32 = _hgrn_scan(proj, lb_all[:, i, :], n_batch, tb)
            o = _hgrn_finish(o32, proj, hg_norm_g[j])
            w_o = hg_w_o[j]
        xs = _matmul_residual(o, w_o, xs, mods3, i, 2, n_batch, tpb)

        h2, logits = _norm_mod(xs, mods3, g3, i, 1, n_batch, tpb, router=(wr_hi, wr_lo))
        w, dest, slot_tok, blk_info = _route(logits[:, :N_EXPERTS], moe_b_router, n_blocks)
        xb = h2.at[slot_tok].get(mode="promise_in_bounds")
        yb = _moe_experts(xb, *blk_info, wg, wu, wd, i)
        y0 = yb.at[dest[:, 0]].get(mode="promise_in_bounds")
        y1 = yb.at[dest[:, 1]].get(mode="promise_in_bounds")
        if i < DEPTH - 1:
            xs = _moe_residual(xs, y0, y1, w, mods3, i, n_batch, tpb)
        else:
            out = _moe_residual_final(xs, y0, y1, w, mods3, i, final_g, n_batch, tpb)
    return out.reshape(n_batch, seq, d).astype(x.dtype)
```

```python
import functools
import math

import numpy as np
import jax
import jax.numpy as jnp
from jax import lax
from jax.experimental import pallas as pl
from jax.experimental.pallas import tpu as pltpu

F32 = jnp.float32
BF16 = jnp.bfloat16

D_MODEL = 2048
DEPTH = 4
GRID_W = 64
CTX_LEN = 256
N_MIXERS = 3
NORM_EPS = 1e-6
ROPE_THETA = 10000.0

NA_HEADS = 16
NA_HEAD_DIM = 128
NA_WIN_ROWS = 8
NA_WIN_COLS = 16

DIFF_HEADS = 8
DIFF_HEAD_DIM = 128

HGRN_HEADS = 16
HGRN_DK = 128
HGRN_CHUNK = 64

N_EXPERTS = 16
N_GROUPS = 4
EXPERTS_PER_GROUP = 4
TOP_K = 2
D_EXPERT = 1024

LANES = 128
ROW_TILE = 256
MM_TM = 512
MM_TN = 1024
MOE_TM = 256
DA_TQ = 512
VMEM_LIMIT = 56 * 1024 * 1024

NA_GROUP = 4
NA_KROWS = 12
DA_SUB = 64
DA_TK = 512
HG_GROUP = 4
MOE_XBUFS = 3
MOE_CAST_SLABS = 8

NEG_BIG = -1e30
LOG2E = 1.4426950408889634


def _cparams(sem):
    return pltpu.CompilerParams(dimension_semantics=sem, vmem_limit_bytes=VMEM_LIMIT)


def _dot(a, b):
    return jnp.dot(a, b, preferred_element_type=F32)


def _dot_nt(a, b):
    return lax.dot_general(a, b, (((1,), (1,)), ((), ())), preferred_element_type=F32)


def _dot_tn(a, b):
    return lax.dot_general(a, b, (((0,), (0,)), ((), ())), preferred_element_type=F32)


def _row_sel(sub, tiles_per_batch, n_batch):
    return jnp.where(sub % tiles_per_batch == 0, n_batch, sub // tiles_per_batch)


def _ada_kernel(c_ref, w_ref, b_ref, o_ref):
    c = c_ref[...]
    a = c * jax.nn.sigmoid(c)
    o_ref[...] = jnp.dot(a, w_ref[...], preferred_element_type=F32,
                         precision=lax.Precision.HIGHEST) + b_ref[...]


def _ada_mods(c_rows, ada_w, ada_b):
    depth, d, n = ada_w.shape
    tn = 1024
    return pl.pallas_call(
        _ada_kernel,
        grid=(depth, n // tn),
        in_specs=[
            pl.BlockSpec((8, d), lambda l, j: (0, 0)),
            pl.BlockSpec((None, d, tn), lambda l, j: (l, 0, j)),
            pl.BlockSpec((None, 1, tn), lambda l, j: (l, 0, j)),
        ],
        out_specs=pl.BlockSpec((None, 8, tn), lambda l, j: (l, 0, j)),
        out_shape=jax.ShapeDtypeStruct((depth, 8, n), F32),
        compiler_params=_cparams(("arbitrary", "arbitrary")),
        name="ada_mods",
    )(c_rows, ada_w, ada_b.reshape(depth, 1, n))


def _norm_mod_kernel(x_ref, g_ref, shift_ref, scale_ref, h_ref):
    x = x_ref[...]
    y = x * lax.rsqrt(jnp.mean(x * x, axis=-1, keepdims=True) + NORM_EPS)
    h = y * g_ref[...] * (1.0 + scale_ref[...]) + shift_ref[...]
    h_ref[...] = h.astype(h_ref.dtype)


def _norm_mod_router_kernel(x_ref, g_ref, shift_ref, scale_ref, whi_ref, wlo_ref, h_ref, lg_ref):
    x = x_ref[...]
    y = x * lax.rsqrt(jnp.mean(x * x, axis=-1, keepdims=True) + NORM_EPS)
    h = y * g_ref[...] * (1.0 + scale_ref[...]) + shift_ref[...]
    h_hi = h.astype(BF16)
    h_lo = (h - h_hi.astype(F32)).astype(BF16)
    bits = lax.bitcast_convert_type(h_hi.astype(F32), jnp.uint32)
    half = bits.shape[1] // 2
    h_ref[...] = (bits[:, :half] >> 16) | bits[:, half:]
    lg_ref[...] = _dot(h_hi, whi_ref[...]) + _dot(h_hi, wlo_ref[...]) + _dot(h_lo, whi_ref[...])


def _norm_mod(xs, mods3, g3, layer, which, n_batch, tpb, router=None):
    t, d = xs.shape
    shift_k, scale_k = (0, 1) if which == 0 else (3, 4)
    nb = n_batch

    def mod_spec(k):
        return pl.BlockSpec((None, 1, d), lambda i: (layer * 8 + _row_sel(i, tpb, nb), 0, k))

    in_specs = [
        pl.BlockSpec((ROW_TILE, d), lambda i: (i, 0)),
        pl.BlockSpec((None, 1, d), lambda i: (layer * 2 + which, 0, 0)),
        mod_spec(shift_k),
        mod_spec(scale_k),
    ]
    args = [xs, g3, mods3, mods3]
    out_specs = [pl.BlockSpec((ROW_TILE, d), lambda i: (i, 0))]
    out_shape = [jax.ShapeDtypeStruct((t, d), BF16)]
    kern = _norm_mod_kernel
    if router is not None:
        in_specs += [pl.BlockSpec((d, LANES), lambda i: (0, 0))] * 2
        args += list(router)
        out_specs = [pl.BlockSpec((ROW_TILE, d // 2), lambda i: (i, 0))]
        out_shape = [jax.ShapeDtypeStruct((t, d // 2), jnp.uint32)]
        out_specs.append(pl.BlockSpec((ROW_TILE, LANES), lambda i: (i, 0)))
        out_shape.append(jax.ShapeDtypeStruct((t, LANES), F32))
        kern = _norm_mod_router_kernel
    return pl.pallas_call(
        kern,
        grid=(t // ROW_TILE,),
        in_specs=in_specs,
        out_specs=out_specs,
        out_shape=out_shape,
        compiler_params=_cparams(("arbitrary",)),
        name="norm_mod",
    )(*args)


def _rope_store(acc, cos_ref, sin_ref, o_ref):
    lane = lax.broadcasted_iota(jnp.int32, (1, LANES), 1)
    first = (lane % 64) < 32
    cos = cos_ref[...]
    sin = sin_ref[...]
    for c in range(acc.shape[1] // LANES):
        xc = acc[:, c * LANES:(c + 1) * LANES]
        sw = jnp.where(first, pltpu.roll(xc, 96, 1), pltpu.roll(xc, 32, 1))
        o_ref[:, c * LANES:(c + 1) * LANES] = (xc * cos + sw * sin).astype(o_ref.dtype)


def _mm_kernel(x_ref, w_ref, o_ref, wb_ref, *, n_q, q_scale):
    @pl.when(pl.program_id(1) == 0)
    def _():
        wb_ref[...] = w_ref[...].astype(BF16)

    acc = _dot(x_ref[...], wb_ref[...])
    j = pl.program_id(0)

    @pl.when(j < n_q)
    def _():
        o_ref[...] = (acc * q_scale).astype(o_ref.dtype)

    @pl.when(j >= n_q)
    def _():
        o_ref[...] = acc.astype(o_ref.dtype)


def _mm_rope_kernel(x_ref, w_ref, cos_ref, sin_ref, o_ref, wb_ref, *, n_q, q_scale, n_rope):
    @pl.when(pl.program_id(1) == 0)
    def _():
        wb_ref[...] = w_ref[...].astype(BF16)

    acc = _dot(x_ref[...], wb_ref[...])
    j = pl.program_id(0)

    @pl.when(j < n_q)
    def _():
        _rope_store(acc * q_scale, cos_ref, sin_ref, o_ref)

    @pl.when((j >= n_q) & (j < n_rope))
    def _():
        _rope_store(acc, cos_ref, sin_ref, o_ref)

    @pl.when(j >= n_rope)
    def _():
        o_ref[...] = acc.astype(o_ref.dtype)


def _mm_res_kernel(x_ref, w_ref, xs_ref, ga_ref, gb_ref, o_ref, wb_ref):
    @pl.when(pl.program_id(1) == 0)
    def _():
        wb_ref[...] = w_ref[...].astype(BF16)

    acc = _dot(x_ref[...], wb_ref[...])
    half = ROW_TILE
    o_ref[:half, :] = xs_ref[:half, :] + ga_ref[...] * acc[:half, :]
    o_ref[half:, :] = xs_ref[half:, :] + gb_ref[...] * acc[half:, :]


def _matmul(x, w, q_cols=0, q_scale=1.0, rope=None, n_rope_cols=0):
    t, k = x.shape
    n = w.shape[1]
    in_specs = [
        pl.BlockSpec((MM_TM, k), lambda j, i: (i, 0)),
        pl.BlockSpec((k, MM_TN), lambda j, i: (0, j)),
    ]
    args = [x, w]
    kern = functools.partial(_mm_kernel, n_q=q_cols // MM_TN, q_scale=q_scale)
    if rope is not None:
        in_specs += [pl.BlockSpec((MM_TM, LANES), lambda j, i: (i, 0))] * 2
        args += list(rope)
        kern = functools.partial(_mm_rope_kernel, n_q=q_cols // MM_TN, q_scale=q_scale,
                                 n_rope=n_rope_cols // MM_TN)
    return pl.pallas_call(
        kern,
        grid=(n // MM_TN, t // MM_TM),
        in_specs=in_specs,
        out_specs=pl.BlockSpec((MM_TM, MM_TN), lambda j, i: (i, j)),
        out_shape=jax.ShapeDtypeStruct((t, n), BF16),
        scratch_shapes=[pltpu.VMEM((k, MM_TN), BF16)],
        compiler_params=_cparams(("arbitrary", "arbitrary")),
        name="matmul",
    )(*args)


def _matmul_residual(x, w, xs, mods3, layer, gate_k, n_batch, tpb):
    t, k = x.shape
    n = w.shape[1]
    ncol = n // MM_TN
    nb = n_batch

    def gate_spec(half):
        return pl.BlockSpec(
            (None, 1, MM_TN),
            lambda j, i: (layer * 8 + _row_sel(2 * i + half, tpb, nb), 0, gate_k * ncol + j))

    return pl.pallas_call(
        _mm_res_kernel,
        grid=(ncol, t // MM_TM),
        in_specs=[
            pl.BlockSpec((MM_TM, k), lambda j, i: (i, 0)),
            pl.BlockSpec((k, MM_TN), lambda j, i: (0, j)),
            pl.BlockSpec((MM_TM, MM_TN), lambda j, i: (i, j)),
            gate_spec(0),
            gate_spec(1),
        ],
        out_specs=pl.BlockSpec((MM_TM, MM_TN), lambda j, i: (i, j)),
        out_shape=jax.ShapeDtypeStruct((t, n), F32),
        scratch_shapes=[pltpu.VMEM((k, MM_TN), BF16)],
        input_output_aliases={2: 0},
        compiler_params=_cparams(("arbitrary", "arbitrary")),
        name="matmul_residual",
    )(x, w, xs, mods3, mods3)


def _na_kernel(q_ref, k_ref, v_ref, tab_ref, o_ref, sl_ref, sc_ref, pl_ref, pc_ref, li_ref, *, rows):
    nq = NA_GROUP * GRID_W
    nk = NA_KROWS * GRID_W
    half_win = NA_WIN_ROWS // 2
    left = lax.broadcasted_iota(jnp.int32, (1, LANES), 1) < GRID_W

    s = _dot_nt(q_ref[0:CTX_LEN, :], k_ref[0:CTX_LEN, :])
    p = jnp.exp2(s - jnp.max(s, axis=-1, keepdims=True))
    l = jnp.sum(p, axis=-1, keepdims=True)
    o_ref[0:CTX_LEN, :] = (_dot(p.astype(BF16), v_ref[0:CTX_LEN, :]) / l).astype(o_ref.dtype)

    def geometry(g):
        r0 = jnp.asarray(g, jnp.int32) * NA_GROUP
        start = jnp.clip(r0 - half_win, 0, rows - NA_KROWS)
        q0 = pl.multiple_of(CTX_LEN + r0 * GRID_W, nq)
        k0 = pl.multiple_of(CTX_LEN + start * GRID_W, GRID_W)
        return r0, start, q0, k0

    def stage_a(g, par):
        _, _, q0, k0 = geometry(g)
        q = q_ref[pl.ds(q0, nq), :]
        sl_ref[par] = _dot_nt(q, k_ref[pl.ds(k0, nk), :])
        sc_ref[par] = _dot_nt(q, k_ref[0:CTX_LEN, :])

    def stage_b(g, par):
        r0, start, _, _ = geometry(g)
        for i in range(NA_GROUP):
            r = r0 + i
            rs = jnp.clip(r - half_win, 0, rows - NA_WIN_ROWS)
            rsl = slice(i * GRID_W, (i + 1) * GRID_W)
            tiles = []
            for jp in range(NA_KROWS // 2):
                kr = start + 2 * jp
                ok0 = ((kr >= rs) & (kr < rs + NA_WIN_ROWS)).astype(jnp.int32)
                ok1 = ((kr + 1 >= rs) & (kr + 1 < rs + NA_WIN_ROWS)).astype(jnp.int32)
                e = jnp.clip(kr - r + (NA_WIN_ROWS - 1) + 2, 0, 2 * NA_WIN_ROWS + 1)
                ok = jnp.where(left, ok0, ok1) > 0
                tiles.append(jnp.where(ok, tab_ref[e], NEG_BIG))
            sl = sl_ref[par, rsl, :] + jnp.concatenate(tiles, axis=1)
            sc = sc_ref[par, rsl, :]
            m = jnp.maximum(jnp.max(sl, axis=-1, keepdims=True), jnp.max(sc, axis=-1, keepdims=True))
            pl_i = jnp.exp2(sl - m)
            pc_i = jnp.exp2(sc - m)
            l_i = jnp.sum(pl_i, axis=-1, keepdims=True) + jnp.sum(pc_i, axis=-1, keepdims=True)
            pl_ref[par, rsl, :] = pl_i.astype(BF16)
            pc_ref[par, rsl, :] = pc_i.astype(BF16)
            li_ref[par, rsl, :] = jnp.broadcast_to(1.0 / l_i, (GRID_W, LANES))

    def stage_c(g, par):
        _, _, q0, k0 = geometry(g)
        o = _dot(pl_ref[par], v_ref[pl.ds(k0, nk), :]) + _dot(pc_ref[par], v_ref[0:CTX_LEN, :])
        o_ref[pl.ds(q0, nq), :] = (o * li_ref[par]).astype(o_ref.dtype)

    n_groups = rows // NA_GROUP
    stage_a(0, 0)
    stage_a(1, 1)
    stage_b(0, 0)

    def pair(u, carry):
        t = 2 + 2 * u
        stage_a(t, 0)
        stage_b(t - 1, 1)
        stage_c(t - 2, 0)
        stage_a(t + 1, 1)
        stage_b(t, 0)
        stage_c(t - 1, 1)
        return carry

    lax.fori_loop(0, (n_groups - 2) // 2, pair, 0)
    stage_b(n_groups - 1, 1)
    stage_c(n_groups - 2, 0)
    stage_c(n_groups - 1, 1)


def _na_bias_table(rpb):
    h = rpb.shape[0]
    qc = np.arange(GRID_W)[:, None]
    kc = np.arange(GRID_W)[None, :]
    cs = np.clip(qc - NA_WIN_COLS // 2, 0, GRID_W - NA_WIN_COLS)
    col_ok = (kc >= cs) & (kc < cs + NA_WIN_COLS)
    dc = np.clip(kc - qc + (NA_WIN_COLS - 1), 0, 2 * NA_WIN_COLS - 2)
    a = jnp.take(rpb.astype(F32), jnp.asarray(dc.reshape(-1)), axis=2)
    a = a.reshape(h, 2 * NA_WIN_ROWS - 1, GRID_W, GRID_W) * LOG2E
    a = jnp.where(jnp.asarray(col_ok)[None, None], a, NEG_BIG)
    a = jnp.pad(a, ((0, 0), (2, 2), (0, 0), (0, 0)), constant_values=NEG_BIG)
    return jnp.concatenate([a[:, :-1], a[:, 1:]], axis=-1)


def _na_attention(qkv, rpb, n_batch, tb):
    t = qkv.shape[0]
    rows = (tb - CTX_LEN) // GRID_W
    assert rows % (2 * NA_GROUP) == 0 and rows >= NA_KROWS
    nq, nk = NA_GROUP * GRID_W, NA_KROWS * GRID_W
    table = _na_bias_table(rpb)
    hh = NA_HEADS
    return pl.pallas_call(
        functools.partial(_na_kernel, rows=rows),
        grid=(hh, n_batch),
        in_specs=[
            pl.BlockSpec((tb, LANES), lambda h, b: (b, h)),
            pl.BlockSpec((tb, LANES), lambda h, b: (b, hh + h)),
            pl.BlockSpec((tb, LANES), lambda h, b: (b, 2 * hh + h)),
            pl.BlockSpec((None, 2 * NA_WIN_ROWS + 2, GRID_W, LANES), lambda h, b: (h, 0, 0, 0)),
        ],
        out_specs=pl.BlockSpec((tb, LANES), lambda h, b: (b, h)),
        out_shape=jax.ShapeDtypeStruct((t, D_MODEL), BF16),
        scratch_shapes=[
            pltpu.VMEM((2, nq, nk), F32),
            pltpu.VMEM((2, nq, CTX_LEN), F32),
            pltpu.VMEM((2, nq, nk), BF16),
            pltpu.VMEM((2, nq, CTX_LEN), BF16),
            pltpu.VMEM((2, nq, LANES), F32),
        ],
        compiler_params=_cparams(("arbitrary", "arbitrary")),
        name="na_attention",
    )(qkv, qkv, qkv, table)


def _da_kernel(lam_ref, g_ref, q_ref, k_ref, v_ref, o_ref, acc_ref, m_ref, l_ref,
               s0_ref, s1_ref, p0_ref, p1_ref, a0_ref, a1_ref, *, lam_init, n_kv_chunks):
    d = DIFF_HEAD_DIM
    i = pl.program_id(2)
    lam = lam_ref[...]
    lam_full = (jnp.exp(jnp.sum(lam[0:1] * lam[1:2], axis=-1, keepdims=True))
                - jnp.exp(jnp.sum(lam[2:3] * lam[3:4], axis=-1, keepdims=True)) + lam_init)
    s_bufs, p_bufs, a_bufs = (s0_ref, s1_ref), (p0_ref, p1_ref), (a0_ref, a1_ref)

    def tile(q0, nq, n_lat_chunks):
        acc_ref[:, 0:nq, :] = jnp.zeros((2, nq, 2 * d), F32)
        m_ref[:, 0:nq, :] = jnp.full((2, nq, LANES), NEG_BIG, F32)
        l_ref[:, 0:nq, :] = jnp.zeros((2, nq, LANES), F32)

        def key_rows(e):
            if isinstance(e, int):
                return (0, CTX_LEN) if e == 0 else (CTX_LEN + (e - 1) * DA_TK, DA_TK)
            return pl.multiple_of(CTX_LEN + (e - 1) * DA_TK, ROW_TILE), DA_TK

        def stage_a(e, par):
            k0, nk = key_rows(e)
            q = q_ref[pl.ds(q0, nq), :]
            k = k_ref[pl.ds(k0, nk), :]
            for c in range(2):
                s_bufs[par][c, 0:nq, 0:nk] = _dot_nt(q[:, c * d:(c + 1) * d], k[:, c * d:(c + 1) * d])

        def stage_b(e, par):
            _, nk = key_rows(e)
            s_ref, p_ref, a_ref = s_bufs[par], p_bufs[par], a_bufs[par]
            for c in range(2):
                for sb in range(nq // DA_SUB):
                    rows = slice(sb * DA_SUB, (sb + 1) * DA_SUB)
                    blocks = [s_ref[c, rows, j * LANES:(j + 1) * LANES] for j in range(nk // LANES)]
                    m_prev = m_ref[c, rows, :]
                    m_new = jnp.maximum(
                        m_prev, jnp.max(functools.reduce(jnp.maximum, blocks), axis=-1, keepdims=True))
                    alpha = jnp.exp2(m_prev - m_new)
                    p = [jnp.exp2(blk - m_new) for blk in blocks]
                    l_ref[c, rows, :] = alpha * l_ref[c, rows, :] + functools.reduce(jnp.add, p)
                    for j, pj in enumerate(p):
                        p_ref[c, rows, j * LANES:(j + 1) * LANES] = pj.astype(BF16)
                    a_ref[c, rows, :] = alpha
                    m_ref[c, rows, :] = m_new

        def stage_c(e, par):
            k0, nk = key_rows(e)
            v = v_ref[pl.ds(k0, nk), :]
            for c in range(2):
                pv = _dot(p_bufs[par][c, 0:nq, 0:nk], v)
                alpha = a_bufs[par][c, 0:nq, :]
                acc_ref[c, 0:nq, 0:LANES] = alpha * acc_ref[c, 0:nq, 0:LANES] + pv[:, :LANES]
                acc_ref[c, 0:nq, LANES:] = alpha * acc_ref[c, 0:nq, LANES:] + pv[:, LANES:]

        n = n_lat_chunks
        if n == 0:
            stage_a(0, 0)
            stage_b(0, 0)
            stage_c(0, 0)
        else:
            assert n % 2 == 0
            stage_a(0, 0)
            stage_a(1, 1)
            stage_b(0, 0)
            stage_a(2, 0)
            stage_b(1, 1)
            stage_c(0, 0)

            def pair(u, carry):
                t = 3 + 2 * u
                stage_a(t, 1)
                stage_b(t - 1, 0)
                stage_c(t - 2, 1)
                stage_a(t + 1, 0)
                stage_b(t, 1)
                stage_c(t - 1, 0)
                return carry

            lax.fori_loop(0, (n - 2) // 2, pair, 0)
            stage_b(n, 0)
            stage_c(n - 1, 1)
            stage_c(n, 0)

        r0 = 1.0 / jnp.sum(l_ref[0, 0:nq, :], axis=-1, keepdims=True)
        r1 = lam_full / jnp.sum(l_ref[1, 0:nq, :], axis=-1, keepdims=True)
        o_a = acc_ref[0, 0:nq, 0:LANES] * r0 - acc_ref[1, 0:nq, 0:LANES] * r1
        o_b = acc_ref[0, 0:nq, LANES:] * r0 - acc_ref[1, 0:nq, LANES:] * r1
        ms = (jnp.sum(o_a * o_a, axis=-1, keepdims=True) + jnp.sum(o_b * o_b, axis=-1, keepdims=True)) / (2 * d)
        inv = lax.rsqrt(ms + NORM_EPS) * (1.0 - lam_init)
        g = g_ref[...]
        o_ref[pl.ds(q0, nq), 0:LANES] = (o_a * inv * g[:, :LANES]).astype(o_ref.dtype)
        o_ref[pl.ds(q0, nq), LANES:] = (o_b * inv * g[:, LANES:]).astype(o_ref.dtype)

    @pl.when(i == 0)
    def _():
        tile(0, CTX_LEN, 0)

    @pl.when(i > 0)
    def _():
        tile(pl.multiple_of(CTX_LEN + (i - 1) * DA_TQ, ROW_TILE), DA_TQ, n_kv_chunks)


def _diff_attention(qkv, lam, subln_g, layer_idx, n_batch, tb):
    t = qkv.shape[0]
    hh = DIFF_HEADS
    w = 2 * DIFF_HEAD_DIM
    n_lat = (tb - CTX_LEN) // DA_TQ
    lam_init = 0.8 - 0.6 * math.exp(-0.3 * layer_idx)
    return pl.pallas_call(
        functools.partial(_da_kernel, lam_init=lam_init, n_kv_chunks=(tb - CTX_LEN) // DA_TK),
        grid=(n_batch, hh, 1 + n_lat),
        in_specs=[
            pl.BlockSpec((4, DIFF_HEAD_DIM), lambda b, h, i: (0, 0)),
            pl.BlockSpec((1, w), lambda b, h, i: (0, 0)),
            pl.BlockSpec((tb, w), lambda b, h, i: (b, h)),
            pl.BlockSpec((tb, w), lambda b, h, i: (b, hh + h)),
            pl.BlockSpec((tb, w), lambda b, h, i: (b, 2 * hh + h)),
        ],
        out_specs=pl.BlockSpec((tb, w), lambda b, h, i: (b, h)),
        out_shape=jax.ShapeDtypeStruct((t, D_MODEL), BF16),
        scratch_shapes=[
            pltpu.VMEM((2, DA_TQ, w), F32),
            pltpu.VMEM((2, DA_TQ, LANES), F32),
            pltpu.VMEM((2, DA_TQ, LANES), F32),
            pltpu.VMEM((2, DA_TQ, DA_TK), F32),
            pltpu.VMEM((2, DA_TQ, DA_TK), F32),
            pltpu.VMEM((2, DA_TQ, DA_TK), BF16),
            pltpu.VMEM((2, DA_TQ, DA_TK), BF16),
            pltpu.VMEM((2, DA_TQ, LANES), F32),
            pltpu.VMEM((2, DA_TQ, LANES), F32),
        ],
        compiler_params=_cparams(("arbitrary", "arbitrary", "arbitrary")),
        name="diff_attention",
    )(lam.astype(F32), subln_g.astype(F32).reshape(1, w), qkv, qkv, qkv)


def _hg_kernel(q_ref, i_ref, ff_ref, fb_ref, lb_ref, o_ref, stf_ref, stb_ref, *, n_groups):
    c = HGRN_CHUNK
    rg = HG_GROUP * c
    dk = HGRN_DK
    o_ref[...] = jnp.zeros(o_ref.shape, F32)
    stf_ref[...] = jnp.zeros(stf_ref.shape, F32)
    stb_ref[...] = jnp.zeros(stb_ref.shape, F32)
    row = lax.broadcasted_iota(jnp.int32, (rg, rg), 0)
    col = lax.broadcasted_iota(jnp.int32, (rg, rg), 1)
    same = (row // c) == (col // c)
    lower = same & (row >= col)
    upper = same & (col >= row)
    lower_b = lower.astype(BF16)
    upper_b = upper.astype(BF16)
    chunk_of_row = lax.broadcasted_iota(jnp.int32, (rg, 1), 0) // c
    lb_all = lb_ref[...]

    def per_chunk_rows(b, r):
        return jnp.concatenate(
            [jnp.broadcast_to(b[g * c + r:g * c + r + 1, :], (c, dk)) for g in range(HG_GROUP)], axis=0)

    def group(gidx, f_ref, lb, mask, mask_b, st_ref, end_row, order):
        r0 = pl.multiple_of(gidx * rg, rg)
        qr = q_ref[pl.ds(r0, rg), :].astype(F32)
        q = qr * jax.nn.sigmoid(qr) * (dk ** -0.5)
        v = i_ref[pl.ds(r0, rg), :]
        fr = f_ref[pl.ds(r0, rg), :].astype(F32)
        fg = lb + (1.0 - lb) * jax.nn.sigmoid(fr)
        k = 1.0 - fg
        lf = jnp.log(fg)
        hi = lf.astype(BF16)
        rem = lf - hi.astype(F32)
        mid = rem.astype(BF16)
        lo = (rem - mid.astype(F32)).astype(BF16)
        cs = _dot(mask_b, jnp.concatenate([hi, mid, lo], axis=1))
        bcum = cs[:, 0:dk] + cs[:, dk:2 * dk] + cs[:, 2 * dk:]
        b_end = per_chunk_rows(bcum, end_row)
        b_mid = per_chunk_rows(bcum, c // 2)
        qt = (q * jnp.exp(bcum - b_mid)).astype(BF16)
        kt = (k * jnp.exp(b_mid - bcum)).astype(BF16)
        a = jnp.where(mask, _dot_nt(qt, kt), 0.0).astype(BF16)
        o_intra = _dot(a, v)
        k_end = k * jnp.exp(b_end - bcum)
        k_exp = jnp.concatenate(
            [jnp.where(chunk_of_row == g, k_end, 0.0) for g in range(HG_GROUP)], axis=1).astype(BF16)
        u = _dot_tn(v, k_exp)
        st = st_ref[...]
        st_in = [None] * HG_GROUP
        for g in order:
            st_in[g] = st.astype(BF16)
            st = st * jnp.exp(bcum[g * c + end_row:g * c + end_row + 1, :]) + u[:, g * dk:(g + 1) * dk]
        st_ref[...] = st
        oi = _dot_nt((q * jnp.exp(bcum)).astype(BF16), jnp.concatenate(st_in, axis=0))
        o_inter = jnp.concatenate(
            [oi[g * c:(g + 1) * c, g * dk:(g + 1) * dk] for g in range(HG_GROUP)], axis=0)
        o_ref[pl.ds(r0, rg), :] += o_intra + o_inter

    fwd_order = list(range(HG_GROUP))

    def body(n, carry):
        group(n, ff_ref, lb_all[0:1, :], lower, lower_b, stf_ref, c - 1, fwd_order)
        nb = jnp.where(n == 0, 0, n_groups - n)
        group(nb, fb_ref, lb_all[1:2, :], upper, upper_b, stb_ref, 0, fwd_order[::-1])
        return carry

    lax.fori_loop(0, n_groups, body, 0)


def _hgrn_scan(proj, lb2, n_batch, tb):
    t = proj.shape[0]
    hh = HGRN_HEADS
    rg = HG_GROUP * HGRN_CHUNK
    assert CTX_LEN == rg and tb % rg == 0

    def col(k):
        return pl.BlockSpec((tb, LANES), lambda b, h: (b, k * hh + h))

    return pl.pallas_call(
        functools.partial(_hg_kernel, n_groups=tb // rg),
        grid=(n_batch, hh),
        in_specs=[col(0), col(1), col(3), col(4), pl.BlockSpec((2, LANES), lambda b, h: (0, h))],
        out_specs=pl.BlockSpec((tb, LANES), lambda b, h: (b, h)),
        out_shape=jax.ShapeDtypeStruct((t, D_MODEL), F32),
        scratch_shapes=[pltpu.VMEM((LANES, LANES), F32), pltpu.VMEM((LANES, LANES), F32)],
        compiler_params=_cparams(("arbitrary", "arbitrary")),
        name="hgrn_scan",
    )(proj, proj, proj, proj, lb2)


def _hg_finish_kernel(o_ref, g_ref, ng_ref, h_ref):
    o = o_ref[...]
    y = o * lax.rsqrt(jnp.mean(o * o, axis=-1, keepdims=True) + NORM_EPS) * ng_ref[...]
    g = g_ref[...].astype(F32)
    h_ref[...] = (y * (g * jax.nn.sigmoid(g))).astype(h_ref.dtype)


def _hgrn_finish(o, proj, norm_g):
    t, d = o.shape
    return pl.pallas_call(
        _hg_finish_kernel,
        grid=(t // ROW_TILE,),
        in_specs=[
            pl.BlockSpec((ROW_TILE, d), lambda i: (i, 0)),
            pl.BlockSpec((ROW_TILE, d), lambda i: (i, 2)),
            pl.BlockSpec((1, d), lambda i: (0, 0)),
        ],
        out_specs=pl.BlockSpec((ROW_TILE, d), lambda i: (i, 0)),
        out_shape=jax.ShapeDtypeStruct((t, d), BF16),
        compiler_params=_cparams(("arbitrary",)),
        name="hgrn_finish",
    )(o, proj, norm_g.astype(F32).reshape(1, d))


def _moe_kernel(be_ref, first_ref, nxt_ref, nu_ref, tok_ref, x_hbm, wg_hbm, wu_hbm, wd_hbm, o_ref,
                xbuf, sg_ref, su_ref, sd_ref, bg_ref, bu_ref, bd_ref, sem, xsem, *, layer):
    b = pl.program_id(0)
    n_used = nu_ref[0]

    def weight_copies(e):
        return (pltpu.make_async_copy(wg_hbm.at[layer, e], sg_ref, sem.at[0]),
                pltpu.make_async_copy(wu_hbm.at[layer, e], su_ref, sem.at[1]),
                pltpu.make_async_copy(wd_hbm.at[layer, e], sd_ref, sem.at[2]))

    def start_gather(blk, slot):
        base = blk * MOE_TM
        for r in range(MOE_TM):
            pltpu.make_async_copy(x_hbm.at[pl.ds(tok_ref[base + r], 1)], xbuf.at[slot, pl.ds(r, 1)],
                                  xsem.at[slot]).start()

    def wait_gather(slot):
        pltpu.make_async_copy(x_hbm.at[pl.ds(0, MOE_TM)], xbuf.at[slot], xsem.at[slot]).wait()

    @pl.when(b == 0)
    def _():
        for cp in weight_copies(be_ref[0]):
            cp.start()
        start_gather(0, 0)
        start_gather(1, 1)

    @pl.when(first_ref[b] == 1)
    def _():
        for cp in weight_copies(be_ref[b]):
            cp.wait()
        for src, dst in ((sg_ref, bg_ref), (su_ref, bu_ref), (sd_ref, bd_ref)):
            slab = src.shape[0] // MOE_CAST_SLABS

            def cast(r, carry, src=src, dst=dst, slab=slab):
                r0 = pl.multiple_of(r * slab, slab)
                dst[pl.ds(r0, slab), :] = src[pl.ds(r0, slab), :].astype(BF16)
                return carry

            lax.fori_loop(0, MOE_CAST_SLABS, cast, 0)

        @pl.when(nxt_ref[b] >= 0)
        def _():
            for cp in weight_copies(nxt_ref[b]):
                cp.start()

    @pl.when(b < n_used)
    def _():
        slot = b % MOE_XBUFS
        wait_gather(slot)
        xu = xbuf[slot]
        x = jnp.concatenate(
            [lax.bitcast_convert_type(xu << 16, F32).astype(BF16),
             lax.bitcast_convert_type(xu & jnp.uint32(0xFFFF0000), F32).astype(BF16)], axis=1)
        g = _dot(x, bg_ref[...])
        u = _dot(x, bu_ref[...])
        h = (g * jax.nn.sigmoid(g) * u).astype(BF16)
        o_ref[...] = _dot(h, bd_ref[...]).astype(o_ref.dtype)
        start_gather(b + 2, (b + 2) % MOE_XBUFS)

    @pl.when(b >= n_used)
    def _():
        @pl.when(b == n_used)
        def _():
            wait_gather(b % MOE_XBUFS)
            wait_gather((b + 1) % MOE_XBUFS)

        o_ref[...] = jnp.zeros(o_ref.shape, o_ref.dtype)


def _moe_experts(hp, slot_tok, blk_expert, blk_first, blk_next, n_used, w_gate, w_up, w_down, layer):
    d = 2 * hp.shape[1]
    f = w_gate.shape[3]
    n_slots = slot_tok.shape[0]
    n_blocks = n_slots // MOE_TM
    grid_spec = pltpu.PrefetchScalarGridSpec(
        num_scalar_prefetch=5,
        grid=(n_blocks,),
        in_specs=[pl.BlockSpec(memory_space=pl.ANY)] * 4,
        out_specs=pl.BlockSpec((MOE_TM, d), lambda b, *_: (b, 0)),
        scratch_shapes=[
            pltpu.VMEM((MOE_XBUFS, MOE_TM, d // 2), jnp.uint32),
            pltpu.VMEM((d, f), F32), pltpu.VMEM((d, f), F32), pltpu.VMEM((f, d), F32),
            pltpu.VMEM((d, f), BF16), pltpu.VMEM((d, f), BF16), pltpu.VMEM((f, d), BF16),
            pltpu.SemaphoreType.DMA((3,)),
            pltpu.SemaphoreType.DMA((MOE_XBUFS,)),
        ],
    )
    return pl.pallas_call(
        functools.partial(_moe_kernel, layer=layer),
        grid_spec=grid_spec,
        out_shape=jax.ShapeDtypeStruct((n_slots, d), BF16),
        compiler_params=_cparams(("arbitrary",)),
        name="moe_experts",
    )(blk_expert, blk_first, blk_next, n_used, slot_tok, hp, w_gate, w_up, w_down)


def _route(logits, b_router, n_blocks):
    t = logits.shape[0]
    s = jax.nn.sigmoid(logits)
    sel = s + b_router.astype(F32)

    def top2(vals):
        ids = jnp.arange(vals.shape[-1], dtype=jnp.int32)
        i1 = jnp.argmax(vals, axis=-1).astype(jnp.int32)
        v1 = jnp.max(vals, axis=-1)
        rest = jnp.where(ids == i1[..., None], -jnp.inf, vals)
        i2 = jnp.argmax(rest, axis=-1).astype(jnp.int32)
        return v1, jnp.max(rest, axis=-1), i1, i2

    g1, g2, _, _ = top2(sel.reshape(t, N_GROUPS, EXPERTS_PER_GROUP))
    g_best = jnp.argmax(g1 + g2, axis=-1)
    in_grp = (jnp.arange(N_EXPERTS) // EXPERTS_PER_GROUP)[None, :] == g_best[:, None]
    _, _, e1, e2 = top2(jnp.where(in_grp, sel, -jnp.inf))
    e_idx = jnp.stack([e1, e2], axis=-1)
    w = jnp.take_along_axis(s, e_idx, axis=1)
    w = w / jnp.sum(w, axis=-1, keepdims=True)
    flat_e = e_idx.reshape(-1).astype(jnp.int32)
    onehot = (flat_e[:, None] == jnp.arange(N_EXPERTS, dtype=jnp.int32)[None, :]).astype(jnp.int32)
    csum = jnp.cumsum(onehot, axis=0)
    rank = jnp.sum((csum - onehot) * onehot, axis=1)
    counts = csum[-1]
    pcounts = (counts + MOE_TM - 1) // MOE_TM * MOE_TM
    pends = jnp.cumsum(pcounts)
    pstarts = pends - pcounts
    dest = (pstarts[flat_e] + rank).astype(jnp.int32)
    flat_t = jnp.arange(t * TOP_K, dtype=jnp.int32) // TOP_K
    slot_tok = jnp.zeros((n_blocks * MOE_TM,), jnp.int32).at[dest].set(flat_t)
    blk_expert = jnp.minimum(
        jnp.searchsorted(pends, jnp.arange(n_blocks, dtype=jnp.int32) * MOE_TM, side='right'),
        N_EXPERTS - 1).astype(jnp.int32)
    n_used = (pends[-1] // MOE_TM).astype(jnp.int32).reshape(1)
    blk_ids = jnp.arange(n_blocks, dtype=jnp.int32)
    prev_expert = jnp.concatenate([jnp.full((1,), -1, jnp.int32), blk_expert[:-1]])
    blk_first = ((blk_ids < n_used[0]) & (blk_expert != prev_expert)).astype(jnp.int32)
    e_ids = jnp.arange(N_EXPERTS, dtype=jnp.int32)
    later_used = (e_ids[None, :] > e_ids[:, None]) & (counts[None, :] > 0)
    next_used = jnp.min(jnp.where(later_used, e_ids[None, :], N_EXPERTS), axis=1)
    next_used = jnp.where(next_used == N_EXPERTS, -1, next_used).astype(jnp.int32)
    blk_next = next_used[blk_expert]
    return w, dest.reshape(t, TOP_K), slot_tok, (blk_expert, blk_first, blk_next, n_used)


def _moe_res_kernel(xs_ref, y0_ref, y1_ref, w_ref, gate_ref, o_ref):
    w = w_ref[...]
    y = w[:, 0:1] * y0_ref[...].astype(F32) + w[:, 1:2] * y1_ref[...].astype(F32)
    o_ref[...] = xs_ref[...] + gate_ref[...] * y


def _moe_res_final_kernel(xs_ref, y0_ref, y1_ref, w_ref, gate_ref, fg_ref, o_ref):
    w = w_ref[...]
    y = w[:, 0:1] * y0_ref[...].astype(F32) + w[:, 1:2] * y1_ref[...].astype(F32)
    x = xs_ref[...] + gate_ref[...] * y
    o_ref[...] = x * lax.rsqrt(jnp.mean(x * x, axis=-1, keepdims=True) + NORM_EPS) * fg_ref[...]


def _moe_residual(xs, y0, y1, w, mods3, layer, n_batch, tpb):
    t, d = xs.shape
    nb = n_batch
    row = lambda i: (i, 0)
    return pl.pallas_call(
        _moe_res_kernel,
        grid=(t // ROW_TILE,),
        in_specs=[
            pl.BlockSpec((ROW_TILE, d), row),
            pl.BlockSpec((ROW_TILE, d), row),
            pl.BlockSpec((ROW_TILE, d), row),
            pl.BlockSpec((ROW_TILE, TOP_K), row),
            pl.BlockSpec((None, 1, d), lambda i: (layer * 8 + _row_sel(i, tpb, nb), 0, 5)),
        ],
        out_specs=pl.BlockSpec((ROW_TILE, d), row),
        out_shape=jax.ShapeDtypeStruct((t, d), F32),
        input_output_aliases={0: 0},
        compiler_params=_cparams(("arbitrary",)),
        name="moe_residual",
    )(xs, y0, y1, w, mods3)


def _moe_residual_final(xs, y0, y1, w, mods3, layer, final_g, n_batch, tpb):
    t, d = xs.shape
    lat = tpb - CTX_LEN // ROW_TILE
    src = lambda i: ((i // lat) * tpb + CTX_LEN // ROW_TILE + i % lat, 0)
    return pl.pallas_call(
        _moe_res_final_kernel,
        grid=(n_batch * lat,),
        in_specs=[
            pl.BlockSpec((ROW_TILE, d), src),
            pl.BlockSpec((ROW_TILE, d), src),
            pl.BlockSpec((ROW_TILE, d), src),
            pl.BlockSpec((ROW_TILE, TOP_K), src),
            pl.BlockSpec((None, 1, d), lambda i: (layer * 8 + i // lat, 0, 5)),
            pl.BlockSpec((1, d), lambda i: (0, 0)),
        ],
        out_specs=pl.BlockSpec((ROW_TILE, d), lambda i: (i, 0)),
        out_shape=jax.ShapeDtypeStruct((n_batch * lat * ROW_TILE, d), F32),
        compiler_params=_cparams(("arbitrary",)),
        name="moe_residual_final",
    )(xs, y0, y1, w, mods3, final_g.astype(F32).reshape(1, d))


def _rope_tables(n_batch, seq):
    quarter = DIFF_HEAD_DIM // 4
    tpos = jnp.arange(seq)
    rowp = (tpos // GRID_W).astype(F32)
    colp = (tpos % GRID_W).astype(F32)
    inv = ROPE_THETA ** (-jnp.arange(quarter, dtype=F32) / quarter)
    ang_r = rowp[:, None] * inv[None, :]
    ang_c = colp[:, None] * inv[None, :]
    cos = jnp.concatenate([jnp.cos(ang_r)] * 2 + [jnp.cos(ang_c)] * 2, axis=-1)
    sin = jnp.concatenate([-jnp.sin(ang_r), jnp.sin(ang_r), -jnp.sin(ang_c), jnp.sin(ang_c)], axis=-1)
    cos = jnp.concatenate([jnp.ones((CTX_LEN, LANES), F32), cos], axis=0)
    sin = jnp.concatenate([jnp.zeros((CTX_LEN, LANES), F32), sin], axis=0)
    return jnp.tile(cos, (n_batch, 1)), jnp.tile(sin, (n_batch, 1))


def kernel(x, c, ctx, c_ctx, ada_w, ada_b, norm_g, na_w_qkv, na_rpb, na_w_o, da_w_qkv, da_lambda, da_subln_g, da_w_o, hg_w_in, hg_lb, hg_norm_g, hg_w_o, moe_w_router, moe_b_router, moe_w_gate, moe_w_up, moe_w_down, final_g):
    n_batch, seq, d = x.shape
    tb = CTX_LEN + seq
    t = n_batch * tb
    tpb = tb // ROW_TILE
    assert d == D_MODEL and ctx.shape[1] == CTX_LEN and n_batch + 1 <= 8
    assert seq % DA_TQ == 0 and t % MM_TM == 0 and seq // GRID_W >= NA_WIN_ROWS

    xs = jnp.concatenate([ctx, x], axis=1).reshape(t, d).astype(F32)

    c_rows = jnp.zeros((8, d), F32).at[:n_batch].set(c.astype(F32)).at[n_batch].set(c_ctx.astype(F32))
    mods = _ada_mods(c_rows, ada_w.astype(F32), ada_b.astype(F32))
    mods3 = mods.reshape(DEPTH * 8, 1, 6 * d)
    g3 = norm_g.astype(F32).reshape(DEPTH * 2, 1, d)

    lb_p = jax.nn.softmax(hg_lb.astype(F32), axis=1)
    lb_all = jnp.cumsum(lb_p, axis=1) - lb_p[:, :1]

    wr = jnp.zeros((d, LANES), F32).at[:, :N_EXPERTS].set(moe_w_router.astype(F32))
    wr_hi = wr.astype(BF16)
    wr_lo = (wr - wr_hi.astype(F32)).astype(BF16)

    n_blocks = -(-(t * TOP_K) // MOE_TM) + N_EXPERTS + (MOE_XBUFS - 1)
    rope = _rope_tables(n_batch, seq)
    wg = moe_w_gate.astype(F32)
    wu = moe_w_up.astype(F32)
    wd = moe_w_down.astype(F32)
    na_qs = NA_HEAD_DIM ** -0.5 * LOG2E
    da_qs = DIFF_HEAD_DIM ** -0.5 * LOG2E

    counters = [0, 0, 0]
    out = None
    for i in range(DEPTH):
        kind = i % N_MIXERS
        j = counters[kind]
        counters[kind] += 1

        h = _norm_mod(xs, mods3, g3, i, 0, n_batch, tpb)[0]
        if kind == 0:
            qkv = _matmul(h, na_w_qkv[j], q_cols=d, q_scale=na_qs)
            o = _na_attention(qkv, na_rpb[j], n_batch, tb)
            w_o = na_w_o[j]
        elif kind == 1:
            qkv = _matmul(h, da_w_qkv[j], q_cols=d, q_scale=da_qs, rope=rope, n_rope_cols=2 * d)
            o = _diff_attention(qkv, da_lambda[j], da_subln_g[j], i, n_batch, tb)
            w_o = da_w_o[j]
        else:
            proj = _matmul(h, hg_w_in[j])
            o32 = _hgrn_scan(proj, lb_all[:, i, :], n_batch, tb)
            o = _hgrn_finish(o32, proj, hg_norm_g[j])
            w_o = hg_w_o[j]
        xs = _matmul_residual(o, w_o, xs, mods3, i, 2, n_batch, tpb)

        hp, logits = _norm_mod(xs, mods3, g3, i, 1, n_batch, tpb, router=(wr_hi, wr_lo))
        w, dest, slot_tok, blk_info = _route(logits[:, :N_EXPERTS], moe_b_router, n_blocks)
        yb = _moe_experts(hp, slot_tok, *blk_info, wg, wu, wd, i)
        y0 = yb.at[dest[:, 0]].get(mode="promise_in_bounds")
        y1 = yb.at[dest[:, 1]].get(mode="promise_in_bounds")
        if i < DEPTH - 1:
            xs = _moe_residual(xs, y0, y1, w, mods3, i, n_batch, tpb)
        else:
            out = _moe_residual_final(xs, y0, y1, w, mods3, i, final_g, n_batch, tpb)
    return out.reshape(n_batch, seq, d).astype(x.dtype)
```

```python
import functools
import math

import numpy as np
import jax
import jax.numpy as jnp
from jax import lax
from jax.experimental import pallas as pl
from jax.experimental.pallas import tpu as pltpu

F32 = jnp.float32
BF16 = jnp.bfloat16

D_MODEL = 2048
DEPTH = 4
GRID_W = 64
CTX_LEN = 256
N_MIXERS = 3
NORM_EPS = 1e-6
ROPE_THETA = 10000.0

NA_HEADS = 16
NA_HEAD_DIM = 128
NA_WIN_ROWS = 8
NA_WIN_COLS = 16

DIFF_HEADS = 8
DIFF_HEAD_DIM = 128

HGRN_HEADS = 16
HGRN_DK = 128
HGRN_CHUNK = 64

N_EXPERTS = 16
N_GROUPS = 4
EXPERTS_PER_GROUP = 4
TOP_K = 2
D_EXPERT = 1024

LANES = 128
ROW_TILE = 256
MM_TM = 512
MM_TN = 1024
MOE_TM = 256
DA_TQ = 512
VMEM_LIMIT = 56 * 1024 * 1024

NA_GROUP = 4
NA_KROWS = 12
DA_SUB = 64
DA_TK = 512
HG_GROUP = 4
MOE_XBUFS = 3
MOE_CAST_SLABS = 8

NEG_BIG = -1e30
LOG2E = 1.4426950408889634


def _cparams(sem):
    return pltpu.CompilerParams(dimension_semantics=sem, vmem_limit_bytes=VMEM_LIMIT)


def _dot(a, b):
    return jnp.dot(a, b, preferred_element_type=F32)


def _dot_nt(a, b):
    return lax.dot_general(a, b, (((1,), (1,)), ((), ())), preferred_element_type=F32)


def _dot_tn(a, b):
    return lax.dot_general(a, b, (((0,), (0,)), ((), ())), preferred_element_type=F32)


def _row_sel(sub, tiles_per_batch, n_batch):
    return jnp.where(sub % tiles_per_batch == 0, n_batch, sub // tiles_per_batch)


def _ada_kernel(c_ref, w_ref, b_ref, o_ref):
    c = c_ref[...]
    a = c * jax.nn.sigmoid(c)
    o_ref[...] = jnp.dot(a, w_ref[...], preferred_element_type=F32,
                         precision=lax.Precision.HIGHEST) + b_ref[...]


def _ada_mods(c_rows, ada_w, ada_b):
    depth, d, n = ada_w.shape
    tn = 1024
    return pl.pallas_call(
        _ada_kernel,
        grid=(depth, n // tn),
        in_specs=[
            pl.BlockSpec((8, d), lambda l, j: (0, 0)),
            pl.BlockSpec((None, d, tn), lambda l, j: (l, 0, j)),
            pl.BlockSpec((None, 1, tn), lambda l, j: (l, 0, j)),
        ],
        out_specs=pl.BlockSpec((None, 8, tn), lambda l, j: (l, 0, j)),
        out_shape=jax.ShapeDtypeStruct((depth, 8, n), F32),
        compiler_params=_cparams(("arbitrary", "arbitrary")),
        name="ada_mods",
    )(c_rows, ada_w, ada_b.reshape(depth, 1, n))


def _norm_mod_kernel(x_ref, g_ref, shift_ref, scale_ref, h_ref):
    x = x_ref[...]
    y = x * lax.rsqrt(jnp.mean(x * x, axis=-1, keepdims=True) + NORM_EPS)
    h = y * g_ref[...] * (1.0 + scale_ref[...]) + shift_ref[...]
    h_ref[...] = h.astype(h_ref.dtype)


def _route_tile(lg, bias, run_ref):
    rows = lg.shape[0]
    lane = lax.broadcasted_iota(jnp.int32, (1, LANES), 1)
    lanef = lane.astype(F32)
    neg = -jnp.inf
    far = float(LANES)
    rmax = lambda a: jnp.max(a, axis=-1, keepdims=True)
    rmin = lambda a: jnp.min(a, axis=-1, keepdims=True)
    rsum = lambda a: jnp.sum(a, axis=-1, keepdims=True)

    s = jax.nn.sigmoid(lg)
    sel = jnp.where(lane < N_EXPERTS, s + bias, neg)
    best = None
    for g in range(N_GROUPS):
        ing = (lane // EXPERTS_PER_GROUP) == g
        v = jnp.where(ing, sel, neg)
        m1 = rmax(v)
        i1 = rmin(jnp.where(v == m1, lanef, far))
        v2 = jnp.where(lanef == i1, neg, v)
        m2 = rmax(v2)
        i2 = rmin(jnp.where(v2 == m2, lanef, far))
        cand = (m1 + m2, i1, i2)
        if best is None:
            best = cand
        else:
            better = cand[0] > best[0]
            best = tuple(jnp.where(better, c, b) for c, b in zip(cand, best))
    _, e1, e2 = best
    is1 = lanef == e1
    is2 = lanef == e2
    s1 = rsum(jnp.where(is1, s, 0.0))
    s2 = rsum(jnp.where(is2, s, 0.0))
    w1 = s1 / (s1 + s2)
    w2 = s2 / (s1 + s2)
    onehot = jnp.where(is1 | is2, 1.0, 0.0)
    row = lax.broadcasted_iota(jnp.int32, (rows, rows), 0)
    col = lax.broadcasted_iota(jnp.int32, (rows, rows), 1)
    before = jnp.where(row > col, 1.0, 0.0).astype(BF16)
    prior = _dot(before, onehot.astype(BF16)) + run_ref[...]
    r1 = rsum(jnp.where(is1, prior, 0.0))
    r2 = rsum(jnp.where(is2, prior, 0.0))
    run_ref[...] += jnp.sum(onehot, axis=0, keepdims=True)
    out = jnp.zeros((rows, LANES), F32)
    for k, val in enumerate((e1, e2, r1, r2, w1, w2)):
        out = jnp.where(lane == k, val, out)
    return out


def _norm_mod_router_kernel(x_ref, g_ref, shift_ref, scale_ref, whi_ref, wlo_ref, rb_ref,
                            h_ref, route_ref, cnt_ref, run_ref):
    @pl.when(pl.program_id(0) == 0)
    def _():
        run_ref[...] = jnp.zeros(run_ref.shape, F32)

    x = x_ref[...]
    y = x * lax.rsqrt(jnp.mean(x * x, axis=-1, keepdims=True) + NORM_EPS)
    h = y * g_ref[...] * (1.0 + scale_ref[...]) + shift_ref[...]
    h_hi = h.astype(BF16)
    h_lo = (h - h_hi.astype(F32)).astype(BF16)
    bits = lax.bitcast_convert_type(h_hi.astype(F32), jnp.uint32)
    half = bits.shape[1] // 2
    h_ref[...] = (bits[:, :half] >> 16) | bits[:, half:]
    lg = _dot(h_hi, whi_ref[...]) + _dot(h_hi, wlo_ref[...]) + _dot(h_lo, whi_ref[...])
    route_ref[...] = _route_tile(lg, rb_ref[...], run_ref)
    cnt_ref[...] = run_ref[...]


def _norm_mod(xs, mods3, g3, layer, which, n_batch, tpb, router=None):
    t, d = xs.shape
    shift_k, scale_k = (0, 1) if which == 0 else (3, 4)
    nb = n_batch

    def mod_spec(k):
        return pl.BlockSpec((None, 1, d), lambda i: (layer * 8 + _row_sel(i, tpb, nb), 0, k))

    in_specs = [
        pl.BlockSpec((ROW_TILE, d), lambda i: (i, 0)),
        pl.BlockSpec((None, 1, d), lambda i: (layer * 2 + which, 0, 0)),
        mod_spec(shift_k),
        mod_spec(scale_k),
    ]
    args = [xs, g3, mods3, mods3]
    out_specs = [pl.BlockSpec((ROW_TILE, d), lambda i: (i, 0))]
    out_shape = [jax.ShapeDtypeStruct((t, d), BF16)]
    kern = _norm_mod_kernel
    scratch = []
    if router is not None:
        in_specs += [pl.BlockSpec((d, LANES), lambda i: (0, 0))] * 2 + [pl.BlockSpec((1, LANES), lambda i: (0, 0))]
        args += list(router)
        out_specs = [pl.BlockSpec((ROW_TILE, d // 2), lambda i: (i, 0)),
                     pl.BlockSpec((ROW_TILE, LANES), lambda i: (i, 0)),
                     pl.BlockSpec((1, LANES), lambda i: (0, 0))]
        out_shape = [jax.ShapeDtypeStruct((t, d // 2), jnp.uint32),
                     jax.ShapeDtypeStruct((t, LANES), F32),
                     jax.ShapeDtypeStruct((1, LANES), F32)]
        scratch = [pltpu.VMEM((1, LANES), F32)]
        kern = _norm_mod_router_kernel
    return pl.pallas_call(
        kern,
        grid=(t // ROW_TILE,),
        in_specs=in_specs,
        out_specs=out_specs,
        out_shape=out_shape,
        scratch_shapes=scratch,
        compiler_params=_cparams(("arbitrary",)),
        name="norm_mod",
    )(*args)


def _rope_store(acc, cos_ref, sin_ref, o_ref):
    lane = lax.broadcasted_iota(jnp.int32, (1, LANES), 1)
    first = (lane % 64) < 32
    cos = cos_ref[...]
    sin = sin_ref[...]
    for c in range(acc.shape[1] // LANES):
        xc = acc[:, c * LANES:(c + 1) * LANES]
        sw = jnp.where(first, pltpu.roll(xc, 96, 1), pltpu.roll(xc, 32, 1))
        o_ref[:, c * LANES:(c + 1) * LANES] = (xc * cos + sw * sin).astype(o_ref.dtype)


def _mm_kernel(x_ref, w_ref, o_ref, wb_ref, *, n_q, q_scale):
    @pl.when(pl.program_id(1) == 0)
    def _():
        wb_ref[...] = w_ref[...].astype(BF16)

    acc = _dot(x_ref[...], wb_ref[...])
    j = pl.program_id(0)

    @pl.when(j < n_q)
    def _():
        o_ref[...] = (acc * q_scale).astype(o_ref.dtype)

    @pl.when(j >= n_q)
    def _():
        o_ref[...] = acc.astype(o_ref.dtype)


def _mm_rope_kernel(x_ref, w_ref, cos_ref, sin_ref, o_ref, wb_ref, *, n_q, q_scale, n_rope):
    @pl.when(pl.program_id(1) == 0)
    def _():
        wb_ref[...] = w_ref[...].astype(BF16)

    acc = _dot(x_ref[...], wb_ref[...])
    j = pl.program_id(0)

    @pl.when(j < n_q)
    def _():
        _rope_store(acc * q_scale, cos_ref, sin_ref, o_ref)

    @pl.when((j >= n_q) & (j < n_rope))
    def _():
        _rope_store(acc, cos_ref, sin_ref, o_ref)

    @pl.when(j >= n_rope)
    def _():
        o_ref[...] = acc.astype(o_ref.dtype)


def _mm_res_kernel(x_ref, w_ref, xs_ref, ga_ref, gb_ref, o_ref, wb_ref):
    @pl.when(pl.program_id(1) == 0)
    def _():
        wb_ref[...] = w_ref[...].astype(BF16)

    acc = _dot(x_ref[...], wb_ref[...])
    half = ROW_TILE
    o_ref[:half, :] = xs_ref[:half, :] + ga_ref[...] * acc[:half, :]
    o_ref[half:, :] = xs_ref[half:, :] + gb_ref[...] * acc[half:, :]


def _matmul(x, w, q_cols=0, q_scale=1.0, rope=None, n_rope_cols=0):
    t, k = x.shape
    n = w.shape[1]
    in_specs = [
        pl.BlockSpec((MM_TM, k), lambda j, i: (i, 0)),
        pl.BlockSpec((k, MM_TN), lambda j, i: (0, j)),
    ]
    args = [x, w]
    kern = functools.partial(_mm_kernel, n_q=q_cols // MM_TN, q_scale=q_scale)
    if rope is not None:
        in_specs += [pl.BlockSpec((MM_TM, LANES), lambda j, i: (i, 0))] * 2
        args += list(rope)
        kern = functools.partial(_mm_rope_kernel, n_q=q_cols // MM_TN, q_scale=q_scale,
                                 n_rope=n_rope_cols // MM_TN)
    return pl.pallas_call(
        kern,
        grid=(n // MM_TN, t // MM_TM),
        in_specs=in_specs,
        out_specs=pl.BlockSpec((MM_TM, MM_TN), lambda j, i: (i, j)),
        out_shape=jax.ShapeDtypeStruct((t, n), BF16),
        scratch_shapes=[pltpu.VMEM((k, MM_TN), BF16)],
        compiler_params=_cparams(("arbitrary", "arbitrary")),
        name="matmul",
    )(*args)


def _matmul_residual(x, w, xs, mods3, layer, gate_k, n_batch, tpb):
    t, k = x.shape
    n = w.shape[1]
    ncol = n // MM_TN
    nb = n_batch

    def gate_spec(half):
        return pl.BlockSpec(
            (None, 1, MM_TN),
            lambda j, i: (layer * 8 + _row_sel(2 * i + half, tpb, nb), 0, gate_k * ncol + j))

    return pl.pallas_call(
        _mm_res_kernel,
        grid=(ncol, t // MM_TM),
        in_specs=[
            pl.BlockSpec((MM_TM, k), lambda j, i: (i, 0)),
            pl.BlockSpec((k, MM_TN), lambda j, i: (0, j)),
            pl.BlockSpec((MM_TM, MM_TN), lambda j, i: (i, j)),
            gate_spec(0),
            gate_spec(1),
        ],
        out_specs=pl.BlockSpec((MM_TM, MM_TN), lambda j, i: (i, j)),
        out_shape=jax.ShapeDtypeStruct((t, n), F32),
        scratch_shapes=[pltpu.VMEM((k, MM_TN), BF16)],
        input_output_aliases={2: 0},
        compiler_params=_cparams(("arbitrary", "arbitrary")),
        name="matmul_residual",
    )(x, w, xs, mods3, mods3)


def _na_kernel(q_ref, k_ref, v_ref, tab_ref, o_ref, sl_ref, sc_ref, pl_ref, pc_ref, li_ref, *, rows):
    nq = NA_GROUP * GRID_W
    nk = NA_KROWS * GRID_W
    half_win = NA_WIN_ROWS // 2
    left = lax.broadcasted_iota(jnp.int32, (1, LANES), 1) < GRID_W

    s = _dot_nt(q_ref[0:CTX_LEN, :], k_ref[0:CTX_LEN, :])
    p = jnp.exp2(s - jnp.max(s, axis=-1, keepdims=True))
    l = jnp.sum(p, axis=-1, keepdims=True)
    o_ref[0:CTX_LEN, :] = (_dot(p.astype(BF16), v_ref[0:CTX_LEN, :]) / l).astype(o_ref.dtype)

    def geometry(g):
        r0 = jnp.asarray(g, jnp.int32) * NA_GROUP
        start = jnp.clip(r0 - half_win, 0, rows - NA_KROWS)
        q0 = pl.multiple_of(CTX_LEN + r0 * GRID_W, nq)
        k0 = pl.multiple_of(CTX_LEN + start * GRID_W, GRID_W)
        return r0, start, q0, k0

    def stage_a(g, par):
        _, _, q0, k0 = geometry(g)
        q = q_ref[pl.ds(q0, nq), :]
        sl_ref[par] = _dot_nt(q, k_ref[pl.ds(k0, nk), :])
        sc_ref[par] = _dot_nt(q, k_ref[0:CTX_LEN, :])

    def stage_b(g, par):
        r0, start, _, _ = geometry(g)
        for i in range(NA_GROUP):
            r = r0 + i
            rs = jnp.clip(r - half_win, 0, rows - NA_WIN_ROWS)
            rsl = slice(i * GRID_W, (i + 1) * GRID_W)
            tiles = []
            for jp in range(NA_KROWS // 2):
                kr = start + 2 * jp
                ok0 = ((kr >= rs) & (kr < rs + NA_WIN_ROWS)).astype(jnp.int32)
                ok1 = ((kr + 1 >= rs) & (kr + 1 < rs + NA_WIN_ROWS)).astype(jnp.int32)
                e = jnp.clip(kr - r + (NA_WIN_ROWS - 1) + 2, 0, 2 * NA_WIN_ROWS + 1)
                ok = jnp.where(left, ok0, ok1) > 0
                tiles.append(jnp.where(ok, tab_ref[e], NEG_BIG))
            sl = sl_ref[par, rsl, :] + jnp.concatenate(tiles, axis=1)
            sc = sc_ref[par, rsl, :]
            m = jnp.maximum(jnp.max(sl, axis=-1, keepdims=True), jnp.max(sc, axis=-1, keepdims=True))
            pl_i = jnp.exp2(sl - m)
            pc_i = jnp.exp2(sc - m)
            l_i = jnp.sum(pl_i, axis=-1, keepdims=True) + jnp.sum(pc_i, axis=-1, keepdims=True)
            pl_ref[par, rsl, :] = pl_i.astype(BF16)
            pc_ref[par, rsl, :] = pc_i.astype(BF16)
            li_ref[par, rsl, :] = jnp.broadcast_to(1.0 / l_i, (GRID_W, LANES))

    def stage_c(g, par):
        _, _, q0, k0 = geometry(g)
        o = _dot(pl_ref[par], v_ref[pl.ds(k0, nk), :]) + _dot(pc_ref[par], v_ref[0:CTX_LEN, :])
        o_ref[pl.ds(q0, nq), :] = (o * li_ref[par]).astype(o_ref.dtype)

    n_groups = rows // NA_GROUP
    stage_a(0, 0)
    stage_a(1, 1)
    stage_b(0, 0)

    def pair(u, carry):
        t = 2 + 2 * u
        stage_a(t, 0)
        stage_b(t - 1, 1)
        stage_c(t - 2, 0)
        stage_a(t + 1, 1)
        stage_b(t, 0)
        stage_c(t - 1, 1)
        return carry

    lax.fori_loop(0, (n_groups - 2) // 2, pair, 0)
    stage_b(n_groups - 1, 1)
    stage_c(n_groups - 2, 0)
    stage_c(n_groups - 1, 1)


def _na_bias_table(rpb):
    h = rpb.shape[0]
    qc = np.arange(GRID_W)[:, None]
    kc = np.arange(GRID_W)[None, :]
    cs = np.clip(qc - NA_WIN_COLS // 2, 0, GRID_W - NA_WIN_COLS)
    col_ok = (kc >= cs) & (kc < cs + NA_WIN_COLS)
    dc = np.clip(kc - qc + (NA_WIN_COLS - 1), 0, 2 * NA_WIN_COLS - 2)
    a = jnp.take(rpb.astype(F32), jnp.asarray(dc.reshape(-1)), axis=2)
    a = a.reshape(h, 2 * NA_WIN_ROWS - 1, GRID_W, GRID_W) * LOG2E
    a = jnp.where(jnp.asarray(col_ok)[None, None], a, NEG_BIG)
    a = jnp.pad(a, ((0, 0), (2, 2), (0, 0), (0, 0)), constant_values=NEG_BIG)
    return jnp.concatenate([a[:, :-1], a[:, 1:]], axis=-1)


def _na_attention(qkv, rpb, n_batch, tb):
    t = qkv.shape[0]
    rows = (tb - CTX_LEN) // GRID_W
    assert rows % (2 * NA_GROUP) == 0 and rows >= NA_KROWS
    nq, nk = NA_GROUP * GRID_W, NA_KROWS * GRID_W
    table = _na_bias_table(rpb)
    hh = NA_HEADS
    return pl.pallas_call(
        functools.partial(_na_kernel, rows=rows),
        grid=(hh, n_batch),
        in_specs=[
            pl.BlockSpec((tb, LANES), lambda h, b: (b, h)),
            pl.BlockSpec((tb, LANES), lambda h, b: (b, hh + h)),
            pl.BlockSpec((tb, LANES), lambda h, b: (b, 2 * hh + h)),
            pl.BlockSpec((None, 2 * NA_WIN_ROWS + 2, GRID_W, LANES), lambda h, b: (h, 0, 0, 0)),
        ],
        out_specs=pl.BlockSpec((tb, LANES), lambda h, b: (b, h)),
        out_shape=jax.ShapeDtypeStruct((t, D_MODEL), BF16),
        scratch_shapes=[
            pltpu.VMEM((2, nq, nk), F32),
            pltpu.VMEM((2, nq, CTX_LEN), F32),
            pltpu.VMEM((2, nq, nk), BF16),
            pltpu.VMEM((2, nq, CTX_LEN), BF16),
            pltpu.VMEM((2, nq, LANES), F32),
        ],
        compiler_params=_cparams(("arbitrary", "arbitrary")),
        name="na_attention",
    )(qkv, qkv, qkv, table)


def _da_kernel(lam_ref, g_ref, q_ref, k_ref, v_ref, o_ref, acc_ref, m_ref, l_ref,
               s0_ref, s1_ref, p0_ref, p1_ref, a0_ref, a1_ref, *, lam_init, n_kv_chunks):
    d = DIFF_HEAD_DIM
    i = pl.program_id(2)
    lam = lam_ref[...]
    lam_full = (jnp.exp(jnp.sum(lam[0:1] * lam[1:2], axis=-1, keepdims=True))
                - jnp.exp(jnp.sum(lam[2:3] * lam[3:4], axis=-1, keepdims=True)) + lam_init)
    s_bufs, p_bufs, a_bufs = (s0_ref, s1_ref), (p0_ref, p1_ref), (a0_ref, a1_ref)

    def tile(q0, nq, n_lat_chunks):
        acc_ref[:, 0:nq, :] = jnp.zeros((2, nq, 2 * d), F32)
        m_ref[:, 0:nq, :] = jnp.full((2, nq, LANES), NEG_BIG, F32)
        l_ref[:, 0:nq, :] = jnp.zeros((2, nq, LANES), F32)

        def key_rows(e):
            if isinstance(e, int):
                return (0, CTX_LEN) if e == 0 else (CTX_LEN + (e - 1) * DA_TK, DA_TK)
            return pl.multiple_of(CTX_LEN + (e - 1) * DA_TK, ROW_TILE), DA_TK

        def stage_a(e, par):
            k0, nk = key_rows(e)
            q = q_ref[pl.ds(q0, nq), :]
            k = k_ref[pl.ds(k0, nk), :]
            for c in range(2):
                s_bufs[par][c, 0:nq, 0:nk] = _dot_nt(q[:, c * d:(c + 1) * d], k[:, c * d:(c + 1) * d])

        def stage_b(e, par):
            _, nk = key_rows(e)
            s_ref, p_ref, a_ref = s_bufs[par], p_bufs[par], a_bufs[par]
            for c in range(2):
                for sb in range(nq // DA_SUB):
                    rows = slice(sb * DA_SUB, (sb + 1) * DA_SUB)
                    blocks = [s_ref[c, rows, j * LANES:(j + 1) * LANES] for j in range(nk // LANES)]
                    m_prev = m_ref[c, rows, :]
                    m_new = jnp.maximum(
                        m_prev, jnp.max(functools.reduce(jnp.maximum, blocks), axis=-1, keepdims=True))
                    alpha = jnp.exp2(m_prev - m_new)
                    p = [jnp.exp2(blk - m_new) for blk in blocks]
                    l_ref[c, rows, :] = alpha * l_ref[c, rows, :] + functools.reduce(jnp.add, p)
                    for j, pj in enumerate(p):
                        p_ref[c, rows, j * LANES:(j + 1) * LANES] = pj.astype(BF16)
                    a_ref[c, rows, :] = alpha
                    m_ref[c, rows, :] = m_new

        def stage_c(e, par):
            k0, nk = key_rows(e)
            v = v_ref[pl.ds(k0, nk), :]
            for c in range(2):
                pv = _dot(p_bufs[par][c, 0:nq, 0:nk], v)
                alpha = a_bufs[par][c, 0:nq, :]
                acc_ref[c, 0:nq, 0:LANES] = alpha * acc_ref[c, 0:nq, 0:LANES] + pv[:, :LANES]
                acc_ref[c, 0:nq, LANES:] = alpha * acc_ref[c, 0:nq, LANES:] + pv[:, LANES:]

        n = n_lat_chunks
        if n == 0:
            stage_a(0, 0)
            stage_b(0, 0)
            stage_c(0, 0)
        else:
            assert n % 2 == 0
            stage_a(0, 0)
            stage_a(1, 1)
            stage_b(0, 0)
            stage_a(2, 0)
            stage_b(1, 1)
            stage_c(0, 0)

            def pair(u, carry):
                t = 3 + 2 * u
                stage_a(t, 1)
                stage_b(t - 1, 0)
                stage_c(t - 2, 1)
                stage_a(t + 1, 0)
                stage_b(t, 1)
                stage_c(t - 1, 0)
                return carry

            lax.fori_loop(0, (n - 2) // 2, pair, 0)
            stage_b(n, 0)
            stage_c(n - 1, 1)
            stage_c(n, 0)

        r0 = 1.0 / jnp.sum(l_ref[0, 0:nq, :], axis=-1, keepdims=True)
        r1 = lam_full / jnp.sum(l_ref[1, 0:nq, :], axis=-1, keepdims=True)
        o_a = acc_ref[0, 0:nq, 0:LANES] * r0 - acc_ref[1, 0:nq, 0:LANES] * r1
        o_b = acc_ref[0, 0:nq, LANES:] * r0 - acc_ref[1, 0:nq, LANES:] * r1
        ms = (jnp.sum(o_a * o_a, axis=-1, keepdims=True) + jnp.sum(o_b * o_b, axis=-1, keepdims=True)) / (2 * d)
        inv = lax.rsqrt(ms + NORM_EPS) * (1.0 - lam_init)
        g = g_ref[...]
        o_ref[pl.ds(q0, nq), 0:LANES] = (o_a * inv * g[:, :LANES]).astype(o_ref.dtype)
        o_ref[pl.ds(q0, nq), LANES:] = (o_b * inv * g[:, LANES:]).astype(o_ref.dtype)

    @pl.when(i == 0)
    def _():
        tile(0, CTX_LEN, 0)

    @pl.when(i > 0)
    def _():
        tile(pl.multiple_of(CTX_LEN + (i - 1) * DA_TQ, ROW_TILE), DA_TQ, n_kv_chunks)


def _diff_attention(qkv, lam, subln_g, layer_idx, n_batch, tb):
    t = qkv.shape[0]
    hh = DIFF_HEADS
    w = 2 * DIFF_HEAD_DIM
    n_lat = (tb - CTX_LEN) // DA_TQ
    lam_init = 0.8 - 0.6 * math.exp(-0.3 * layer_idx)
    return pl.pallas_call(
        functools.partial(_da_kernel, lam_init=lam_init, n_kv_chunks=(tb - CTX_LEN) // DA_TK),
        grid=(n_batch, hh, 1 + n_lat),
        in_specs=[
            pl.BlockSpec((4, DIFF_HEAD_DIM), lambda b, h, i: (0, 0)),
            pl.BlockSpec((1, w), lambda b, h, i: (0, 0)),
            pl.BlockSpec((tb, w), lambda b, h, i: (b, h)),
            pl.BlockSpec((tb, w), lambda b, h, i: (b, hh + h)),
            pl.BlockSpec((tb, w), lambda b, h, i: (b, 2 * hh + h)),
        ],
        out_specs=pl.BlockSpec((tb, w), lambda b, h, i: (b, h)),
        out_shape=jax.ShapeDtypeStruct((t, D_MODEL), BF16),
        scratch_shapes=[
            pltpu.VMEM((2, DA_TQ, w), F32),
            pltpu.VMEM((2, DA_TQ, LANES), F32),
            pltpu.VMEM((2, DA_TQ, LANES), F32),
            pltpu.VMEM((2, DA_TQ, DA_TK), F32),
            pltpu.VMEM((2, DA_TQ, DA_TK), F32),
            pltpu.VMEM((2, DA_TQ, DA_TK), BF16),
            pltpu.VMEM((2, DA_TQ, DA_TK), BF16),
            pltpu.VMEM((2, DA_TQ, LANES), F32),
            pltpu.VMEM((2, DA_TQ, LANES), F32),
        ],
        compiler_params=_cparams(("arbitrary", "arbitrary", "arbitrary")),
        name="diff_attention",
    )(lam.astype(F32), subln_g.astype(F32).reshape(1, w), qkv, qkv, qkv)


def _hg_kernel(q_ref, i_ref, ff_ref, fb_ref, lb_ref, o_ref, stf_ref, stb_ref, *, n_groups):
    c = HGRN_CHUNK
    rg = HG_GROUP * c
    dk = HGRN_DK
    o_ref[...] = jnp.zeros(o_ref.shape, F32)
    stf_ref[...] = jnp.zeros(stf_ref.shape, F32)
    stb_ref[...] = jnp.zeros(stb_ref.shape, F32)
    row = lax.broadcasted_iota(jnp.int32, (rg, rg), 0)
    col = lax.broadcasted_iota(jnp.int32, (rg, rg), 1)
    same = (row // c) == (col // c)
    lower = same & (row >= col)
    upper = same & (col >= row)
    lower_b = lower.astype(BF16)
    upper_b = upper.astype(BF16)
    chunk_of_row = lax.broadcasted_iota(jnp.int32, (rg, 1), 0) // c
    lb_all = lb_ref[...]

    def per_chunk_rows(b, r):
        return jnp.concatenate(
            [jnp.broadcast_to(b[g * c + r:g * c + r + 1, :], (c, dk)) for g in range(HG_GROUP)], axis=0)

    def group(gidx, f_ref, lb, mask, mask_b, st_ref, end_row, order):
        r0 = pl.multiple_of(gidx * rg, rg)
        qr = q_ref[pl.ds(r0, rg), :].astype(F32)
        q = qr * jax.nn.sigmoid(qr) * (dk ** -0.5)
        v = i_ref[pl.ds(r0, rg), :]
        fr = f_ref[pl.ds(r0, rg), :].astype(F32)
        fg = lb + (1.0 - lb) * jax.nn.sigmoid(fr)
        k = 1.0 - fg
        lf = jnp.log(fg)
        hi = lf.astype(BF16)
        rem = lf - hi.astype(F32)
        mid = rem.astype(BF16)
        lo = (rem - mid.astype(F32)).astype(BF16)
        cs = _dot(mask_b, jnp.concatenate([hi, mid, lo], axis=1))
        bcum = cs[:, 0:dk] + cs[:, dk:2 * dk] + cs[:, 2 * dk:]
        b_end = per_chunk_rows(bcum, end_row)
        b_mid = per_chunk_rows(bcum, c // 2)
        qt = (q * jnp.exp(bcum - b_mid)).astype(BF16)
        kt = (k * jnp.exp(b_mid - bcum)).astype(BF16)
        a = jnp.where(mask, _dot_nt(qt, kt), 0.0).astype(BF16)
        o_intra = _dot(a, v)
        k_end = k * jnp.exp(b_end - bcum)
        k_exp = jnp.concatenate(
            [jnp.where(chunk_of_row == g, k_end, 0.0) for g in range(HG_GROUP)], axis=1).astype(BF16)
        u = _dot_tn(v, k_exp)
        st = st_ref[...]
        st_in = [None] * HG_GROUP
        for g in order:
            st_in[g] = st.astype(BF16)
            st = st * jnp.exp(bcum[g * c + end_row:g * c + end_row + 1, :]) + u[:, g * dk:(g + 1) * dk]
        st_ref[...] = st
        oi = _dot_nt((q * jnp.exp(bcum)).astype(BF16), jnp.concatenate(st_in, axis=0))
        o_inter = jnp.concatenate(
            [oi[g * c:(g + 1) * c, g * dk:(g + 1) * dk] for g in range(HG_GROUP)], axis=0)
        o_ref[pl.ds(r0, rg), :] += o_intra + o_inter

    fwd_order = list(range(HG_GROUP))

    def body(n, carry):
        group(n, ff_ref, lb_all[0:1, :], lower, lower_b, stf_ref, c - 1, fwd_order)
        nb = jnp.where(n == 0, 0, n_groups - n)
        group(nb, fb_ref, lb_all[1:2, :], upper, upper_b, stb_ref, 0, fwd_order[::-1])
        return carry

    lax.fori_loop(0, n_groups, body, 0)


def _hgrn_scan(proj, lb2, n_batch, tb):
    t = proj.shape[0]
    hh = HGRN_HEADS
    rg = HG_GROUP * HGRN_CHUNK
    assert CTX_LEN == rg and tb % rg == 0

    def col(k):
        return pl.BlockSpec((tb, LANES), lambda b, h: (b, k * hh + h))

    return pl.pallas_call(
        functools.partial(_hg_kernel, n_groups=tb // rg),
        grid=(n_batch, hh),
        in_specs=[col(0), col(1), col(3), col(4), pl.BlockSpec((2, LANES), lambda b, h: (0, h))],
        out_specs=pl.BlockSpec((tb, LANES), lambda b, h: (b, h)),
        out_shape=jax.ShapeDtypeStruct((t, D_MODEL), F32),
        scratch_shapes=[pltpu.VMEM((LANES, LANES), F32), pltpu.VMEM((LANES, LANES), F32)],
        compiler_params=_cparams(("arbitrary", "arbitrary")),
        name="hgrn_scan",
    )(proj, proj, proj, proj, lb2)


def _hg_finish_kernel(o_ref, g_ref, ng_ref, h_ref):
    o = o_ref[...]
    y = o * lax.rsqrt(jnp.mean(o * o, axis=-1, keepdims=True) + NORM_EPS) * ng_ref[...]
    g = g_ref[...].astype(F32)
    h_ref[...] = (y * (g * jax.nn.sigmoid(g))).astype(h_ref.dtype)


def _hgrn_finish(o, proj, norm_g):
    t, d = o.shape
    return pl.pallas_call(
        _hg_finish_kernel,
        grid=(t // ROW_TILE,),
        in_specs=[
            pl.BlockSpec((ROW_TILE, d), lambda i: (i, 0)),
            pl.BlockSpec((ROW_TILE, d), lambda i: (i, 2)),
            pl.BlockSpec((1, d), lambda i: (0, 0)),
        ],
        out_specs=pl.BlockSpec((ROW_TILE, d), lambda i: (i, 0)),
        out_shape=jax.ShapeDtypeStruct((t, d), BF16),
        compiler_params=_cparams(("arbitrary",)),
        name="hgrn_finish",
    )(o, proj, norm_g.astype(F32).reshape(1, d))


def _moe_kernel(be_ref, first_ref, nxt_ref, nu_ref, tok_ref, x_hbm, wg_hbm, wu_hbm, wd_hbm, o_ref,
                xbuf, sg_ref, su_ref, sd_ref, bg_ref, bu_ref, bd_ref, sem, xsem, *, layer):
    b = pl.program_id(0)
    n_used = nu_ref[0]

    def weight_copies(e):
        return (pltpu.make_async_copy(wg_hbm.at[layer, e], sg_ref, sem.at[0]),
                pltpu.make_async_copy(wu_hbm.at[layer, e], su_ref, sem.at[1]),
                pltpu.make_async_copy(wd_hbm.at[layer, e], sd_ref, sem.at[2]))

    def start_gather(blk, slot):
        base = blk * MOE_TM
        for r in range(MOE_TM):
            pltpu.make_async_copy(x_hbm.at[pl.ds(tok_ref[base + r], 1)], xbuf.at[slot, pl.ds(r, 1)],
                                  xsem.at[slot]).start()

    def wait_gather(slot):
        pltpu.make_async_copy(x_hbm.at[pl.ds(0, MOE_TM)], xbuf.at[slot], xsem.at[slot]).wait()

    @pl.when(b == 0)
    def _():
        for cp in weight_copies(be_ref[0]):
            cp.start()
        start_gather(0, 0)
        start_gather(1, 1)

    @pl.when(first_ref[b] == 1)
    def _():
        for cp in weight_copies(be_ref[b]):
            cp.wait()
        for src, dst in ((sg_ref, bg_ref), (su_ref, bu_ref), (sd_ref, bd_ref)):
            slab = src.shape[0] // MOE_CAST_SLABS

            def cast(r, carry, src=src, dst=dst, slab=slab):
                r0 = pl.multiple_of(r * slab, slab)
                dst[pl.ds(r0, slab), :] = src[pl.ds(r0, slab), :].astype(BF16)
                return carry

            lax.fori_loop(0, MOE_CAST_SLABS, cast, 0)

        @pl.when(nxt_ref[b] >= 0)
        def _():
            for cp in weight_copies(nxt_ref[b]):
                cp.start()

    @pl.when(b < n_used)
    def _():
        slot = b % MOE_XBUFS
        wait_gather(slot)
        xu = xbuf[slot]
        x = jnp.concatenate(
            [lax.bitcast_convert_type(xu << 16, F32).astype(BF16),
             lax.bitcast_convert_type(xu & jnp.uint32(0xFFFF0000), F32).astype(BF16)], axis=1)
        g = _dot(x, bg_ref[...])
        u = _dot(x, bu_ref[...])
        h = (g * jax.nn.sigmoid(g) * u).astype(BF16)
        o_ref[...] = _dot(h, bd_ref[...]).astype(o_ref.dtype)
        start_gather(b + 2, (b + 2) % MOE_XBUFS)

    @pl.when(b >= n_used)
    def _():
        @pl.when(b == n_used)
        def _():
            wait_gather(b % MOE_XBUFS)
            wait_gather((b + 1) % MOE_XBUFS)

        o_ref[...] = jnp.zeros(o_ref.shape, o_ref.dtype)


def _moe_experts(hp, slot_tok, blk_expert, blk_first, blk_next, n_used, w_gate, w_up, w_down, layer):
    d = 2 * hp.shape[1]
    f = w_gate.shape[3]
    n_slots = slot_tok.shape[0]
    n_blocks = n_slots // MOE_TM
    grid_spec = pltpu.PrefetchScalarGridSpec(
        num_scalar_prefetch=5,
        grid=(n_blocks,),
        in_specs=[pl.BlockSpec(memory_space=pl.ANY)] * 4,
        out_specs=pl.BlockSpec((MOE_TM, d), lambda b, *_: (b, 0)),
        scratch_shapes=[
            pltpu.VMEM((MOE_XBUFS, MOE_TM, d // 2), jnp.uint32),
            pltpu.VMEM((d, f), F32), pltpu.VMEM((d, f), F32), pltpu.VMEM((f, d), F32),
            pltpu.VMEM((d, f), BF16), pltpu.VMEM((d, f), BF16), pltpu.VMEM((f, d), BF16),
            pltpu.SemaphoreType.DMA((3,)),
            pltpu.SemaphoreType.DMA((MOE_XBUFS,)),
        ],
    )
    return pl.pallas_call(
        functools.partial(_moe_kernel, layer=layer),
        grid_spec=grid_spec,
        out_shape=jax.ShapeDtypeStruct((n_slots, d), BF16),
        compiler_params=_cparams(("arbitrary",)),
        name="moe_experts",
    )(blk_expert, blk_first, blk_next, n_used, slot_tok, hp, w_gate, w_up, w_down)


def _route(route, counts_row, n_blocks):
    t = route.shape[0]
    e_idx = route[:, 0:TOP_K].astype(jnp.int32)
    rank = route[:, TOP_K:2 * TOP_K].astype(jnp.int32)
    w = route[:, 2 * TOP_K:3 * TOP_K]
    counts = counts_row[0, :N_EXPERTS].astype(jnp.int32)
    pcounts = (counts + MOE_TM - 1) // MOE_TM * MOE_TM
    pends = jnp.cumsum(pcounts)
    pstarts = pends - pcounts
    e_ids = jnp.arange(N_EXPERTS, dtype=jnp.int32)
    start_of = jnp.sum(jnp.where(e_idx[..., None] == e_ids, pstarts, 0), axis=-1)
    dest = (start_of + rank).reshape(-1).astype(jnp.int32)
    flat_t = jnp.arange(t * TOP_K, dtype=jnp.int32) // TOP_K
    slot_tok = jnp.zeros((n_blocks * MOE_TM,), jnp.int32).at[dest].set(flat_t)
    blk_expert = jnp.minimum(
        jnp.searchsorted(pends, jnp.arange(n_blocks, dtype=jnp.int32) * MOE_TM, side='right'),
        N_EXPERTS - 1).astype(jnp.int32)
    n_used = (pends[-1] // MOE_TM).astype(jnp.int32).reshape(1)
    blk_ids = jnp.arange(n_blocks, dtype=jnp.int32)
    prev_expert = jnp.concatenate([jnp.full((1,), -1, jnp.int32), blk_expert[:-1]])
    blk_first = ((blk_ids < n_used[0]) & (blk_expert != prev_expert)).astype(jnp.int32)
    e_ids = jnp.arange(N_EXPERTS, dtype=jnp.int32)
    later_used = (e_ids[None, :] > e_ids[:, None]) & (counts[None, :] > 0)
    next_used = jnp.min(jnp.where(later_used, e_ids[None, :], N_EXPERTS), axis=1)
    next_used = jnp.where(next_used == N_EXPERTS, -1, next_used).astype(jnp.int32)
    blk_next = next_used[blk_expert]
    return w, dest.reshape(t, TOP_K), slot_tok, (blk_expert, blk_first, blk_next, n_used)


def _moe_res_kernel(xs_ref, y0_ref, y1_ref, w_ref, gate_ref, o_ref):
    w = w_ref[...]
    y = w[:, 0:1] * y0_ref[...].astype(F32) + w[:, 1:2] * y1_ref[...].astype(F32)
    o_ref[...] = xs_ref[...] + gate_ref[...] * y


def _moe_res_final_kernel(xs_ref, y0_ref, y1_ref, w_ref, gate_ref, fg_ref, o_ref):
    w = w_ref[...]
    y = w[:, 0:1] * y0_ref[...].astype(F32) + w[:, 1:2] * y1_ref[...].astype(F32)
    x = xs_ref[...] + gate_ref[...] * y
    o_ref[...] = x * lax.rsqrt(jnp.mean(x * x, axis=-1, keepdims=True) + NORM_EPS) * fg_ref[...]


def _moe_residual(xs, y0, y1, w, mods3, layer, n_batch, tpb):
    t, d = xs.shape
    nb = n_batch
    row = lambda i: (i, 0)
    return pl.pallas_call(
        _moe_res_kernel,
        grid=(t // ROW_TILE,),
        in_specs=[
            pl.BlockSpec((ROW_TILE, d), row),
            pl.BlockSpec((ROW_TILE, d), row),
            pl.BlockSpec((ROW_TILE, d), row),
            pl.BlockSpec((ROW_TILE, TOP_K), row),
            pl.BlockSpec((None, 1, d), lambda i: (layer * 8 + _row_sel(i, tpb, nb), 0, 5)),
        ],
        out_specs=pl.BlockSpec((ROW_TILE, d), row),
        out_shape=jax.ShapeDtypeStruct((t, d), F32),
        input_output_aliases={0: 0},
        compiler_params=_cparams(("arbitrary",)),
        name="moe_residual",
    )(xs, y0, y1, w, mods3)


def _moe_residual_final(xs, y0, y1, w, mods3, layer, final_g, n_batch, tpb):
    t, d = xs.shape
    lat = tpb - CTX_LEN // ROW_TILE
    src = lambda i: ((i // lat) * tpb + CTX_LEN // ROW_TILE + i % lat, 0)
    return pl.pallas_call(
        _moe_res_final_kernel,
        grid=(n_batch * lat,),
        in_specs=[
            pl.BlockSpec((ROW_TILE, d), src),
            pl.BlockSpec((ROW_TILE, d), src),
            pl.BlockSpec((ROW_TILE, d), src),
            pl.BlockSpec((ROW_TILE, TOP_K), src),
            pl.BlockSpec((None, 1, d), lambda i: (layer * 8 + i // lat, 0, 5)),
            pl.BlockSpec((1, d), lambda i: (0, 0)),
        ],
        out_specs=pl.BlockSpec((ROW_TILE, d), lambda i: (i, 0)),
        out_shape=jax.ShapeDtypeStruct((n_batch * lat * ROW_TILE, d), F32),
        compiler_params=_cparams(("arbitrary",)),
        name="moe_residual_final",
    )(xs, y0, y1, w, mods3, final_g.astype(F32).reshape(1, d))


def _rope_tables(n_batch, seq):
    quarter = DIFF_HEAD_DIM // 4
    tpos = jnp.arange(seq)
    rowp = (tpos // GRID_W).astype(F32)
    colp = (tpos % GRID_W).astype(F32)
    inv = ROPE_THETA ** (-jnp.arange(quarter, dtype=F32) / quarter)
    ang_r = rowp[:, None] * inv[None, :]
    ang_c = colp[:, None] * inv[None, :]
    cos = jnp.concatenate([jnp.cos(ang_r)] * 2 + [jnp.cos(ang_c)] * 2, axis=-1)
    sin = jnp.concatenate([-jnp.sin(ang_r), jnp.sin(ang_r), -jnp.sin(ang_c), jnp.sin(ang_c)], axis=-1)
    cos = jnp.concatenate([jnp.ones((CTX_LEN, LANES), F32), cos], axis=0)
    sin = jnp.concatenate([jnp.zeros((CTX_LEN, LANES), F32), sin], axis=0)
    return jnp.tile(cos, (n_batch, 1)), jnp.tile(sin, (n_batch, 1))


def kernel(x, c, ctx, c_ctx, ada_w, ada_b, norm_g, na_w_qkv, na_rpb, na_w_o, da_w_qkv, da_lambda, da_subln_g, da_w_o, hg_w_in, hg_lb, hg_norm_g, hg_w_o, moe_w_router, moe_b_router, moe_w_gate, moe_w_up, moe_w_down, final_g):
    n_batch, seq, d = x.shape
    tb = CTX_LEN + seq
    t = n_batch * tb
    tpb = tb // ROW_TILE
    assert d == D_MODEL and ctx.shape[1] == CTX_LEN and n_batch + 1 <= 8
    assert seq % DA_TQ == 0 and t % MM_TM == 0 and seq // GRID_W >= NA_WIN_ROWS

    xs = jnp.concatenate([ctx, x], axis=1).reshape(t, d).astype(F32)

    c_rows = jnp.zeros((8, d), F32).at[:n_batch].set(c.astype(F32)).at[n_batch].set(c_ctx.astype(F32))
    mods = _ada_mods(c_rows, ada_w.astype(F32), ada_b.astype(F32))
    mods3 = mods.reshape(DEPTH * 8, 1, 6 * d)
    g3 = norm_g.astype(F32).reshape(DEPTH * 2, 1, d)

    lb_p = jax.nn.softmax(hg_lb.astype(F32), axis=1)
    lb_all = jnp.cumsum(lb_p, axis=1) - lb_p[:, :1]

    wr = jnp.zeros((d, LANES), F32).at[:, :N_EXPERTS].set(moe_w_router.astype(F32))
    wr_hi = wr.astype(BF16)
    wr_lo = (wr - wr_hi.astype(F32)).astype(BF16)
    rb = jnp.zeros((1, LANES), F32).at[0, :N_EXPERTS].set(moe_b_router.astype(F32))

    n_blocks = -(-(t * TOP_K) // MOE_TM) + N_EXPERTS + (MOE_XBUFS - 1)
    rope = _rope_tables(n_batch, seq)
    wg = moe_w_gate.astype(F32)
    wu = moe_w_up.astype(F32)
    wd = moe_w_down.astype(F32)
    na_qs = NA_HEAD_DIM ** -0.5 * LOG2E
    da_qs = DIFF_HEAD_DIM ** -0.5 * LOG2E

    counters = [0, 0, 0]
    out = None
    for i in range(DEPTH):
        kind = i % N_MIXERS
        j = counters[kind]
        counters[kind] += 1

        h = _norm_mod(xs, mods3, g3, i, 0, n_batch, tpb)[0]
        if kind == 0:
            qkv = _matmul(h, na_w_qkv[j], q_cols=d, q_scale=na_qs)
            o = _na_attention(qkv, na_rpb[j], n_batch, tb)
            w_o = na_w_o[j]
        elif kind == 1:
            qkv = _matmul(h, da_w_qkv[j], q_cols=d, q_scale=da_qs, rope=rope, n_rope_cols=2 * d)
            o = _diff_attention(qkv, da_lambda[j], da_subln_g[j], i, n_batch, tb)
            w_o = da_w_o[j]
        else:
            proj = _matmul(h, hg_w_in[j])
            o32 = _hgrn_scan(proj, lb_all[:, i, :], n_batch, tb)
            o = _hgrn_finish(o32, proj, hg_norm_g[j])
            w_o = hg_w_o[j]
        xs = _matmul_residual(o, w_o, xs, mods3, i, 2, n_batch, tpb)

        hp, route, counts = _norm_mod(xs, mods3, g3, i, 1, n_batch, tpb, router=(wr_hi, wr_lo, rb))
        w, dest, slot_tok, blk_info = _route(route, counts, n_blocks)
        yb = _moe_experts(hp, slot_tok, *blk_info, wg, wu, wd, i)
        y0 = yb.at[dest[:, 0]].get(mode="promise_in_bounds")
        y1 = yb.at[dest[:, 1]].get(mode="promise_in_bounds")
        if i < DEPTH - 1:
            xs = _moe_residual(xs, y0, y1, w, mods3, i, n_batch, tpb)
        else:
            out = _moe_residual_final(xs, y0, y1, w, mods3, i, final_g, n_batch, tpb)
    return out.reshape(n_batch, seq, d).astype(x.dtype)
```

```python
import functools
import math

import numpy as np
import jax
import jax.numpy as jnp
from jax import lax
from jax.experimental import pallas as pl
from jax.experimental.pallas import tpu as pltpu

F32 = jnp.float32
BF16 = jnp.bfloat16

D_MODEL = 2048
DEPTH = 4
GRID_W = 64
CTX_LEN = 256
N_MIXERS = 3
NORM_EPS = 1e-6
ROPE_THETA = 10000.0

NA_HEADS = 16
NA_HEAD_DIM = 128
NA_WIN_ROWS = 8
NA_WIN_COLS = 16

DIFF_HEADS = 8
DIFF_HEAD_DIM = 128

HGRN_HEADS = 16
HGRN_DK = 128
HGRN_CHUNK = 64

N_EXPERTS = 16
N_GROUPS = 4
EXPERTS_PER_GROUP = 4
TOP_K = 2
D_EXPERT = 1024

LANES = 128
ROW_TILE = 256
MM_TM = 512
MM_TN = 1024
MOE_TM = 256
DA_TQ = 512
VMEM_LIMIT = 56 * 1024 * 1024

NA_GROUP = 4
NA_KROWS = 12
DA_SUB = 64
DA_TK = 512
HG_GROUP = 4
HG_HEADS = 2
MOE_XBUFS = 3
MOE_CAST_SLABS = 8

NEG_BIG = -1e30
LOG2E = 1.4426950408889634


def _cparams(sem):
    return pltpu.CompilerParams(dimension_semantics=sem, vmem_limit_bytes=VMEM_LIMIT)


def _dot(a, b):
    return jnp.dot(a, b, preferred_element_type=F32)


def _dot_nt(a, b):
    return lax.dot_general(a, b, (((1,), (1,)), ((), ())), preferred_element_type=F32)


def _dot_tn(a, b):
    return lax.dot_general(a, b, (((0,), (0,)), ((), ())), preferred_element_type=F32)


def _row_sel(sub, tiles_per_batch, n_batch):
    return jnp.where(sub % tiles_per_batch == 0, n_batch, sub // tiles_per_batch)


def _ada_kernel(c_ref, w_ref, b_ref, o_ref):
    c = c_ref[...]
    a = c * jax.nn.sigmoid(c)
    o_ref[...] = jnp.dot(a, w_ref[...], preferred_element_type=F32,
                         precision=lax.Precision.HIGHEST) + b_ref[...]


def _ada_mods(c_rows, ada_w, ada_b):
    depth, d, n = ada_w.shape
    tn = 1024
    return pl.pallas_call(
        _ada_kernel,
        grid=(depth, n // tn),
        in_specs=[
            pl.BlockSpec((8, d), lambda l, j: (0, 0)),
            pl.BlockSpec((None, d, tn), lambda l, j: (l, 0, j)),
            pl.BlockSpec((None, 1, tn), lambda l, j: (l, 0, j)),
        ],
        out_specs=pl.BlockSpec((None, 8, tn), lambda l, j: (l, 0, j)),
        out_shape=jax.ShapeDtypeStruct((depth, 8, n), F32),
        compiler_params=_cparams(("arbitrary", "arbitrary")),
        name="ada_mods",
    )(c_rows, ada_w, ada_b.reshape(depth, 1, n))


def _norm_mod_kernel(x_ref, g_ref, shift_ref, scale_ref, h_ref):
    x = x_ref[...]
    y = x * lax.rsqrt(jnp.mean(x * x, axis=-1, keepdims=True) + NORM_EPS)
    h = y * g_ref[...] * (1.0 + scale_ref[...]) + shift_ref[...]
    h_ref[...] = h.astype(h_ref.dtype)


def _route_tile(lg, bias, run_ref):
    rows = lg.shape[0]
    lane = lax.broadcasted_iota(jnp.int32, (1, LANES), 1)
    lanef = lane.astype(F32)
    neg = -jnp.inf
    far = float(LANES)
    rmax = lambda a: jnp.max(a, axis=-1, keepdims=True)
    rmin = lambda a: jnp.min(a, axis=-1, keepdims=True)
    rsum = lambda a: jnp.sum(a, axis=-1, keepdims=True)

    s = jax.nn.sigmoid(lg)
    sel = jnp.where(lane < N_EXPERTS, s + bias, neg)
    best = None
    for g in range(N_GROUPS):
        ing = (lane // EXPERTS_PER_GROUP) == g
        v = jnp.where(ing, sel, neg)
        m1 = rmax(v)
        i1 = rmin(jnp.where(v == m1, lanef, far))
        v2 = jnp.where(lanef == i1, neg, v)
        m2 = rmax(v2)
        i2 = rmin(jnp.where(v2 == m2, lanef, far))
        cand = (m1 + m2, i1, i2)
        if best is None:
            best = cand
        else:
            better = cand[0] > best[0]
            best = tuple(jnp.where(better, c, b) for c, b in zip(cand, best))
    _, e1, e2 = best
    is1 = lanef == e1
    is2 = lanef == e2
    s1 = rsum(jnp.where(is1, s, 0.0))
    s2 = rsum(jnp.where(is2, s, 0.0))
    w1 = s1 / (s1 + s2)
    w2 = s2 / (s1 + s2)
    onehot = jnp.where(is1 | is2, 1.0, 0.0)
    row = lax.broadcasted_iota(jnp.int32, (rows, rows), 0)
    col = lax.broadcasted_iota(jnp.int32, (rows, rows), 1)
    before = jnp.where(row > col, 1.0, 0.0).astype(BF16)
    prior = _dot(before, onehot.astype(BF16)) + run_ref[...]
    r1 = rsum(jnp.where(is1, prior, 0.0))
    r2 = rsum(jnp.where(is2, prior, 0.0))
    run_ref[...] += jnp.sum(onehot, axis=0, keepdims=True)
    out = jnp.zeros((rows, LANES), F32)
    for k, val in enumerate((e1, e2, r1, r2, w1, w2)):
        out = jnp.where(lane == k, val, out)
    return out


def _norm_mod_router_kernel(x_ref, g_ref, shift_ref, scale_ref, whi_ref, wlo_ref, rb_ref,
                            h_ref, route_ref, cnt_ref, run_ref):
    @pl.when(pl.program_id(0) == 0)
    def _():
        run_ref[...] = jnp.zeros(run_ref.shape, F32)

    x = x_ref[...]
    y = x * lax.rsqrt(jnp.mean(x * x, axis=-1, keepdims=True) + NORM_EPS)
    h = y * g_ref[...] * (1.0 + scale_ref[...]) + shift_ref[...]
    h_hi = h.astype(BF16)
    h_lo = (h - h_hi.astype(F32)).astype(BF16)
    bits = lax.bitcast_convert_type(h_hi.astype(F32), jnp.uint32)
    half = bits.shape[1] // 2
    h_ref[...] = (bits[:, :half] >> 16) | bits[:, half:]
    lg = _dot(h_hi, whi_ref[...]) + _dot(h_hi, wlo_ref[...]) + _dot(h_lo, whi_ref[...])
    route_ref[...] = _route_tile(lg, rb_ref[...], run_ref)
    cnt_ref[...] = run_ref[...]


def _norm_mod(xs, mods3, g3, layer, which, n_batch, tpb, router=None):
    t, d = xs.shape
    shift_k, scale_k = (0, 1) if which == 0 else (3, 4)
    nb = n_batch

    def mod_spec(k):
        return pl.BlockSpec((None, 1, d), lambda i: (layer * 8 + _row_sel(i, tpb, nb), 0, k))

    in_specs = [
        pl.BlockSpec((ROW_TILE, d), lambda i: (i, 0)),
        pl.BlockSpec((None, 1, d), lambda i: (layer * 2 + which, 0, 0)),
        mod_spec(shift_k),
        mod_spec(scale_k),
    ]
    args = [xs, g3, mods3, mods3]
    out_specs = [pl.BlockSpec((ROW_TILE, d), lambda i: (i, 0))]
    out_shape = [jax.ShapeDtypeStruct((t, d), BF16)]
    kern = _norm_mod_kernel
    scratch = []
    if router is not None:
        in_specs += [pl.BlockSpec((d, LANES), lambda i: (0, 0))] * 2 + [pl.BlockSpec((1, LANES), lambda i: (0, 0))]
        args += list(router)
        out_specs = [pl.BlockSpec((ROW_TILE, d // 2), lambda i: (i, 0)),
                     pl.BlockSpec((ROW_TILE, LANES), lambda i: (i, 0)),
                     pl.BlockSpec((1, LANES), lambda i: (0, 0))]
        out_shape = [jax.ShapeDtypeStruct((t, d // 2), jnp.uint32),
                     jax.ShapeDtypeStruct((t, LANES), F32),
                     jax.ShapeDtypeStruct((1, LANES), F32)]
        scratch = [pltpu.VMEM((1, LANES), F32)]
        kern = _norm_mod_router_kernel
    return pl.pallas_call(
        kern,
        grid=(t // ROW_TILE,),
        in_specs=in_specs,
        out_specs=out_specs,
        out_shape=out_shape,
        scratch_shapes=scratch,
        compiler_params=_cparams(("arbitrary",)),
        name="norm_mod",
    )(*args)


def _rope_store(acc, cos_ref, sin_ref, o_ref):
    lane = lax.broadcasted_iota(jnp.int32, (1, LANES), 1)
    first = (lane % 64) < 32
    cos = cos_ref[...]
    sin = sin_ref[...]
    for c in range(acc.shape[1] // LANES):
        xc = acc[:, c * LANES:(c + 1) * LANES]
        sw = jnp.where(first, pltpu.roll(xc, 96, 1), pltpu.roll(xc, 32, 1))
        o_ref[:, c * LANES:(c + 1) * LANES] = (xc * cos + sw * sin).astype(o_ref.dtype)


def _mm_kernel(x_ref, w_ref, o_ref, wb_ref, *, n_q, q_scale):
    @pl.when(pl.program_id(1) == 0)
    def _():
        wb_ref[...] = w_ref[...].astype(BF16)

    acc = _dot(x_ref[...], wb_ref[...])
    j = pl.program_id(0)

    @pl.when(j < n_q)
    def _():
        o_ref[...] = (acc * q_scale).astype(o_ref.dtype)

    @pl.when(j >= n_q)
    def _():
        o_ref[...] = acc.astype(o_ref.dtype)


def _mm_rope_kernel(x_ref, w_ref, cos_ref, sin_ref, o_ref, wb_ref, *, n_q, q_scale, n_rope):
    @pl.when(pl.program_id(1) == 0)
    def _():
        wb_ref[...] = w_ref[...].astype(BF16)

    acc = _dot(x_ref[...], wb_ref[...])
    j = pl.program_id(0)

    @pl.when(j < n_q)
    def _():
        _rope_store(acc * q_scale, cos_ref, sin_ref, o_ref)

    @pl.when((j >= n_q) & (j < n_rope))
    def _():
        _rope_store(acc, cos_ref, sin_ref, o_ref)

    @pl.when(j >= n_rope)
    def _():
        o_ref[...] = acc.astype(o_ref.dtype)


def _mm_res_kernel(x_ref, w_ref, xs_ref, ga_ref, gb_ref, o_ref, wb_ref):
    @pl.when(pl.program_id(1) == 0)
    def _():
        wb_ref[...] = w_ref[...].astype(BF16)

    acc = _dot(x_ref[...], wb_ref[...])
    half = ROW_TILE
    o_ref[:half, :] = xs_ref[:half, :] + ga_ref[...] * acc[:half, :]
    o_ref[half:, :] = xs_ref[half:, :] + gb_ref[...] * acc[half:, :]


def _matmul(x, w, q_cols=0, q_scale=1.0, rope=None, n_rope_cols=0):
    t, k = x.shape
    n = w.shape[1]
    in_specs = [
        pl.BlockSpec((MM_TM, k), lambda j, i: (i, 0)),
        pl.BlockSpec((k, MM_TN), lambda j, i: (0, j)),
    ]
    args = [x, w]
    kern = functools.partial(_mm_kernel, n_q=q_cols // MM_TN, q_scale=q_scale)
    if rope is not None:
        in_specs += [pl.BlockSpec((MM_TM, LANES), lambda j, i: (i, 0))] * 2
        args += list(rope)
        kern = functools.partial(_mm_rope_kernel, n_q=q_cols // MM_TN, q_scale=q_scale,
                                 n_rope=n_rope_cols // MM_TN)
    return pl.pallas_call(
        kern,
        grid=(n // MM_TN, t // MM_TM),
        in_specs=in_specs,
        out_specs=pl.BlockSpec((MM_TM, MM_TN), lambda j, i: (i, j)),
        out_shape=jax.ShapeDtypeStruct((t, n), BF16),
        scratch_shapes=[pltpu.VMEM((k, MM_TN), BF16)],
        compiler_params=_cparams(("arbitrary", "arbitrary")),
        name="matmul",
    )(*args)


def _matmul_residual(x, w, xs, mods3, layer, gate_k, n_batch, tpb):
    t, k = x.shape
    n = w.shape[1]
    ncol = n // MM_TN
    nb = n_batch

    def gate_spec(half):
        return pl.BlockSpec(
            (None, 1, MM_TN),
            lambda j, i: (layer * 8 + _row_sel(2 * i + half, tpb, nb), 0, gate_k * ncol + j))

    return pl.pallas_call(
        _mm_res_kernel,
        grid=(ncol, t // MM_TM),
        in_specs=[
            pl.BlockSpec((MM_TM, k), lambda j, i: (i, 0)),
            pl.BlockSpec((k, MM_TN), lambda j, i: (0, j)),
            pl.BlockSpec((MM_TM, MM_TN), lambda j, i: (i, j)),
            gate_spec(0),
            gate_spec(1),
        ],
        out_specs=pl.BlockSpec((MM_TM, MM_TN), lambda j, i: (i, j)),
        out_shape=jax.ShapeDtypeStruct((t, n), F32),
        scratch_shapes=[pltpu.VMEM((k, MM_TN), BF16)],
        input_output_aliases={2: 0},
        compiler_params=_cparams(("arbitrary", "arbitrary")),
        name="matmul_residual",
    )(x, w, xs, mods3, mods3)


def _na_kernel(q_ref, k_ref, v_ref, tab_ref, o_ref, sl_ref, sc_ref, pl_ref, pc_ref, li_ref, *, rows):
    nq = NA_GROUP * GRID_W
    nk = NA_KROWS * GRID_W
    half_win = NA_WIN_ROWS // 2
    left = lax.broadcasted_iota(jnp.int32, (1, LANES), 1) < GRID_W

    s = _dot_nt(q_ref[0:CTX_LEN, :], k_ref[0:CTX_LEN, :])
    p = jnp.exp2(s - jnp.max(s, axis=-1, keepdims=True))
    l = jnp.sum(p, axis=-1, keepdims=True)
    o_ref[0:CTX_LEN, :] = (_dot(p.astype(BF16), v_ref[0:CTX_LEN, :]) / l).astype(o_ref.dtype)

    def geometry(g):
        r0 = jnp.asarray(g, jnp.int32) * NA_GROUP
        start = jnp.clip(r0 - half_win, 0, rows - NA_KROWS)
        q0 = pl.multiple_of(CTX_LEN + r0 * GRID_W, nq)
        k0 = pl.multiple_of(CTX_LEN + start * GRID_W, GRID_W)
        return r0, start, q0, k0

    def stage_a(g, par):
        _, _, q0, k0 = geometry(g)
        q = q_ref[pl.ds(q0, nq), :]
        sl_ref[par] = _dot_nt(q, k_ref[pl.ds(k0, nk), :])
        sc_ref[par] = _dot_nt(q, k_ref[0:CTX_LEN, :])

    def stage_b(g, par):
        r0, start, _, _ = geometry(g)
        for i in range(NA_GROUP):
            r = r0 + i
            rs = jnp.clip(r - half_win, 0, rows - NA_WIN_ROWS)
            rsl = slice(i * GRID_W, (i + 1) * GRID_W)
            tiles = []
            for jp in range(NA_KROWS // 2):
                kr = start + 2 * jp
                ok0 = ((kr >= rs) & (kr < rs + NA_WIN_ROWS)).astype(jnp.int32)
                ok1 = ((kr + 1 >= rs) & (kr + 1 < rs + NA_WIN_ROWS)).astype(jnp.int32)
                e = jnp.clip(kr - r + (NA_WIN_ROWS - 1) + 2, 0, 2 * NA_WIN_ROWS + 1)
                ok = jnp.where(left, ok0, ok1) > 0
                tiles.append(jnp.where(ok, tab_ref[e], NEG_BIG))
            sl = sl_ref[par, rsl, :] + jnp.concatenate(tiles, axis=1)
            sc = sc_ref[par, rsl, :]
            m = jnp.maximum(jnp.max(sl, axis=-1, keepdims=True), jnp.max(sc, axis=-1, keepdims=True))
            pl_i = jnp.exp2(sl - m)
            pc_i = jnp.exp2(sc - m)
            l_i = jnp.sum(pl_i, axis=-1, keepdims=True) + jnp.sum(pc_i, axis=-1, keepdims=True)
            pl_ref[par, rsl, :] = pl_i.astype(BF16)
            pc_ref[par, rsl, :] = pc_i.astype(BF16)
            li_ref[par, rsl, :] = jnp.broadcast_to(1.0 / l_i, (GRID_W, LANES))

    def stage_c(g, par):
        _, _, q0, k0 = geometry(g)
        o = _dot(pl_ref[par], v_ref[pl.ds(k0, nk), :]) + _dot(pc_ref[par], v_ref[0:CTX_LEN, :])
        o_ref[pl.ds(q0, nq), :] = (o * li_ref[par]).astype(o_ref.dtype)

    n_groups = rows // NA_GROUP
    stage_a(0, 0)
    stage_a(1, 1)
    stage_b(0, 0)

    def pair(u, carry):
        t = 2 + 2 * u
        stage_a(t, 0)
        stage_b(t - 1, 1)
        stage_c(t - 2, 0)
        stage_a(t + 1, 1)
        stage_b(t, 0)
        stage_c(t - 1, 1)
        return carry

    lax.fori_loop(0, (n_groups - 2) // 2, pair, 0)
    stage_b(n_groups - 1, 1)
    stage_c(n_groups - 2, 0)
    stage_c(n_groups - 1, 1)


def _na_bias_table(rpb):
    h = rpb.shape[0]
    qc = np.arange(GRID_W)[:, None]
    kc = np.arange(GRID_W)[None, :]
    cs = np.clip(qc - NA_WIN_COLS // 2, 0, GRID_W - NA_WIN_COLS)
    col_ok = (kc >= cs) & (kc < cs + NA_WIN_COLS)
    dc = np.clip(kc - qc + (NA_WIN_COLS - 1), 0, 2 * NA_WIN_COLS - 2)
    a = jnp.take(rpb.astype(F32), jnp.asarray(dc.reshape(-1)), axis=2)
    a = a.reshape(h, 2 * NA_WIN_ROWS - 1, GRID_W, GRID_W) * LOG2E
    a = jnp.where(jnp.asarray(col_ok)[None, None], a, NEG_BIG)
    a = jnp.pad(a, ((0, 0), (2, 2), (0, 0), (0, 0)), constant_values=NEG_BIG)
    return jnp.concatenate([a[:, :-1], a[:, 1:]], axis=-1)


def _na_attention(qkv, rpb, n_batch, tb):
    t = qkv.shape[0]
    rows = (tb - CTX_LEN) // GRID_W
    assert rows % (2 * NA_GROUP) == 0 and rows >= NA_KROWS
    nq, nk = NA_GROUP * GRID_W, NA_KROWS * GRID_W
    table = _na_bias_table(rpb)
    hh = NA_HEADS
    return pl.pallas_call(
        functools.partial(_na_kernel, rows=rows),
        grid=(hh, n_batch),
        in_specs=[
            pl.BlockSpec((tb, LANES), lambda h, b: (b, h)),
            pl.BlockSpec((tb, LANES), lambda h, b: (b, hh + h)),
            pl.BlockSpec((tb, LANES), lambda h, b: (b, 2 * hh + h)),
            pl.BlockSpec((None, 2 * NA_WIN_ROWS + 2, GRID_W, LANES), lambda h, b: (h, 0, 0, 0)),
        ],
        out_specs=pl.BlockSpec((tb, LANES), lambda h, b: (b, h)),
        out_shape=jax.ShapeDtypeStruct((t, D_MODEL), BF16),
        scratch_shapes=[
            pltpu.VMEM((2, nq, nk), F32),
            pltpu.VMEM((2, nq, CTX_LEN), F32),
            pltpu.VMEM((2, nq, nk), BF16),
            pltpu.VMEM((2, nq, CTX_LEN), BF16),
            pltpu.VMEM((2, nq, LANES), F32),
        ],
        compiler_params=_cparams(("arbitrary", "arbitrary")),
        name="na_attention",
    )(qkv, qkv, qkv, table)


def _da_kernel(lam_ref, g_ref, q_ref, k_ref, v_ref, o_ref, acc_ref, m_ref, l_ref,
               s0_ref, s1_ref, p0_ref, p1_ref, a0_ref, a1_ref, qbd_ref, *, lam_init, n_kv_chunks):
    d = DIFF_HEAD_DIM
    i = pl.program_id(2)
    lam = lam_ref[...]
    lam_full = (jnp.exp(jnp.sum(lam[0:1] * lam[1:2], axis=-1, keepdims=True))
                - jnp.exp(jnp.sum(lam[2:3] * lam[3:4], axis=-1, keepdims=True)) + lam_init)
    s_bufs, p_bufs, a_bufs = (s0_ref, s1_ref), (p0_ref, p1_ref), (a0_ref, a1_ref)

    def tile(q0, nq, n_lat_chunks):
        acc_ref[:, 0:nq, :] = jnp.zeros((2, nq, 2 * d), F32)
        m_ref[:, 0:nq, :] = jnp.full((2, nq, LANES), NEG_BIG, F32)
        l_ref[:, 0:nq, :] = jnp.zeros((2, nq, LANES), F32)

        def key_rows(e):
            if isinstance(e, int):
                return (0, CTX_LEN) if e == 0 else (CTX_LEN + (e - 1) * DA_TK, DA_TK)
            return pl.multiple_of(CTX_LEN + (e - 1) * DA_TK, ROW_TILE), DA_TK

        q = q_ref[pl.ds(q0, nq), :]
        zero = jnp.zeros((nq, d), q.dtype)
        qbd_ref[0:nq, :] = jnp.concatenate([q[:, :d], zero], axis=1)
        qbd_ref[nq:2 * nq, :] = jnp.concatenate([zero, q[:, d:]], axis=1)

        def stage_a(e, par):
            k0, nk = key_rows(e)
            s = _dot_nt(qbd_ref[0:2 * nq, :], k_ref[pl.ds(k0, nk), :])
            for c in range(2):
                s_bufs[par][c, 0:nq, 0:nk] = s[c * nq:(c + 1) * nq, :]

        def stage_b(e, par):
            _, nk = key_rows(e)
            s_ref, p_ref, a_ref = s_bufs[par], p_bufs[par], a_bufs[par]
            for c in range(2):
                for sb in range(nq // DA_SUB):
                    rows = slice(sb * DA_SUB, (sb + 1) * DA_SUB)
                    blocks = [s_ref[c, rows, j * LANES:(j + 1) * LANES] for j in range(nk // LANES)]
                    m_prev = m_ref[c, rows, :]
                    m_new = jnp.maximum(
                        m_prev, jnp.max(functools.reduce(jnp.maximum, blocks), axis=-1, keepdims=True))
                    alpha = jnp.exp2(m_prev - m_new)
                    p = [jnp.exp2(blk - m_new) for blk in blocks]
                    l_ref[c, rows, :] = alpha * l_ref[c, rows, :] + functools.reduce(jnp.add, p)
                    for j, pj in enumerate(p):
                        p_ref[c, rows, j * LANES:(j + 1) * LANES] = pj.astype(BF16)
                    a_ref[c, rows, :] = alpha
                    m_ref[c, rows, :] = m_new

        def stage_c(e, par):
            k0, nk = key_rows(e)
            v = v_ref[pl.ds(k0, nk), :]
            p = jnp.concatenate([p_bufs[par][0, 0:nq, 0:nk], p_bufs[par][1, 0:nq, 0:nk]], axis=0)
            pv_all = _dot(p, v)
            for c in range(2):
                pv = pv_all[c * nq:(c + 1) * nq, :]
                alpha = a_bufs[par][c, 0:nq, :]
                acc_ref[c, 0:nq, 0:LANES] = alpha * acc_ref[c, 0:nq, 0:LANES] + pv[:, :LANES]
                acc_ref[c, 0:nq, LANES:] = alpha * acc_ref[c, 0:nq, LANES:] + pv[:, LANES:]

        n = n_lat_chunks
        if n == 0:
            stage_a(0, 0)
            stage_b(0, 0)
            stage_c(0, 0)
        else:
            assert n % 2 == 0
            stage_a(0, 0)
            stage_a(1, 1)
            stage_b(0, 0)
            stage_a(2, 0)
            stage_b(1, 1)
            stage_c(0, 0)

            def pair(u, carry):
                t = 3 + 2 * u
                stage_a(t, 1)
                stage_b(t - 1, 0)
                stage_c(t - 2, 1)
                stage_a(t + 1, 0)
                stage_b(t, 1)
                stage_c(t - 1, 0)
                return carry

            lax.fori_loop(0, (n - 2) // 2, pair, 0)
            stage_b(n, 0)
            stage_c(n - 1, 1)
            stage_c(n, 0)

        r0 = 1.0 / jnp.sum(l_ref[0, 0:nq, :], axis=-1, keepdims=True)
        r1 = lam_full / jnp.sum(l_ref[1, 0:nq, :], axis=-1, keepdims=True)
        o_a = acc_ref[0, 0:nq, 0:LANES] * r0 - acc_ref[1, 0:nq, 0:LANES] * r1
        o_b = acc_ref[0, 0:nq, LANES:] * r0 - acc_ref[1, 0:nq, LANES:] * r1
        ms = (jnp.sum(o_a * o_a, axis=-1, keepdims=True) + jnp.sum(o_b * o_b, axis=-1, keepdims=True)) / (2 * d)
        inv = lax.rsqrt(ms + NORM_EPS) * (1.0 - lam_init)
        g = g_ref[...]
        o_ref[pl.ds(q0, nq), 0:LANES] = (o_a * inv * g[:, :LANES]).astype(o_ref.dtype)
        o_ref[pl.ds(q0, nq), LANES:] = (o_b * inv * g[:, LANES:]).astype(o_ref.dtype)

    @pl.when(i == 0)
    def _():
        tile(0, CTX_LEN, 0)

    @pl.when(i > 0)
    def _():
        tile(pl.multiple_of(CTX_LEN + (i - 1) * DA_TQ, ROW_TILE), DA_TQ, n_kv_chunks)


def _diff_attention(qkv, lam, subln_g, layer_idx, n_batch, tb):
    t = qkv.shape[0]
    hh = DIFF_HEADS
    w = 2 * DIFF_HEAD_DIM
    n_lat = (tb - CTX_LEN) // DA_TQ
    lam_init = 0.8 - 0.6 * math.exp(-0.3 * layer_idx)
    return pl.pallas_call(
        functools.partial(_da_kernel, lam_init=lam_init, n_kv_chunks=(tb - CTX_LEN) // DA_TK),
        grid=(n_batch, hh, 1 + n_lat),
        in_specs=[
            pl.BlockSpec((4, DIFF_HEAD_DIM), lambda b, h, i: (0, 0)),
            pl.BlockSpec((1, w), lambda b, h, i: (0, 0)),
            pl.BlockSpec((tb, w), lambda b, h, i: (b, h)),
            pl.BlockSpec((tb, w), lambda b, h, i: (b, hh + h)),
            pl.BlockSpec((tb, w), lambda b, h, i: (b, 2 * hh + h)),
        ],
        out_specs=pl.BlockSpec((tb, w), lambda b, h, i: (b, h)),
        out_shape=jax.ShapeDtypeStruct((t, D_MODEL), BF16),
        scratch_shapes=[
            pltpu.VMEM((2, DA_TQ, w), F32),
            pltpu.VMEM((2, DA_TQ, LANES), F32),
            pltpu.VMEM((2, DA_TQ, LANES), F32),
            pltpu.VMEM((2, DA_TQ, DA_TK), F32),
            pltpu.VMEM((2, DA_TQ, DA_TK), F32),
            pltpu.VMEM((2, DA_TQ, DA_TK), BF16),
            pltpu.VMEM((2, DA_TQ, DA_TK), BF16),
            pltpu.VMEM((2, DA_TQ, LANES), F32),
            pltpu.VMEM((2, DA_TQ, LANES), F32),
            pltpu.VMEM((2 * DA_TQ, w), BF16),
        ],
        compiler_params=_cparams(("arbitrary", "arbitrary", "arbitrary")),
        name="diff_attention",
    )(lam.astype(F32), subln_g.astype(F32).reshape(1, w), qkv, qkv, qkv)


def _hg_kernel(q_ref, i_ref, ff_ref, fb_ref, lb_ref, o_ref, acc_ref, stf_ref, stb_ref, *, n_groups):
    c = HGRN_CHUNK
    rg = HG_GROUP * c
    dk = HGRN_DK
    acc_ref[...] = jnp.zeros(acc_ref.shape, F32)
    stf_ref[...] = jnp.zeros(stf_ref.shape, F32)
    stb_ref[...] = jnp.zeros(stb_ref.shape, F32)
    row = lax.broadcasted_iota(jnp.int32, (rg, rg), 0)
    col = lax.broadcasted_iota(jnp.int32, (rg, rg), 1)
    same = (row // c) == (col // c)
    lower = same & (row >= col)
    upper = same & (col >= row)
    lower_b = lower.astype(BF16)
    upper_b = upper.astype(BF16)
    chunk_of_row = lax.broadcasted_iota(jnp.int32, (rg, 1), 0) // c
    lb_all = lb_ref[...]

    def per_chunk_rows(b, r):
        return jnp.concatenate(
            [jnp.broadcast_to(b[g * c + r:g * c + r + 1, :], (c, dk)) for g in range(HG_GROUP)], axis=0)

    def group(hd, gidx, f_ref, lb_row, mask, mask_b, st_ref, end_row, order):
        r0 = pl.multiple_of(gidx * rg, rg)
        cols = slice(hd * dk, (hd + 1) * dk)
        lb = lb_all[lb_row:lb_row + 1, cols]
        qr = q_ref[pl.ds(r0, rg), cols].astype(F32)
        q = qr * jax.nn.sigmoid(qr) * (dk ** -0.5)
        v = i_ref[pl.ds(r0, rg), cols]
        fr = f_ref[pl.ds(r0, rg), cols].astype(F32)
        fg = lb + (1.0 - lb) * jax.nn.sigmoid(fr)
        k = 1.0 - fg
        lf = jnp.log(fg)
        hi = lf.astype(BF16)
        rem = lf - hi.astype(F32)
        mid = rem.astype(BF16)
        lo = (rem - mid.astype(F32)).astype(BF16)
        cs = _dot(mask_b, jnp.concatenate([hi, mid, lo], axis=1))
        bcum = cs[:, 0:dk] + cs[:, dk:2 * dk] + cs[:, 2 * dk:]
        b_end = per_chunk_rows(bcum, end_row)
        b_mid = per_chunk_rows(bcum, c // 2)
        qt = (q * jnp.exp(bcum - b_mid)).astype(BF16)
        kt = (k * jnp.exp(b_mid - bcum)).astype(BF16)
        a = jnp.where(mask, _dot_nt(qt, kt), 0.0).astype(BF16)
        o_intra = _dot(a, v)
        k_end = k * jnp.exp(b_end - bcum)
        k_exp = jnp.concatenate(
            [jnp.where(chunk_of_row == g, k_end, 0.0) for g in range(HG_GROUP)], axis=1).astype(BF16)
        u = _dot_tn(v, k_exp)
        st = st_ref[hd]
        st_in = [None] * HG_GROUP
        for g in order:
            st_in[g] = st.astype(BF16)
            st = st * jnp.exp(bcum[g * c + end_row:g * c + end_row + 1, :]) + u[:, g * dk:(g + 1) * dk]
        st_ref[hd] = st
        oi = _dot_nt((q * jnp.exp(bcum)).astype(BF16), jnp.concatenate(st_in, axis=0))
        o_inter = jnp.concatenate(
            [oi[g * c:(g + 1) * c, g * dk:(g + 1) * dk] for g in range(HG_GROUP)], axis=0)
        acc_ref[pl.ds(r0, rg), cols] += o_intra + o_inter

    fwd_order = list(range(HG_GROUP))

    def body(n, carry):
        nb = jnp.where(n == 0, 0, n_groups - n)
        for hd in range(HG_HEADS):
            group(hd, n, ff_ref, 0, lower, lower_b, stf_ref, c - 1, fwd_order)
            group(hd, nb, fb_ref, 1, upper, upper_b, stb_ref, 0, fwd_order[::-1])
        return carry

    lax.fori_loop(0, n_groups, body, 0)
    o_ref[...] = acc_ref[...].astype(o_ref.dtype)


def _hgrn_scan(proj, lb2, n_batch, tb):
    t = proj.shape[0]
    hh = HGRN_HEADS // HG_HEADS
    wd = HG_HEADS * LANES
    rg = HG_GROUP * HGRN_CHUNK
    assert CTX_LEN == rg and tb % rg == 0

    def col(k):
        return pl.BlockSpec((tb, wd), lambda b, h: (b, k * hh + h), pipeline_mode=pl.Buffered(1))

    return pl.pallas_call(
        functools.partial(_hg_kernel, n_groups=tb // rg),
        grid=(n_batch, hh),
        in_specs=[col(0), col(1), col(3), col(4), pl.BlockSpec((2, wd), lambda b, h: (0, h))],
        out_specs=pl.BlockSpec((tb, wd), lambda b, h: (b, h)),
        out_shape=jax.ShapeDtypeStruct((t, D_MODEL), BF16),
        scratch_shapes=[pltpu.VMEM((tb, wd), F32),
                        pltpu.VMEM((HG_HEADS, LANES, LANES), F32), pltpu.VMEM((HG_HEADS, LANES, LANES), F32)],
        compiler_params=_cparams(("arbitrary", "arbitrary")),
        name="hgrn_scan",
    )(proj, proj, proj, proj, lb2)


def _hg_finish_kernel(o_ref, g_ref, ng_ref, h_ref):
    o = o_ref[...].astype(F32)
    y = o * lax.rsqrt(jnp.mean(o * o, axis=-1, keepdims=True) + NORM_EPS) * ng_ref[...]
    g = g_ref[...].astype(F32)
    h_ref[...] = (y * (g * jax.nn.sigmoid(g))).astype(h_ref.dtype)


def _hgrn_finish(o, proj, norm_g):
    t, d = o.shape
    return pl.pallas_call(
        _hg_finish_kernel,
        grid=(t // ROW_TILE,),
        in_specs=[
            pl.BlockSpec((ROW_TILE, d), lambda i: (i, 0)),
            pl.BlockSpec((ROW_TILE, d), lambda i: (i, 2)),
            pl.BlockSpec((1, d), lambda i: (0, 0)),
        ],
        out_specs=pl.BlockSpec((ROW_TILE, d), lambda i: (i, 0)),
        out_shape=jax.ShapeDtypeStruct((t, d), BF16),
        compiler_params=_cparams(("arbitrary",)),
        name="hgrn_finish",
    )(o, proj, norm_g.astype(F32).reshape(1, d))


def _moe_kernel(be_ref, first_ref, nxt_ref, nu_ref, tok_ref, x_hbm, wg_hbm, wu_hbm, wd_hbm, o_ref,
                xbuf, sg_ref, su_ref, sd_ref, bg_ref, bu_ref, bd_ref, sem, xsem, *, layer):
    b = pl.program_id(0)
    n_used = nu_ref[0]

    def weight_copies(e):
        return (pltpu.make_async_copy(wg_hbm.at[layer, e], sg_ref, sem.at[0]),
                pltpu.make_async_copy(wu_hbm.at[layer, e], su_ref, sem.at[1]),
                pltpu.make_async_copy(wd_hbm.at[layer, e], sd_ref, sem.at[2]))

    def start_gather(blk, slot):
        base = blk * MOE_TM
        for r in range(MOE_TM):
            pltpu.make_async_copy(x_hbm.at[pl.ds(tok_ref[base + r], 1)], xbuf.at[slot, pl.ds(r, 1)],
                                  xsem.at[slot]).start()

    def wait_gather(slot):
        pltpu.make_async_copy(x_hbm.at[pl.ds(0, MOE_TM)], xbuf.at[slot], xsem.at[slot]).wait()

    @pl.when(b == 0)
    def _():
        for cp in weight_copies(be_ref[0]):
            cp.start()
        start_gather(0, 0)
        start_gather(1, 1)

    @pl.when(first_ref[b] == 1)
    def _():
        for cp in weight_copies(be_ref[b]):
            cp.wait()
        for src, dst in ((sg_ref, bg_ref), (su_ref, bu_ref), (sd_ref, bd_ref)):
            slab = src.shape[0] // MOE_CAST_SLABS

            def cast(r, carry, src=src, dst=dst, slab=slab):
                r0 = pl.multiple_of(r * slab, slab)
                dst[pl.ds(r0, slab), :] = src[pl.ds(r0, slab), :].astype(BF16)
                return carry

            lax.fori_loop(0, MOE_CAST_SLABS, cast, 0)

        @pl.when(nxt_ref[b] >= 0)
        def _():
            for cp in weight_copies(nxt_ref[b]):
                cp.start()

    @pl.when(b < n_used)
    def _():
        slot = b % MOE_XBUFS
        wait_gather(slot)
        xu = xbuf[slot]
        x = jnp.concatenate(
            [lax.bitcast_convert_type(xu << 16, F32).astype(BF16),
             lax.bitcast_convert_type(xu & jnp.uint32(0xFFFF0000), F32).astype(BF16)], axis=1)
        g = _dot(x, bg_ref[...])
        u = _dot(x, bu_ref[...])
        h = (g * jax.nn.sigmoid(g) * u).astype(BF16)
        o_ref[...] = _dot(h, bd_ref[...]).astype(o_ref.dtype)
        start_gather(b + 2, (b + 2) % MOE_XBUFS)

    @pl.when(b >= n_used)
    def _():
        @pl.when(b == n_used)
        def _():
            wait_gather(b % MOE_XBUFS)
            wait_gather((b + 1) % MOE_XBUFS)

        o_ref[...] = jnp.zeros(o_ref.shape, o_ref.dtype)


def _moe_experts(hp, slot_tok, blk_expert, blk_first, blk_next, n_used, w_gate, w_up, w_down, layer):
    d = 2 * hp.shape[1]
    f = w_gate.shape[3]
    n_slots = slot_tok.shape[0]
    n_blocks = n_slots // MOE_TM
    grid_spec = pltpu.PrefetchScalarGridSpec(
        num_scalar_prefetch=5,
        grid=(n_blocks,),
        in_specs=[pl.BlockSpec(memory_space=pl.ANY)] * 4,
        out_specs=pl.BlockSpec((MOE_TM, d), lambda b, *_: (b, 0)),
        scratch_shapes=[
            pltpu.VMEM((MOE_XBUFS, MOE_TM, d // 2), jnp.uint32),
            pltpu.VMEM((d, f), F32), pltpu.VMEM((d, f), F32), pltpu.VMEM((f, d), F32),
            pltpu.VMEM((d, f), BF16), pltpu.VMEM((d, f), BF16), pltpu.VMEM((f, d), BF16),
            pltpu.SemaphoreType.DMA((3,)),
            pltpu.SemaphoreType.DMA((MOE_XBUFS,)),
        ],
    )
    return pl.pallas_call(
        functools.partial(_moe_kernel, layer=layer),
        grid_spec=grid_spec,
        out_shape=jax.ShapeDtypeStruct((n_slots, d), BF16),
        compiler_params=_cparams(("arbitrary",)),
        name="moe_experts",
    )(blk_expert, blk_first, blk_next, n_used, slot_tok, hp, w_gate, w_up, w_down)


def _route(route, counts_row, n_blocks):
    t = route.shape[0]
    e_idx = route[:, 0:TOP_K].astype(jnp.int32)
    rank = route[:, TOP_K:2 * TOP_K].astype(jnp.int32)
    w = route[:, 2 * TOP_K:3 * TOP_K]
    counts = counts_row[0, :N_EXPERTS].astype(jnp.int32)
    pcounts = (counts + MOE_TM - 1) // MOE_TM * MOE_TM
    pends = jnp.cumsum(pcounts)
    pstarts = pends - pcounts
    e_ids = jnp.arange(N_EXPERTS, dtype=jnp.int32)
    start_of = jnp.sum(jnp.where(e_idx[..., None] == e_ids, pstarts, 0), axis=-1)
    dest = (start_of + rank).reshape(-1).astype(jnp.int32)
    flat_t = jnp.arange(t * TOP_K, dtype=jnp.int32) // TOP_K
    slot_tok = jnp.zeros((n_blocks * MOE_TM,), jnp.int32).at[dest].set(flat_t)
    blk_row0 = jnp.arange(n_blocks, dtype=jnp.int32) * MOE_TM
    blk_expert = jnp.minimum(
        jnp.sum((pends[None, :] <= blk_row0[:, None]).astype(jnp.int32), axis=1), N_EXPERTS - 1).astype(jnp.int32)
    n_used = (pends[-1] // MOE_TM).astype(jnp.int32).reshape(1)
    blk_ids = jnp.arange(n_blocks, dtype=jnp.int32)
    prev_expert = jnp.concatenate([jnp.full((1,), -1, jnp.int32), blk_expert[:-1]])
    blk_first = ((blk_ids < n_used[0]) & (blk_expert != prev_expert)).astype(jnp.int32)
    e_ids = jnp.arange(N_EXPERTS, dtype=jnp.int32)
    later_used = (e_ids[None, :] > e_ids[:, None]) & (counts[None, :] > 0)
    next_used = jnp.min(jnp.where(later_used, e_ids[None, :], N_EXPERTS), axis=1)
    next_used = jnp.where(next_used == N_EXPERTS, -1, next_used).astype(jnp.int32)
    blk_next = next_used[blk_expert]
    return w, dest.reshape(t, TOP_K), slot_tok, (blk_expert, blk_first, blk_next, n_used)


def _moe_res_kernel(xs_ref, y0_ref, y1_ref, w_ref, gate_ref, g_ref, shift_ref, scale_ref, o_ref, h_ref):
    w = w_ref[...]
    y = w[:, 0:1] * y0_ref[...].astype(F32) + w[:, 1:2] * y1_ref[...].astype(F32)
    x = xs_ref[...] + gate_ref[...] * y
    o_ref[...] = x
    xn = x * lax.rsqrt(jnp.mean(x * x, axis=-1, keepdims=True) + NORM_EPS)
    h_ref[...] = (xn * g_ref[...] * (1.0 + scale_ref[...]) + shift_ref[...]).astype(h_ref.dtype)


def _moe_res_final_kernel(xs_ref, y0_ref, y1_ref, w_ref, gate_ref, fg_ref, o_ref):
    w = w_ref[...]
    y = w[:, 0:1] * y0_ref[...].astype(F32) + w[:, 1:2] * y1_ref[...].astype(F32)
    x = xs_ref[...] + gate_ref[...] * y
    o_ref[...] = x * lax.rsqrt(jnp.mean(x * x, axis=-1, keepdims=True) + NORM_EPS) * fg_ref[...]


def _moe_residual(xs, y0, y1, w, mods3, g3, layer, n_batch, tpb):
    t, d = xs.shape
    nb = n_batch
    row = lambda i: (i, 0)

    def mod_spec(lyr, k):
        return pl.BlockSpec((None, 1, d), lambda i: (lyr * 8 + _row_sel(i, tpb, nb), 0, k))

    return pl.pallas_call(
        _moe_res_kernel,
        grid=(t // ROW_TILE,),
        in_specs=[
            pl.BlockSpec((ROW_TILE, d), row),
            pl.BlockSpec((ROW_TILE, d), row),
            pl.BlockSpec((ROW_TILE, d), row),
            pl.BlockSpec((ROW_TILE, TOP_K), row),
            mod_spec(layer, 5),
            pl.BlockSpec((None, 1, d), lambda i: ((layer + 1) * 2, 0, 0)),
            mod_spec(layer + 1, 0),
            mod_spec(layer + 1, 1),
        ],
        out_specs=[pl.BlockSpec((ROW_TILE, d), row), pl.BlockSpec((ROW_TILE, d), row)],
        out_shape=[jax.ShapeDtypeStruct((t, d), F32), jax.ShapeDtypeStruct((t, d), BF16)],
        input_output_aliases={0: 0},
        compiler_params=_cparams(("arbitrary",)),
        name="moe_residual",
    )(xs, y0, y1, w, mods3, g3, mods3, mods3)


def _moe_residual_final(xs, y0, y1, w, mods3, layer, final_g, n_batch, tpb):
    t, d = xs.shape
    lat = tpb - CTX_LEN // ROW_TILE
    src = lambda i: ((i // lat) * tpb + CTX_LEN // ROW_TILE + i % lat, 0)
    return pl.pallas_call(
        _moe_res_final_kernel,
        grid=(n_batch * lat,),
        in_specs=[
            pl.BlockSpec((ROW_TILE, d), src),
            pl.BlockSpec((ROW_TILE, d), src),
            pl.BlockSpec((ROW_TILE, d), src),
            pl.BlockSpec((ROW_TILE, TOP_K), src),
            pl.BlockSpec((None, 1, d), lambda i: (layer * 8 + i // lat, 0, 5)),
            pl.BlockSpec((1, d), lambda i: (0, 0)),
        ],
        out_specs=pl.BlockSpec((ROW_TILE, d), lambda i: (i, 0)),
        out_shape=jax.ShapeDtypeStruct((n_batch * lat * ROW_TILE, d), F32),
        compiler_params=_cparams(("arbitrary",)),
        name="moe_residual_final",
    )(xs, y0, y1, w, mods3, final_g.astype(F32).reshape(1, d))


def _rope_tables(n_batch, seq):
    quarter = DIFF_HEAD_DIM // 4
    tpos = jnp.arange(seq)
    rowp = (tpos // GRID_W).astype(F32)
    colp = (tpos % GRID_W).astype(F32)
    inv = ROPE_THETA ** (-jnp.arange(quarter, dtype=F32) / quarter)
    ang_r = rowp[:, None] * inv[None, :]
    ang_c = colp[:, None] * inv[None, :]
    cos = jnp.concatenate([jnp.cos(ang_r)] * 2 + [jnp.cos(ang_c)] * 2, axis=-1)
    sin = jnp.concatenate([-jnp.sin(ang_r), jnp.sin(ang_r), -jnp.sin(ang_c), jnp.sin(ang_c)], axis=-1)
    cos = jnp.concatenate([jnp.ones((CTX_LEN, LANES), F32), cos], axis=0)
    sin = jnp.concatenate([jnp.zeros((CTX_LEN, LANES), F32), sin], axis=0)
    return jnp.tile(cos, (n_batch, 1)), jnp.tile(sin, (n_batch, 1))


def kernel(x, c, ctx, c_ctx, ada_w, ada_b, norm_g, na_w_qkv, na_rpb, na_w_o, da_w_qkv, da_lambda, da_subln_g, da_w_o, hg_w_in, hg_lb, hg_norm_g, hg_w_o, moe_w_router, moe_b_router, moe_w_gate, moe_w_up, moe_w_down, final_g):
    n_batch, seq, d = x.shape
    tb = CTX_LEN + seq
    t = n_batch * tb
    tpb = tb // ROW_TILE
    assert d == D_MODEL and ctx.shape[1] == CTX_LEN and n_batch + 1 <= 8
    assert seq % DA_TQ == 0 and t % MM_TM == 0 and seq // GRID_W >= NA_WIN_ROWS

    xs = jnp.concatenate([ctx, x], axis=1).reshape(t, d).astype(F32)

    c_rows = jnp.zeros((8, d), F32).at[:n_batch].set(c.astype(F32)).at[n_batch].set(c_ctx.astype(F32))
    mods = _ada_mods(c_rows, ada_w.astype(F32), ada_b.astype(F32))
    mods3 = mods.reshape(DEPTH * 8, 1, 6 * d)
    g3 = norm_g.astype(F32).reshape(DEPTH * 2, 1, d)

    lb_p = jax.nn.softmax(hg_lb.astype(F32), axis=1)
    lb_all = jnp.cumsum(lb_p, axis=1) - lb_p[:, :1]

    wr = jnp.zeros((d, LANES), F32).at[:, :N_EXPERTS].set(moe_w_router.astype(F32))
    wr_hi = wr.astype(BF16)
    wr_lo = (wr - wr_hi.astype(F32)).astype(BF16)
    rb = jnp.zeros((1, LANES), F32).at[0, :N_EXPERTS].set(moe_b_router.astype(F32))

    n_blocks = -(-(t * TOP_K) // MOE_TM) + N_EXPERTS + (MOE_XBUFS - 1)
    rope = _rope_tables(n_batch, seq)
    wg = moe_w_gate.astype(F32)
    wu = moe_w_up.astype(F32)
    wd = moe_w_down.astype(F32)
    na_qs = NA_HEAD_DIM ** -0.5 * LOG2E
    da_qs = DIFF_HEAD_DIM ** -0.5 * LOG2E

    counters = [0, 0, 0]
    out = None
    for i in range(DEPTH):
        kind = i % N_MIXERS
        j = counters[kind]
        counters[kind] += 1

        if i == 0:
            h = _norm_mod(xs, mods3, g3, i, 0, n_batch, tpb)[0]
        if kind == 0:
            qkv = _matmul(h, na_w_qkv[j], q_cols=d, q_scale=na_qs)
            o = _na_attention(qkv, na_rpb[j], n_batch, tb)
            w_o = na_w_o[j]
        elif kind == 1:
            qkv = _matmul(h, da_w_qkv[j], q_cols=d, q_scale=da_qs, rope=rope, n_rope_cols=2 * d)
            o = _diff_attention(qkv, da_lambda[j], da_subln_g[j], i, n_batch, tb)
            w_o = da_w_o[j]
        else:
            proj = _matmul(h, hg_w_in[j])
            o32 = _hgrn_scan(proj, lb_all[:, i, :], n_batch, tb)
            o = _hgrn_finish(o32, proj, hg_norm_g[j])
            w_o = hg_w_o[j]
        xs = _matmul_residual(o, w_o, xs, mods3, i, 2, n_batch, tpb)

        hp, route, counts = _norm_mod(xs, mods3, g3, i, 1, n_batch, tpb, router=(wr_hi, wr_lo, rb))
        w, dest, slot_tok, blk_info = _route(route, counts, n_blocks)
        yb = _moe_experts(hp, slot_tok, *blk_info, wg, wu, wd, i)
        y0 = yb.at[dest[:, 0]].get(mode="promise_in_bounds")
        y1 = yb.at[dest[:, 1]].get(mode="promise_in_bounds")
        if i < DEPTH - 1:
            xs, h = _moe_residual(xs, y0, y1, w, mods3, g3, i, n_batch, tpb)
        else:
            out = _moe_residual_final(xs, y0, y1, w, mods3, i, final_g, n_batch, tpb)
    return out.reshape(n_batch, seq, d).astype(x.dtype)
```

```python
import functools
import math

import numpy as np
import jax
import jax.numpy as jnp
from jax import lax
from jax.experimental import pallas as pl
from jax.experimental.pallas import tpu as pltpu

F32 = jnp.float32
BF16 = jnp.bfloat16

D_MODEL = 2048
DEPTH = 4
GRID_W = 64
CTX_LEN = 256
N_MIXERS = 3
NORM_EPS = 1e-6
ROPE_THETA = 10000.0

NA_HEADS = 16
NA_HEAD_DIM = 128
NA_WIN_ROWS = 8
NA_WIN_COLS = 16

DIFF_HEADS = 8
DIFF_HEAD_DIM = 128

HGRN_HEADS = 16
HGRN_DK = 128
HGRN_CHUNK = 64

N_EXPERTS = 16
N_GROUPS = 4
EXPERTS_PER_GROUP = 4
TOP_K = 2
D_EXPERT = 1024

LANES = 128
ROW_TILE = 256
MM_TM = 512
MM_TN = 1024
MOE_TM = 256
DA_TQ = 512
VMEM_LIMIT = 56 * 1024 * 1024

NA_GROUP = 4
NA_KROWS = 12
DA_SUB = 64
DA_TK = 512
HG_GROUP = 4
HG_HEADS = 1
HG_SAFE_FORGET = math.exp(-2.0)
MOE_XBUFS = 3
MOE_CAST_SLABS = 8

NEG_BIG = -1e30
LOG2E = 1.4426950408889634


def _cparams(sem):
    return pltpu.CompilerParams(dimension_semantics=sem, vmem_limit_bytes=VMEM_LIMIT)


def _dot(a, b):
    return jnp.dot(a, b, preferred_element_type=F32)


def _dot_nt(a, b):
    return lax.dot_general(a, b, (((1,), (1,)), ((), ())), preferred_element_type=F32)


def _dot_tn(a, b):
    return lax.dot_general(a, b, (((0,), (0,)), ((), ())), preferred_element_type=F32)


def _row_sel(sub, tiles_per_batch, n_batch):
    return jnp.where(sub % tiles_per_batch == 0, n_batch, sub // tiles_per_batch)


def _ada_kernel(c_ref, w_ref, b_ref, o_ref):
    c = c_ref[...]
    a = c * jax.nn.sigmoid(c)
    o_ref[...] = jnp.dot(a, w_ref[...], preferred_element_type=F32,
                         precision=lax.Precision.HIGHEST) + b_ref[...]


def _ada_mods(c_rows, ada_w, ada_b):
    depth, d, n = ada_w.shape
    tn = 1024
    return pl.pallas_call(
        _ada_kernel,
        grid=(depth, n // tn),
        in_specs=[
            pl.BlockSpec((8, d), lambda l, j: (0, 0)),
            pl.BlockSpec((None, d, tn), lambda l, j: (l, 0, j)),
            pl.BlockSpec((None, 1, tn), lambda l, j: (l, 0, j)),
        ],
        out_specs=pl.BlockSpec((None, 8, tn), lambda l, j: (l, 0, j)),
        out_shape=jax.ShapeDtypeStruct((depth, 8, n), F32),
        compiler_params=_cparams(("arbitrary", "arbitrary")),
        name="ada_mods",
    )(c_rows, ada_w, ada_b.reshape(depth, 1, n))


def _norm_mod_kernel(x_ref, g_ref, shift_ref, scale_ref, h_ref):
    x = x_ref[...]
    y = x * lax.rsqrt(jnp.mean(x * x, axis=-1, keepdims=True) + NORM_EPS)
    h = y * g_ref[...] * (1.0 + scale_ref[...]) + shift_ref[...]
    h_ref[...] = h.astype(h_ref.dtype)


def _route_tile(lg, bias, run_ref):
    rows = lg.shape[0]
    lane = lax.broadcasted_iota(jnp.int32, (1, LANES), 1)
    lanef = lane.astype(F32)
    neg = -jnp.inf
    far = float(LANES)
    rmax = lambda a: jnp.max(a, axis=-1, keepdims=True)
    rmin = lambda a: jnp.min(a, axis=-1, keepdims=True)
    rsum = lambda a: jnp.sum(a, axis=-1, keepdims=True)

    s = jax.nn.sigmoid(lg)
    sel = jnp.where(lane < N_EXPERTS, s + bias, neg)
    best = None
    for g in range(N_GROUPS):
        ing = (lane // EXPERTS_PER_GROUP) == g
        v = jnp.where(ing, sel, neg)
        m1 = rmax(v)
        i1 = rmin(jnp.where(v == m1, lanef, far))
        v2 = jnp.where(lanef == i1, neg, v)
        m2 = rmax(v2)
        i2 = rmin(jnp.where(v2 == m2, lanef, far))
        cand = (m1 + m2, i1, i2)
        if best is None:
            best = cand
        else:
            better = cand[0] > best[0]
            best = tuple(jnp.where(better, c, b) for c, b in zip(cand, best))
    _, e1, e2 = best
    is1 = lanef == e1
    is2 = lanef == e2
    s1 = rsum(jnp.where(is1, s, 0.0))
    s2 = rsum(jnp.where(is2, s, 0.0))
    w1 = s1 / (s1 + s2)
    w2 = s2 / (s1 + s2)
    onehot = jnp.where(is1 | is2, 1.0, 0.0)
    row = lax.broadcasted_iota(jnp.int32, (rows, rows), 0)
    col = lax.broadcasted_iota(jnp.int32, (rows, rows), 1)
    before = jnp.where(row > col, 1.0, 0.0).astype(BF16)
    prior = _dot(before, onehot.astype(BF16)) + run_ref[...]
    r1 = rsum(jnp.where(is1, prior, 0.0))
    r2 = rsum(jnp.where(is2, prior, 0.0))
    run_ref[...] += jnp.sum(onehot, axis=0, keepdims=True)
    out = jnp.zeros((rows, LANES), F32)
    for k, val in enumerate((e1, e2, r1, r2, w1, w2)):
        out = jnp.where(lane == k, val, out)
    return out


def _norm_mod_router_kernel(x_ref, g_ref, shift_ref, scale_ref, whi_ref, wlo_ref, rb_ref,
                            h_ref, route_ref, cnt_ref, run_ref):
    @pl.when(pl.program_id(0) == 0)
    def _():
        run_ref[...] = jnp.zeros(run_ref.shape, F32)

    x = x_ref[...]
    y = x * lax.rsqrt(jnp.mean(x * x, axis=-1, keepdims=True) + NORM_EPS)
    h = y * g_ref[...] * (1.0 + scale_ref[...]) + shift_ref[...]
    h_hi = h.astype(BF16)
    h_lo = (h - h_hi.astype(F32)).astype(BF16)
    bits = lax.bitcast_convert_type(h_hi.astype(F32), jnp.uint32)
    half = bits.shape[1] // 2
    h_ref[...] = (bits[:, :half] >> 16) | bits[:, half:]
    lg = _dot(h_hi, whi_ref[...]) + _dot(h_hi, wlo_ref[...]) + _dot(h_lo, whi_ref[...])
    route_ref[...] = _route_tile(lg, rb_ref[...], run_ref)
    cnt_ref[...] = run_ref[...]


def _norm_mod(xs, mods3, g3, layer, which, n_batch, tpb, router=None):
    t, d = xs.shape
    shift_k, scale_k = (0, 1) if which == 0 else (3, 4)
    nb = n_batch

    def mod_spec(k):
        return pl.BlockSpec((None, 1, d), lambda i: (layer * 8 + _row_sel(i, tpb, nb), 0, k))

    in_specs = [
        pl.BlockSpec((ROW_TILE, d), lambda i: (i, 0)),
        pl.BlockSpec((None, 1, d), lambda i: (layer * 2 + which, 0, 0)),
        mod_spec(shift_k),
        mod_spec(scale_k),
    ]
    args = [xs, g3, mods3, mods3]
    out_specs = [pl.BlockSpec((ROW_TILE, d), lambda i: (i, 0))]
    out_shape = [jax.ShapeDtypeStruct((t, d), BF16)]
    kern = _norm_mod_kernel
    scratch = []
    if router is not None:
        in_specs += [pl.BlockSpec((d, LANES), lambda i: (0, 0))] * 2 + [pl.BlockSpec((1, LANES), lambda i: (0, 0))]
        args += list(router)
        out_specs = [pl.BlockSpec((ROW_TILE, d // 2), lambda i: (i, 0)),
                     pl.BlockSpec((ROW_TILE, LANES), lambda i: (i, 0)),
                     pl.BlockSpec((1, LANES), lambda i: (0, 0))]
        out_shape = [jax.ShapeDtypeStruct((t, d // 2), jnp.uint32),
                     jax.ShapeDtypeStruct((t, LANES), F32),
                     jax.ShapeDtypeStruct((1, LANES), F32)]
        scratch = [pltpu.VMEM((1, LANES), F32)]
        kern = _norm_mod_router_kernel
    return pl.pallas_call(
        kern,
        grid=(t // ROW_TILE,),
        in_specs=in_specs,
        out_specs=out_specs,
        out_shape=out_shape,
        scratch_shapes=scratch,
        compiler_params=_cparams(("arbitrary",)),
        name="norm_mod",
    )(*args)


def _rope_store(acc, cos_ref, sin_ref, o_ref):
    lane = lax.broadcasted_iota(jnp.int32, (1, LANES), 1)
    first = (lane % 64) < 32
    cos = cos_ref[...]
    sin = sin_ref[...]
    for c in range(acc.shape[1] // LANES):
        xc = acc[:, c * LANES:(c + 1) * LANES]
        sw = jnp.where(first, pltpu.roll(xc, 96, 1), pltpu.roll(xc, 32, 1))
        o_ref[:, c * LANES:(c + 1) * LANES] = (xc * cos + sw * sin).astype(o_ref.dtype)


def _mm_kernel(x_ref, w_ref, o_ref, wb_ref, *, n_q, q_scale):
    @pl.when(pl.program_id(1) == 0)
    def _():
        wb_ref[...] = w_ref[...].astype(BF16)

    acc = _dot(x_ref[...], wb_ref[...])
    j = pl.program_id(0)

    @pl.when(j < n_q)
    def _():
        o_ref[...] = (acc * q_scale).astype(o_ref.dtype)

    @pl.when(j >= n_q)
    def _():
        o_ref[...] = acc.astype(o_ref.dtype)


def _mm_rope_kernel(x_ref, w_ref, cos_ref, sin_ref, o_ref, wb_ref, *, n_q, q_scale, n_rope):
    @pl.when(pl.program_id(1) == 0)
    def _():
        wb_ref[...] = w_ref[...].astype(BF16)

    acc = _dot(x_ref[...], wb_ref[...])
    j = pl.program_id(0)

    @pl.when(j < n_q)
    def _():
        _rope_store(acc * q_scale, cos_ref, sin_ref, o_ref)

    @pl.when((j >= n_q) & (j < n_rope))
    def _():
        _rope_store(acc, cos_ref, sin_ref, o_ref)

    @pl.when(j >= n_rope)
    def _():
        o_ref[...] = acc.astype(o_ref.dtype)


def _mm_res_kernel(x_ref, w_ref, xs_ref, ga_ref, gb_ref, o_ref, wb_ref):
    @pl.when(pl.program_id(1) == 0)
    def _():
        wb_ref[...] = w_ref[...].astype(BF16)

    acc = _dot(x_ref[...], wb_ref[...])
    half = ROW_TILE
    o_ref[:half, :] = xs_ref[:half, :] + ga_ref[...] * acc[:half, :]
    o_ref[half:, :] = xs_ref[half:, :] + gb_ref[...] * acc[half:, :]


def _matmul(x, w, q_cols=0, q_scale=1.0, rope=None, n_rope_cols=0):
    t, k = x.shape
    n = w.shape[1]
    in_specs = [
        pl.BlockSpec((MM_TM, k), lambda j, i: (i, 0)),
        pl.BlockSpec((k, MM_TN), lambda j, i: (0, j)),
    ]
    args = [x, w]
    kern = functools.partial(_mm_kernel, n_q=q_cols // MM_TN, q_scale=q_scale)
    if rope is not None:
        in_specs += [pl.BlockSpec((MM_TM, LANES), lambda j, i: (i, 0))] * 2
        args += list(rope)
        kern = functools.partial(_mm_rope_kernel, n_q=q_cols // MM_TN, q_scale=q_scale,
                                 n_rope=n_rope_cols // MM_TN)
    return pl.pallas_call(
        kern,
        grid=(n // MM_TN, t // MM_TM),
        in_specs=in_specs,
        out_specs=pl.BlockSpec((MM_TM, MM_TN), lambda j, i: (i, j)),
        out_shape=jax.ShapeDtypeStruct((t, n), BF16),
        scratch_shapes=[pltpu.VMEM((k, MM_TN), BF16)],
        compiler_params=_cparams(("arbitrary", "arbitrary")),
        name="matmul",
    )(*args)


def _matmul_residual(x, w, xs, mods3, layer, gate_k, n_batch, tpb):
    t, k = x.shape
    n = w.shape[1]
    ncol = n // MM_TN
    nb = n_batch

    def gate_spec(half):
        return pl.BlockSpec(
            (None, 1, MM_TN),
            lambda j, i: (layer * 8 + _row_sel(2 * i + half, tpb, nb), 0, gate_k * ncol + j))

    return pl.pallas_call(
        _mm_res_kernel,
        grid=(ncol, t // MM_TM),
        in_specs=[
            pl.BlockSpec((MM_TM, k), lambda j, i: (i, 0)),
            pl.BlockSpec((k, MM_TN), lambda j, i: (0, j)),
            pl.BlockSpec((MM_TM, MM_TN), lambda j, i: (i, j)),
            gate_spec(0),
            gate_spec(1),
        ],
        out_specs=pl.BlockSpec((MM_TM, MM_TN), lambda j, i: (i, j)),
        out_shape=jax.ShapeDtypeStruct((t, n), F32),
        scratch_shapes=[pltpu.VMEM((k, MM_TN), BF16)],
        input_output_aliases={2: 0},
        compiler_params=_cparams(("arbitrary", "arbitrary")),
        name="matmul_residual",
    )(x, w, xs, mods3, mods3)


def _na_kernel(q_ref, k_ref, v_ref, tab_ref, o_ref, sl_ref, sc_ref, pl_ref, pc_ref, li_ref, *, rows):
    nq = NA_GROUP * GRID_W
    nk = NA_KROWS * GRID_W
    half_win = NA_WIN_ROWS // 2
    left = lax.broadcasted_iota(jnp.int32, (1, LANES), 1) < GRID_W

    s = _dot_nt(q_ref[0:CTX_LEN, :], k_ref[0:CTX_LEN, :])
    p = jnp.exp2(s - jnp.max(s, axis=-1, keepdims=True))
    l = jnp.sum(p, axis=-1, keepdims=True)
    o_ref[0:CTX_LEN, :] = (_dot(p.astype(BF16), v_ref[0:CTX_LEN, :]) / l).astype(o_ref.dtype)

    def geometry(g):
        r0 = jnp.asarray(g, jnp.int32) * NA_GROUP
        start = jnp.clip(r0 - half_win, 0, rows - NA_KROWS)
        q0 = pl.multiple_of(CTX_LEN + r0 * GRID_W, nq)
        k0 = pl.multiple_of(CTX_LEN + start * GRID_W, GRID_W)
        return r0, start, q0, k0

    def stage_a(g, par):
        _, _, q0, k0 = geometry(g)
        q = q_ref[pl.ds(q0, nq), :]
        sl_ref[par] = _dot_nt(q, k_ref[pl.ds(k0, nk), :])
        sc_ref[par] = _dot_nt(q, k_ref[0:CTX_LEN, :])

    def stage_b(g, par):
        r0, start, _, _ = geometry(g)
        for i in range(NA_GROUP):
            r = r0 + i
            rs = jnp.clip(r - half_win, 0, rows - NA_WIN_ROWS)
            rsl = slice(i * GRID_W, (i + 1) * GRID_W)
            tiles = []
            for jp in range(NA_KROWS // 2):
                kr = start + 2 * jp
                ok0 = ((kr >= rs) & (kr < rs + NA_WIN_ROWS)).astype(jnp.int32)
                ok1 = ((kr + 1 >= rs) & (kr + 1 < rs + NA_WIN_ROWS)).astype(jnp.int32)
                e = jnp.clip(kr - r + (NA_WIN_ROWS - 1) + 2, 0, 2 * NA_WIN_ROWS + 1)
                ok = jnp.where(left, ok0, ok1) > 0
                tiles.append(jnp.where(ok, tab_ref[e], NEG_BIG))
            sl = sl_ref[par, rsl, :] + jnp.concatenate(tiles, axis=1)
            sc = sc_ref[par, rsl, :]
            m = jnp.maximum(jnp.max(sl, axis=-1, keepdims=True), jnp.max(sc, axis=-1, keepdims=True))
            pl_i = jnp.exp2(sl - m)
            pc_i = jnp.exp2(sc - m)
            l_i = jnp.sum(pl_i, axis=-1, keepdims=True) + jnp.sum(pc_i, axis=-1, keepdims=True)
            pl_ref[par, rsl, :] = pl_i.astype(BF16)
            pc_ref[par, rsl, :] = pc_i.astype(BF16)
            li_ref[par, rsl, :] = jnp.broadcast_to(1.0 / l_i, (GRID_W, LANES))

    def stage_c(g, par):
        _, _, q0, k0 = geometry(g)
        o = _dot(pl_ref[par], v_ref[pl.ds(k0, nk), :]) + _dot(pc_ref[par], v_ref[0:CTX_LEN, :])
        o_ref[pl.ds(q0, nq), :] = (o * li_ref[par]).astype(o_ref.dtype)

    n_groups = rows // NA_GROUP
    stage_a(0, 0)
    stage_a(1, 1)
    stage_b(0, 0)

    def pair(u, carry):
        t = 2 + 2 * u
        stage_a(t, 0)
        stage_b(t - 1, 1)
        stage_c(t - 2, 0)
        stage_a(t + 1, 1)
        stage_b(t, 0)
        stage_c(t - 1, 1)
        return carry

    lax.fori_loop(0, (n_groups - 2) // 2, pair, 0)
    stage_b(n_groups - 1, 1)
    stage_c(n_groups - 2, 0)
    stage_c(n_groups - 1, 1)


def _na_bias_table(rpb):
    h = rpb.shape[0]
    qc = np.arange(GRID_W)[:, None]
    kc = np.arange(GRID_W)[None, :]
    cs = np.clip(qc - NA_WIN_COLS // 2, 0, GRID_W - NA_WIN_COLS)
    col_ok = (kc >= cs) & (kc < cs + NA_WIN_COLS)
    dc = np.clip(kc - qc + (NA_WIN_COLS - 1), 0, 2 * NA_WIN_COLS - 2)
    a = jnp.take(rpb.astype(F32), jnp.asarray(dc.reshape(-1)), axis=2)
    a = a.reshape(h, 2 * NA_WIN_ROWS - 1, GRID_W, GRID_W) * LOG2E
    a = jnp.where(jnp.asarray(col_ok)[None, None], a, NEG_BIG)
    a = jnp.pad(a, ((0, 0), (2, 2), (0, 0), (0, 0)), constant_values=NEG_BIG)
    return jnp.concatenate([a[:, :-1], a[:, 1:]], axis=-1)


def _na_attention(qkv, rpb, n_batch, tb):
    t = qkv.shape[0]
    rows = (tb - CTX_LEN) // GRID_W
    assert rows % (2 * NA_GROUP) == 0 and rows >= NA_KROWS
    nq, nk = NA_GROUP * GRID_W, NA_KROWS * GRID_W
    table = _na_bias_table(rpb)
    hh = NA_HEADS
    return pl.pallas_call(
        functools.partial(_na_kernel, rows=rows),
        grid=(hh, n_batch),
        in_specs=[
            pl.BlockSpec((tb, LANES), lambda h, b: (b, h)),
            pl.BlockSpec((tb, LANES), lambda h, b: (b, hh + h)),
            pl.BlockSpec((tb, LANES), lambda h, b: (b, 2 * hh + h)),
            pl.BlockSpec((None, 2 * NA_WIN_ROWS + 2, GRID_W, LANES), lambda h, b: (h, 0, 0, 0)),
        ],
        out_specs=pl.BlockSpec((tb, LANES), lambda h, b: (b, h)),
        out_shape=jax.ShapeDtypeStruct((t, D_MODEL), BF16),
        scratch_shapes=[
            pltpu.VMEM((2, nq, nk), F32),
            pltpu.VMEM((2, nq, CTX_LEN), F32),
            pltpu.VMEM((2, nq, nk), BF16),
            pltpu.VMEM((2, nq, CTX_LEN), BF16),
            pltpu.VMEM((2, nq, LANES), F32),
        ],
        compiler_params=_cparams(("arbitrary", "arbitrary")),
        name="na_attention",
    )(qkv, qkv, qkv, table)


def _da_kernel(lam_ref, g_ref, q_ref, k_ref, v_ref, o_ref, acc_ref, m_ref, l_ref,
               s0_ref, s1_ref, p0_ref, p1_ref, a0_ref, a1_ref, qbd_ref, *, lam_init, n_kv_chunks):
    d = DIFF_HEAD_DIM
    i = pl.program_id(2)
    lam = lam_ref[...]
    lam_full = (jnp.exp(jnp.sum(lam[0:1] * lam[1:2], axis=-1, keepdims=True))
                - jnp.exp(jnp.sum(lam[2:3] * lam[3:4], axis=-1, keepdims=True)) + lam_init)
    s_bufs, p_bufs, a_bufs = (s0_ref, s1_ref), (p0_ref, p1_ref), (a0_ref, a1_ref)

    def tile(q0, nq, n_lat_chunks):
        acc_ref[:, 0:nq, :] = jnp.zeros((2, nq, 2 * d), F32)
        m_ref[:, 0:nq, :] = jnp.full((2, nq, LANES), NEG_BIG, F32)
        l_ref[:, 0:nq, :] = jnp.zeros((2, nq, LANES), F32)

        def key_rows(e):
            if isinstance(e, int):
                return (0, CTX_LEN) if e == 0 else (CTX_LEN + (e - 1) * DA_TK, DA_TK)
            return pl.multiple_of(CTX_LEN + (e - 1) * DA_TK, ROW_TILE), DA_TK

        q = q_ref[pl.ds(q0, nq), :]
        zero = jnp.zeros((nq, d), q.dtype)
        qbd_ref[0:nq, :] = jnp.concatenate([q[:, :d], zero], axis=1)
        qbd_ref[nq:2 * nq, :] = jnp.concatenate([zero, q[:, d:]], axis=1)

        def stage_a(e, par):
            k0, nk = key_rows(e)
            s = _dot_nt(qbd_ref[0:2 * nq, :], k_ref[pl.ds(k0, nk), :])
            for c in range(2):
                s_bufs[par][c, 0:nq, 0:nk] = s[c * nq:(c + 1) * nq, :]

        def stage_b(e, par):
            _, nk = key_rows(e)
            s_ref, p_ref, a_ref = s_bufs[par], p_bufs[par], a_bufs[par]
            for c in range(2):
                for sb in range(nq // DA_SUB):
                    rows = slice(sb * DA_SUB, (sb + 1) * DA_SUB)
                    blocks = [s_ref[c, rows, j * LANES:(j + 1) * LANES] for j in range(nk // LANES)]
                    m_prev = m_ref[c, rows, :]
                    m_new = jnp.maximum(
                        m_prev, jnp.max(functools.reduce(jnp.maximum, blocks), axis=-1, keepdims=True))
                    alpha = jnp.exp2(m_prev - m_new)
                    p = [jnp.exp2(blk - m_new) for blk in blocks]
                    l_ref[c, rows, :] = alpha * l_ref[c, rows, :] + functools.reduce(jnp.add, p)
                    for j, pj in enumerate(p):
                        p_ref[c, rows, j * LANES:(j + 1) * LANES] = pj.astype(BF16)
                    a_ref[c, rows, :] = alpha
                    m_ref[c, rows, :] = m_new

        def stage_c(e, par):
            k0, nk = key_rows(e)
            v = v_ref[pl.ds(k0, nk), :]
            p = jnp.concatenate([p_bufs[par][0, 0:nq, 0:nk], p_bufs[par][1, 0:nq, 0:nk]], axis=0)
            pv_all = _dot(p, v)
            for c in range(2):
                pv = pv_all[c * nq:(c + 1) * nq, :]
                alpha = a_bufs[par][c, 0:nq, :]
                acc_ref[c, 0:nq, 0:LANES] = alpha * acc_ref[c, 0:nq, 0:LANES] + pv[:, :LANES]
                acc_ref[c, 0:nq, LANES:] = alpha * acc_ref[c, 0:nq, LANES:] + pv[:, LANES:]

        n = n_lat_chunks
        if n == 0:
            stage_a(0, 0)
            stage_b(0, 0)
            stage_c(0, 0)
        else:
            assert n % 2 == 0
            stage_a(0, 0)
            stage_a(1, 1)
            stage_b(0, 0)
            stage_a(2, 0)
            stage_b(1, 1)
            stage_c(0, 0)

            def pair(u, carry):
                t = 3 + 2 * u
                stage_a(t, 1)
                stage_b(t - 1, 0)
                stage_c(t - 2, 1)
                stage_a(t + 1, 0)
                stage_b(t, 1)
                stage_c(t - 1, 0)
                return carry

            lax.fori_loop(0, (n - 2) // 2, pair, 0)
            stage_b(n, 0)
            stage_c(n - 1, 1)
            stage_c(n, 0)

        r0 = 1.0 / jnp.sum(l_ref[0, 0:nq, :], axis=-1, keepdims=True)
        r1 = lam_full / jnp.sum(l_ref[1, 0:nq, :], axis=-1, keepdims=True)
        o_a = acc_ref[0, 0:nq, 0:LANES] * r0 - acc_ref[1, 0:nq, 0:LANES] * r1
        o_b = acc_ref[0, 0:nq, LANES:] * r0 - acc_ref[1, 0:nq, LANES:] * r1
        ms = (jnp.sum(o_a * o_a, axis=-1, keepdims=True) + jnp.sum(o_b * o_b, axis=-1, keepdims=True)) / (2 * d)
        inv = lax.rsqrt(ms + NORM_EPS) * (1.0 - lam_init)
        g = g_ref[...]
        o_ref[pl.ds(q0, nq), 0:LANES] = (o_a * inv * g[:, :LANES]).astype(o_ref.dtype)
        o_ref[pl.ds(q0, nq), LANES:] = (o_b * inv * g[:, LANES:]).astype(o_ref.dtype)

    @pl.when(i == 0)
    def _():
        tile(0, CTX_LEN, 0)

    @pl.when(i > 0)
    def _():
        tile(pl.multiple_of(CTX_LEN + (i - 1) * DA_TQ, ROW_TILE), DA_TQ, n_kv_chunks)


def _diff_attention(qkv, lam, subln_g, layer_idx, n_batch, tb):
    t = qkv.shape[0]
    hh = DIFF_HEADS
    w = 2 * DIFF_HEAD_DIM
    n_lat = (tb - CTX_LEN) // DA_TQ
    lam_init = 0.8 - 0.6 * math.exp(-0.3 * layer_idx)
    return pl.pallas_call(
        functools.partial(_da_kernel, lam_init=lam_init, n_kv_chunks=(tb - CTX_LEN) // DA_TK),
        grid=(n_batch, hh, 1 + n_lat),
        in_specs=[
            pl.BlockSpec((4, DIFF_HEAD_DIM), lambda b, h, i: (0, 0)),
            pl.BlockSpec((1, w), lambda b, h, i: (0, 0)),
            pl.BlockSpec((tb, w), lambda b, h, i: (b, h)),
            pl.BlockSpec((tb, w), lambda b, h, i: (b, hh + h)),
            pl.BlockSpec((tb, w), lambda b, h, i: (b, 2 * hh + h)),
        ],
        out_specs=pl.BlockSpec((tb, w), lambda b, h, i: (b, h)),
        out_shape=jax.ShapeDtypeStruct((t, D_MODEL), BF16),
        scratch_shapes=[
            pltpu.VMEM((2, DA_TQ, w), F32),
            pltpu.VMEM((2, DA_TQ, LANES), F32),
            pltpu.VMEM((2, DA_TQ, LANES), F32),
            pltpu.VMEM((2, DA_TQ, DA_TK), F32),
            pltpu.VMEM((2, DA_TQ, DA_TK), F32),
            pltpu.VMEM((2, DA_TQ, DA_TK), BF16),
            pltpu.VMEM((2, DA_TQ, DA_TK), BF16),
            pltpu.VMEM((2, DA_TQ, LANES), F32),
            pltpu.VMEM((2, DA_TQ, LANES), F32),
            pltpu.VMEM((2 * DA_TQ, w), BF16),
        ],
        compiler_params=_cparams(("arbitrary", "arbitrary", "arbitrary")),
        name="diff_attention",
    )(lam.astype(F32), subln_g.astype(F32).reshape(1, w), qkv, qkv, qkv)


def _hg_kernel(q_ref, i_ref, ff_ref, fb_ref, lb_ref, o_ref, acc_ref, stf_ref, stb_ref, xb_ref, xk_ref, xv_ref, *,
               n_groups):
    c = HGRN_CHUNK
    rg = HG_GROUP * c
    dk = HGRN_DK
    acc_ref[...] = jnp.zeros(acc_ref.shape, F32)
    stf_ref[...] = jnp.zeros(stf_ref.shape, F32)
    stb_ref[...] = jnp.zeros(stb_ref.shape, F32)
    row = lax.broadcasted_iota(jnp.int32, (rg, rg), 0)
    col = lax.broadcasted_iota(jnp.int32, (rg, rg), 1)
    same = (row // c) == (col // c)
    lower = same & (row >= col)
    upper = same & (col >= row)
    lower_b = lower.astype(BF16)
    upper_b = upper.astype(BF16)
    chunk_of_row = lax.broadcasted_iota(jnp.int32, (rg, 1), 0) // c
    lb_all = lb_ref[...]

    def per_chunk_rows(b, r):
        return jnp.concatenate(
            [jnp.broadcast_to(b[g * c + r:g * c + r + 1, :], (c, dk)) for g in range(HG_GROUP)], axis=0)

    def intra_exact(q, k, v, bcum, forward):
        xb_ref[...] = bcum
        xk_ref[...] = k
        xv_ref[...] = v.astype(F32)
        rowi = lax.broadcasted_iota(jnp.int32, (c, 1), 0)
        outs = []
        for g in range(HG_GROUP):
            bc_g = bcum[g * c:(g + 1) * c, :]
            q_g = q[g * c:(g + 1) * c, :]

            def column(s, acc, g=g, bc_g=bc_g, q_g=q_g):
                live = (rowi >= s) if forward else (rowi <= s)
                bs = xb_ref[pl.ds(g * c + s, 1), :]
                w = jnp.where(live, jnp.exp(jnp.where(live, bc_g - bs, 0.0)), 0.0) * q_g
                a = jnp.sum(w * xk_ref[pl.ds(g * c + s, 1), :], axis=-1, keepdims=True)
                return acc + a * xv_ref[pl.ds(g * c + s, 1), :]

            outs.append(lax.fori_loop(0, c, column, jnp.zeros((c, dk), F32)))
        return jnp.concatenate(outs, axis=0)

    def group(hd, gidx, f_ref, lb_row, mask, mask_b, st_ref, end_row, order, exact):
        r0 = pl.multiple_of(gidx * rg, rg)
        cols = slice(hd * dk, (hd + 1) * dk)
        lb = lb_all[lb_row:lb_row + 1, cols]
        qr = q_ref[pl.ds(r0, rg), cols].astype(F32)
        q = qr * jax.nn.sigmoid(qr) * (dk ** -0.5)
        v = i_ref[pl.ds(r0, rg), cols]
        fr = f_ref[pl.ds(r0, rg), cols].astype(F32)
        fg = lb + (1.0 - lb) * jax.nn.sigmoid(fr)
        k = 1.0 - fg
        lf = jnp.log(fg)
        hi = lf.astype(BF16)
        rem = lf - hi.astype(F32)
        mid = rem.astype(BF16)
        lo = (rem - mid.astype(F32)).astype(BF16)
        cs = _dot(mask_b, jnp.concatenate([hi, mid, lo], axis=1))
        bcum = cs[:, 0:dk] + cs[:, dk:2 * dk] + cs[:, 2 * dk:]
        b_end = per_chunk_rows(bcum, end_row)
        if exact:
            o_intra = intra_exact(q, k, v, bcum, end_row != 0)
        else:
            b_mid = per_chunk_rows(bcum, c // 2)
            qt = (q * jnp.exp(bcum - b_mid)).astype(BF16)
            kt = (k * jnp.exp(b_mid - bcum)).astype(BF16)
            a = jnp.where(mask, _dot_nt(qt, kt), 0.0).astype(BF16)
            o_intra = _dot(a, v)
        k_end = k * jnp.exp(b_end - bcum)
        k_exp = jnp.concatenate(
            [jnp.where(chunk_of_row == g, k_end, 0.0) for g in range(HG_GROUP)], axis=1).astype(BF16)
        u = _dot_tn(v, k_exp)
        st = st_ref[hd]
        st_in = [None] * HG_GROUP
        for g in order:
            st_in[g] = st.astype(BF16)
            st = st * jnp.exp(bcum[g * c + end_row:g * c + end_row + 1, :]) + u[:, g * dk:(g + 1) * dk]
        st_ref[hd] = st
        oi = _dot_nt((q * jnp.exp(bcum)).astype(BF16), jnp.concatenate(st_in, axis=0))
        o_inter = jnp.concatenate(
            [oi[g * c:(g + 1) * c, g * dk:(g + 1) * dk] for g in range(HG_GROUP)], axis=0)
        acc_ref[pl.ds(r0, rg), cols] += o_intra + o_inter

    fwd_order = list(range(HG_GROUP))

    def scan(exact):
        def body(n, carry):
            nb = jnp.where(n == 0, 0, n_groups - n)
            for hd in range(HG_HEADS):
                group(hd, n, ff_ref, 0, lower, lower_b, stf_ref, c - 1, fwd_order, exact)
                group(hd, nb, fb_ref, 1, upper, upper_b, stb_ref, 0, fwd_order[::-1], exact)
            return carry

        lax.fori_loop(0, n_groups, body, 0)

    def min_forget(f_ref, lb_row):
        lb = lb_all[lb_row:lb_row + 1, :]
        f_min = jnp.min(f_ref[...].astype(F32), axis=0, keepdims=True)
        return jnp.min(lb + (1.0 - lb) * jax.nn.sigmoid(f_min))

    safe = jnp.minimum(min_forget(ff_ref, 0), min_forget(fb_ref, 1)) >= HG_SAFE_FORGET

    @pl.when(safe)
    def _():
        scan(False)

    @pl.when(jnp.logical_not(safe))
    def _():
        scan(True)

    o_ref[...] = acc_ref[...].astype(o_ref.dtype)


def _hgrn_scan(proj, lb2, n_batch, tb):
    t = proj.shape[0]
    hh = HGRN_HEADS // HG_HEADS
    wd = HG_HEADS * LANES
    rg = HG_GROUP * HGRN_CHUNK
    assert CTX_LEN == rg and tb % rg == 0

    def col(k):
        return pl.BlockSpec((tb, wd), lambda b, h: (b, k * hh + h), pipeline_mode=pl.Buffered(1))

    return pl.pallas_call(
        functools.partial(_hg_kernel, n_groups=tb // rg),
        grid=(n_batch, hh),
        in_specs=[col(0), col(1), col(3), col(4), pl.BlockSpec((2, wd), lambda b, h: (0, h))],
        out_specs=pl.BlockSpec((tb, wd), lambda b, h: (b, h)),
        out_shape=jax.ShapeDtypeStruct((t, D_MODEL), BF16),
        scratch_shapes=[pltpu.VMEM((tb, wd), F32),
                        pltpu.VMEM((HG_HEADS, LANES, LANES), F32), pltpu.VMEM((HG_HEADS, LANES, LANES), F32),
                        pltpu.VMEM((rg, LANES), F32), pltpu.VMEM((rg, LANES), F32), pltpu.VMEM((rg, LANES), F32)],
        compiler_params=_cparams(("arbitrary", "arbitrary")),
        name="hgrn_scan",
    )(proj, proj, proj, proj, lb2)


def _hg_finish_kernel(o_ref, g_ref, ng_ref, h_ref):
    o = o_ref[...].astype(F32)
    y = o * lax.rsqrt(jnp.mean(o * o, axis=-1, keepdims=True) + NORM_EPS) * ng_ref[...]
    g = g_ref[...].astype(F32)
    h_ref[...] = (y * (g * jax.nn.sigmoid(g))).astype(h_ref.dtype)


def _hgrn_finish(o, proj, norm_g):
    t, d = o.shape
    return pl.pallas_call(
        _hg_finish_kernel,
        grid=(t // ROW_TILE,),
        in_specs=[
            pl.BlockSpec((ROW_TILE, d), lambda i: (i, 0)),
            pl.BlockSpec((ROW_TILE, d), lambda i: (i, 2)),
            pl.BlockSpec((1, d), lambda i: (0, 0)),
        ],
        out_specs=pl.BlockSpec((ROW_TILE, d), lambda i: (i, 0)),
        out_shape=jax.ShapeDtypeStruct((t, d), BF16),
        compiler_params=_cparams(("arbitrary",)),
        name="hgrn_finish",
    )(o, proj, norm_g.astype(F32).reshape(1, d))


def _moe_kernel(be_ref, first_ref, nxt_ref, nu_ref, tok_ref, x_hbm, wg_hbm, wu_hbm, wd_hbm, o_ref,
                xbuf, sg_ref, su_ref, sd_ref, bg_ref, bu_ref, bd_ref, sem, xsem, *, layer):
    b = pl.program_id(0)
    n_used = nu_ref[0]

    def weight_copies(e):
        return (pltpu.make_async_copy(wg_hbm.at[layer, e], sg_ref, sem.at[0]),
                pltpu.make_async_copy(wu_hbm.at[layer, e], su_ref, sem.at[1]),
                pltpu.make_async_copy(wd_hbm.at[layer, e], sd_ref, sem.at[2]))

    def start_gather(blk, slot):
        base = blk * MOE_TM
        for r in range(MOE_TM):
            pltpu.make_async_copy(x_hbm.at[pl.ds(tok_ref[base + r], 1)], xbuf.at[slot, pl.ds(r, 1)],
                                  xsem.at[slot]).start()

    def wait_gather(slot):
        pltpu.make_async_copy(x_hbm.at[pl.ds(0, MOE_TM)], xbuf.at[slot], xsem.at[slot]).wait()

    @pl.when(b == 0)
    def _():
        for cp in weight_copies(be_ref[0]):
            cp.start()
        start_gather(0, 0)
        start_gather(1, 1)

    @pl.when(first_ref[b] == 1)
    def _():
        for cp in weight_copies(be_ref[b]):
            cp.wait()
        for src, dst in ((sg_ref, bg_ref), (su_ref, bu_ref), (sd_ref, bd_ref)):
            slab = src.shape[0] // MOE_CAST_SLABS

            def cast(r, carry, src=src, dst=dst, slab=slab):
                r0 = pl.multiple_of(r * slab, slab)
                dst[pl.ds(r0, slab), :] = src[pl.ds(r0, slab), :].astype(BF16)
                return carry

            lax.fori_loop(0, MOE_CAST_SLABS, cast, 0)

        @pl.when(nxt_ref[b] >= 0)
        def _():
            for cp in weight_copies(nxt_ref[b]):
                cp.start()

    @pl.when(b < n_used)
    def _():
        slot = b % MOE_XBUFS
        wait_gather(slot)
        xu = xbuf[slot]
        x = jnp.concatenate(
            [lax.bitcast_convert_type(xu << 16, F32).astype(BF16),
             lax.bitcast_convert_type(xu & jnp.uint32(0xFFFF0000), F32).astype(BF16)], axis=1)
        g = _dot(x, bg_ref[...])
        u = _dot(x, bu_ref[...])
        h = (g * jax.nn.sigmoid(g) * u).astype(BF16)
        o_ref[...] = _dot(h, bd_ref[...]).astype(o_ref.dtype)
        start_gather(b + 2, (b + 2) % MOE_XBUFS)

    @pl.when(b >= n_used)
    def _():
        @pl.when(b == n_used)
        def _():
            wait_gather(b % MOE_XBUFS)
            wait_gather((b + 1) % MOE_XBUFS)

        o_ref[...] = jnp.zeros(o_ref.shape, o_ref.dtype)


def _moe_experts(hp, slot_tok, blk_expert, blk_first, blk_next, n_used, w_gate, w_up, w_down, layer):
    d = 2 * hp.shape[1]
    f = w_gate.shape[3]
    n_slots = slot_tok.shape[0]
    n_blocks = n_slots // MOE_TM
    grid_spec = pltpu.PrefetchScalarGridSpec(
        num_scalar_prefetch=5,
        grid=(n_blocks,),
        in_specs=[pl.BlockSpec(memory_space=pl.ANY)] * 4,
        out_specs=pl.BlockSpec((MOE_TM, d), lambda b, *_: (b, 0)),
        scratch_shapes=[
            pltpu.VMEM((MOE_XBUFS, MOE_TM, d // 2), jnp.uint32),
            pltpu.VMEM((d, f), F32), pltpu.VMEM((d, f), F32), pltpu.VMEM((f, d), F32),
            pltpu.VMEM((d, f), BF16), pltpu.VMEM((d, f), BF16), pltpu.VMEM((f, d), BF16),
            pltpu.SemaphoreType.DMA((3,)),
            pltpu.SemaphoreType.DMA((MOE_XBUFS,)),
        ],
    )
    return pl.pallas_call(
        functools.partial(_moe_kernel, layer=layer),
        grid_spec=grid_spec,
        out_shape=jax.ShapeDtypeStruct((n_slots, d), BF16),
        compiler_params=_cparams(("arbitrary",)),
        name="moe_experts",
    )(blk_expert, blk_first, blk_next, n_used, slot_tok, hp, w_gate, w_up, w_down)


def _route(route, counts_row, n_blocks):
    t = route.shape[0]
    e_idx = route[:, 0:TOP_K].astype(jnp.int32)
    rank = route[:, TOP_K:2 * TOP_K].astype(jnp.int32)
    w = route[:, 2 * TOP_K:3 * TOP_K]
    counts = counts_row[0, :N_EXPERTS].astype(jnp.int32)
    pcounts = (counts + MOE_TM - 1) // MOE_TM * MOE_TM
    pends = jnp.cumsum(pcounts)
    pstarts = pends - pcounts
    e_ids = jnp.arange(N_EXPERTS, dtype=jnp.int32)
    start_of = jnp.sum(jnp.where(e_idx[..., None] == e_ids, pstarts, 0), axis=-1)
    dest = (start_of + rank).reshape(-1).astype(jnp.int32)
    flat_t = jnp.arange(t * TOP_K, dtype=jnp.int32) // TOP_K
    slot_tok = jnp.zeros((n_blocks * MOE_TM,), jnp.int32).at[dest].set(flat_t)
    blk_row0 = jnp.arange(n_blocks, dtype=jnp.int32) * MOE_TM
    blk_expert = jnp.minimum(
        jnp.sum((pends[None, :] <= blk_row0[:, None]).astype(jnp.int32), axis=1), N_EXPERTS - 1).astype(jnp.int32)
    n_used = (pends[-1] // MOE_TM).astype(jnp.int32).reshape(1)
    blk_ids = jnp.arange(n_blocks, dtype=jnp.int32)
    prev_expert = jnp.concatenate([jnp.full((1,), -1, jnp.int32), blk_expert[:-1]])
    blk_first = ((blk_ids < n_used[0]) & (blk_expert != prev_expert)).astype(jnp.int32)
    e_ids = jnp.arange(N_EXPERTS, dtype=jnp.int32)
    later_used = (e_ids[None, :] > e_ids[:, None]) & (counts[None, :] > 0)
    next_used = jnp.min(jnp.where(later_used, e_ids[None, :], N_EXPERTS), axis=1)
    next_used = jnp.where(next_used == N_EXPERTS, -1, next_used).astype(jnp.int32)
    blk_next = next_used[blk_expert]
    return w, dest.reshape(t, TOP_K), slot_tok, (blk_expert, blk_first, blk_next, n_used)


def _moe_res_kernel(xs_ref, y0_ref, y1_ref, w_ref, gate_ref, g_ref, shift_ref, scale_ref, o_ref, h_ref):
    w = w_ref[...]
    y = w[:, 0:1] * y0_ref[...].astype(F32) + w[:, 1:2] * y1_ref[...].astype(F32)
    x = xs_ref[...] + gate_ref[...] * y
    o_ref[...] = x
    xn = x * lax.rsqrt(jnp.mean(x * x, axis=-1, keepdims=True) + NORM_EPS)
    h_ref[...] = (xn * g_ref[...] * (1.0 + scale_ref[...]) + shift_ref[...]).astype(h_ref.dtype)


def _moe_res_final_kernel(xs_ref, y0_ref, y1_ref, w_ref, gate_ref, fg_ref, o_ref):
    w = w_ref[...]
    y = w[:, 0:1] * y0_ref[...].astype(F32) + w[:, 1:2] * y1_ref[...].astype(F32)
    x = xs_ref[...] + gate_ref[...] * y
    o_ref[...] = x * lax.rsqrt(jnp.mean(x * x, axis=-1, keepdims=True) + NORM_EPS) * fg_ref[...]


def _moe_residual(xs, y0, y1, w, mods3, g3, layer, n_batch, tpb):
    t, d = xs.shape
    nb = n_batch
    row = lambda i: (i, 0)

    def mod_spec(lyr, k):
        return pl.BlockSpec((None, 1, d), lambda i: (lyr * 8 + _row_sel(i, tpb, nb), 0, k))

    return pl.pallas_call(
        _moe_res_kernel,
        grid=(t // ROW_TILE,),
        in_specs=[
            pl.BlockSpec((ROW_TILE, d), row),
            pl.BlockSpec((ROW_TILE, d), row),
            pl.BlockSpec((ROW_TILE, d), row),
            pl.BlockSpec((ROW_TILE, TOP_K), row),
            mod_spec(layer, 5),
            pl.BlockSpec((None, 1, d), lambda i: ((layer + 1) * 2, 0, 0)),
            mod_spec(layer + 1, 0),
            mod_spec(layer + 1, 1),
        ],
        out_specs=[pl.BlockSpec((ROW_TILE, d), row), pl.BlockSpec((ROW_TILE, d), row)],
        out_shape=[jax.ShapeDtypeStruct((t, d), F32), jax.ShapeDtypeStruct((t, d), BF16)],
        input_output_aliases={0: 0},
        compiler_params=_cparams(("arbitrary",)),
        name="moe_residual",
    )(xs, y0, y1, w, mods3, g3, mods3, mods3)


def _moe_residual_final(xs, y0, y1, w, mods3, layer, final_g, n_batch, tpb):
    t, d = xs.shape
    lat = tpb - CTX_LEN // ROW_TILE
    src = lambda i: ((i // lat) * tpb + CTX_LEN // ROW_TILE + i % lat, 0)
    return pl.pallas_call(
        _moe_res_final_kernel,
        grid=(n_batch * lat,),
        in_specs=[
            pl.BlockSpec((ROW_TILE, d), src),
            pl.BlockSpec((ROW_TILE, d), src),
            pl.BlockSpec((ROW_TILE, d), src),
            pl.BlockSpec((ROW_TILE, TOP_K), src),
            pl.BlockSpec((None, 1, d), lambda i: (layer * 8 + i // lat, 0, 5)),
            pl.BlockSpec((1, d), lambda i: (0, 0)),
        ],
        out_specs=pl.BlockSpec((ROW_TILE, d), lambda i: (i, 0)),
        out_shape=jax.ShapeDtypeStruct((n_batch * lat * ROW_TILE, d), F32),
        compiler_params=_cparams(("arbitrary",)),
        name="moe_residual_final",
    )(xs, y0, y1, w, mods3, final_g.astype(F32).reshape(1, d))


def _rope_tables(n_batch, seq):
    quarter = DIFF_HEAD_DIM // 4
    tpos = jnp.arange(seq)
    rowp = (tpos // GRID_W).astype(F32)
    colp = (tpos % GRID_W).astype(F32)
    inv = ROPE_THETA ** (-jnp.arange(quarter, dtype=F32) / quarter)
    ang_r = rowp[:, None] * inv[None, :]
    ang_c = colp[:, None] * inv[None, :]
    cos = jnp.concatenate([jnp.cos(ang_r)] * 2 + [jnp.cos(ang_c)] * 2, axis=-1)
    sin = jnp.concatenate([-jnp.sin(ang_r), jnp.sin(ang_r), -jnp.sin(ang_c), jnp.sin(ang_c)], axis=-1)
    cos = jnp.concatenate([jnp.ones((CTX_LEN, LANES), F32), cos], axis=0)
    sin = jnp.concatenate([jnp.zeros((CTX_LEN, LANES), F32), sin], axis=0)
    return jnp.tile(cos, (n_batch, 1)), jnp.tile(sin, (n_batch, 1))


def kernel(x, c, ctx, c_ctx, ada_w, ada_b, norm_g, na_w_qkv, na_rpb, na_w_o, da_w_qkv, da_lambda, da_subln_g, da_w_o, hg_w_in, hg_lb, hg_norm_g, hg_w_o, moe_w_router, moe_b_router, moe_w_gate, moe_w_up, moe_w_down, final_g):
    n_batch, seq, d = x.shape
    tb = CTX_LEN + seq
    t = n_batch * tb
    tpb = tb // ROW_TILE
    assert d == D_MODEL and ctx.shape[1] == CTX_LEN and n_batch + 1 <= 8
    assert seq % DA_TQ == 0 and t % MM_TM == 0 and seq // GRID_W >= NA_WIN_ROWS

    xs = jnp.concatenate([ctx, x], axis=1).reshape(t, d).astype(F32)

    c_rows = jnp.zeros((8, d), F32).at[:n_batch].set(c.astype(F32)).at[n_batch].set(c_ctx.astype(F32))
    mods = _ada_mods(c_rows, ada_w.astype(F32), ada_b.astype(F32))
    mods3 = mods.reshape(DEPTH * 8, 1, 6 * d)
    g3 = norm_g.astype(F32).reshape(DEPTH * 2, 1, d)

    lb_p = jax.nn.softmax(hg_lb.astype(F32), axis=1)
    lb_all = jnp.cumsum(lb_p, axis=1) - lb_p[:, :1]

    wr = jnp.zeros((d, LANES), F32).at[:, :N_EXPERTS].set(moe_w_router.astype(F32))
    wr_hi = wr.astype(BF16)
    wr_lo = (wr - wr_hi.astype(F32)).astype(BF16)
    rb = jnp.zeros((1, LANES), F32).at[0, :N_EXPERTS].set(moe_b_router.astype(F32))

    n_blocks = -(-(t * TOP_K) // MOE_TM) + N_EXPERTS + (MOE_XBUFS - 1)
    rope = _rope_tables(n_batch, seq)
    wg = moe_w_gate.astype(F32)
    wu = moe_w_up.astype(F32)
    wd = moe_w_down.astype(F32)
    na_qs = NA_HEAD_DIM ** -0.5 * LOG2E
    da_qs = DIFF_HEAD_DIM ** -0.5 * LOG2E

    counters = [0, 0, 0]
    out = None
    for i in range(DEPTH):
        kind = i % N_MIXERS
        j = counters[kind]
        counters[kind] += 1

        if i == 0:
            h = _norm_mod(xs, mods3, g3, i, 0, n_batch, tpb)[0]
        if kind == 0:
            qkv = _matmul(h, na_w_qkv[j], q_cols=d, q_scale=na_qs)
            o = _na_attention(qkv, na_rpb[j], n_batch, tb)
            w_o = na_w_o[j]
        elif kind == 1:
            qkv = _matmul(h, da_w_qkv[j], q_cols=d, q_scale=da_qs, rope=rope, n_rope_cols=2 * d)
            o = _diff_attention(qkv, da_lambda[j], da_subln_g[j], i, n_batch, tb)
            w_o = da_w_o[j]
        else:
            proj = _matmul(h, hg_w_in[j])
            o32 = _hgrn_scan(proj, lb_all[:, i, :], n_batch, tb)
            o = _hgrn_finish(o32, proj, hg_norm_g[j])
            w_o = hg_w_o[j]
        xs = _matmul_residual(o, w_o, xs, mods3, i, 2, n_batch, tpb)

        hp, route, counts = _norm_mod(xs, mods3, g3, i, 1, n_batch, tpb, router=(wr_hi, wr_lo, rb))
        w, dest, slot_tok, blk_info = _route(route, counts, n_blocks)
        yb = _moe_experts(hp, slot_tok, *blk_info, wg, wu, wd, i)
        y0 = yb.at[dest[:, 0]].get(mode="promise_in_bounds")
        y1 = yb.at[dest[:, 1]].get(mode="promise_in_bounds")
        if i < DEPTH - 1:
            xs, h = _moe_residual(xs, y0, y1, w, mods3, g3, i, n_batch, tpb)
        else:
            out = _moe_residual_final(xs, y0, y1, w, mods3, i, final_g, n_batch, tpb)
    return out.reshape(n_batch, seq, d).astype(x.dtype)
```

```python
import functools
import math

import numpy as np
import jax
import jax.numpy as jnp
from jax import lax
from jax.experimental import pallas as pl
from jax.experimental.pallas import tpu as pltpu

F32 = jnp.float32
BF16 = jnp.bfloat16

D_MODEL = 2048
DEPTH = 4
GRID_W = 64
CTX_LEN = 256
N_MIXERS = 3
NORM_EPS = 1e-6
ROPE_THETA = 10000.0

NA_HEADS = 16
NA_HEAD_DIM = 128
NA_WIN_ROWS = 8
NA_WIN_COLS = 16

DIFF_HEADS = 8
DIFF_HEAD_DIM = 128

HGRN_HEADS = 16
HGRN_DK = 128
HGRN_CHUNK = 64

N_EXPERTS = 16
N_GROUPS = 4
EXPERTS_PER_GROUP = 4
TOP_K = 2
D_EXPERT = 1024

LANES = 128
ROW_TILE = 256
MM_TM = 512
MM_TN = 1024
MOE_TM = 256
DA_TQ = 512
VMEM_LIMIT = 56 * 1024 * 1024

NA_GROUP = 4
NA_KROWS = 12
DA_SUB = 64
DA_TK = 512
HG_GROUP = 4
HG_HEADS = 1
HG_SAFE_FORGET = math.exp(-2.0)
MOE_XBUFS = 3
MOE_CAST_SLABS = 8

NEG_BIG = -1e30
LOG2E = 1.4426950408889634


def _cparams(sem):
    return pltpu.CompilerParams(dimension_semantics=sem, vmem_limit_bytes=VMEM_LIMIT)


def _dot(a, b):
    return jnp.dot(a, b, preferred_element_type=F32)


def _dot_nt(a, b):
    return lax.dot_general(a, b, (((1,), (1,)), ((), ())), preferred_element_type=F32)


def _dot_tn(a, b):
    return lax.dot_general(a, b, (((0,), (0,)), ((), ())), preferred_element_type=F32)


def _row_sel(sub, tiles_per_batch, n_batch):
    return jnp.where(sub % tiles_per_batch == 0, n_batch, sub // tiles_per_batch)


def _ada_kernel(c_ref, w_ref, b_ref, o_ref):
    c = c_ref[...]
    a = c * jax.nn.sigmoid(c)
    o_ref[...] = jnp.dot(a, w_ref[...], preferred_element_type=F32,
                         precision=lax.Precision.HIGHEST) + b_ref[...]


def _ada_mods(c_rows, ada_w, ada_b):
    depth, d, n = ada_w.shape
    tn = 1024
    return pl.pallas_call(
        _ada_kernel,
        grid=(depth, n // tn),
        in_specs=[
            pl.BlockSpec((8, d), lambda l, j: (0, 0)),
            pl.BlockSpec((None, d, tn), lambda l, j: (l, 0, j)),
            pl.BlockSpec((None, 1, tn), lambda l, j: (l, 0, j)),
        ],
        out_specs=pl.BlockSpec((None, 8, tn), lambda l, j: (l, 0, j)),
        out_shape=jax.ShapeDtypeStruct((depth, 8, n), F32),
        compiler_params=_cparams(("arbitrary", "arbitrary")),
        name="ada_mods",
    )(c_rows, ada_w, ada_b.reshape(depth, 1, n))


def _norm_mod_kernel(x_ref, g_ref, shift_ref, scale_ref, h_ref):
    x = x_ref[...]
    y = x * lax.rsqrt(jnp.mean(x * x, axis=-1, keepdims=True) + NORM_EPS)
    h = y * g_ref[...] * (1.0 + scale_ref[...]) + shift_ref[...]
    h_ref[...] = h.astype(h_ref.dtype)


def _route_tile(lg, bias, run_ref):
    rows = lg.shape[0]
    lane = lax.broadcasted_iota(jnp.int32, (1, LANES), 1)
    lanef = lane.astype(F32)
    neg = -jnp.inf
    far = float(LANES)
    rmax = lambda a: jnp.max(a, axis=-1, keepdims=True)
    rmin = lambda a: jnp.min(a, axis=-1, keepdims=True)
    rsum = lambda a: jnp.sum(a, axis=-1, keepdims=True)

    s = jax.nn.sigmoid(lg)
    sel = jnp.where(lane < N_EXPERTS, s + bias, neg)
    best = None
    for g in range(N_GROUPS):
        ing = (lane // EXPERTS_PER_GROUP) == g
        v = jnp.where(ing, sel, neg)
        m1 = rmax(v)
        i1 = rmin(jnp.where(v == m1, lanef, far))
        v2 = jnp.where(lanef == i1, neg, v)
        m2 = rmax(v2)
        i2 = rmin(jnp.where(v2 == m2, lanef, far))
        cand = (m1 + m2, i1, i2)
        if best is None:
            best = cand
        else:
            better = cand[0] > best[0]
            best = tuple(jnp.where(better, c, b) for c, b in zip(cand, best))
    _, e1, e2 = best
    is1 = lanef == e1
    is2 = lanef == e2
    s1 = rsum(jnp.where(is1, s, 0.0))
    s2 = rsum(jnp.where(is2, s, 0.0))
    w1 = s1 / (s1 + s2)
    w2 = s2 / (s1 + s2)
    onehot = jnp.where(is1 | is2, 1.0, 0.0)
    row = lax.broadcasted_iota(jnp.int32, (rows, rows), 0)
    col = lax.broadcasted_iota(jnp.int32, (rows, rows), 1)
    before = jnp.where(row > col, 1.0, 0.0).astype(BF16)
    prior = _dot(before, onehot.astype(BF16)) + run_ref[...]
    r1 = rsum(jnp.where(is1, prior, 0.0))
    r2 = rsum(jnp.where(is2, prior, 0.0))
    run_ref[...] += jnp.sum(onehot, axis=0, keepdims=True)
    out = jnp.zeros((rows, LANES), F32)
    for k, val in enumerate((e1, e2, r1, r2, w1, w2)):
        out = jnp.where(lane == k, val, out)
    return out


def _norm_mod_router_kernel(x_ref, g_ref, shift_ref, scale_ref, whi_ref, wlo_ref, rb_ref,
                            h_ref, route_ref, cnt_ref, run_ref):
    @pl.when(pl.program_id(0) == 0)
    def _():
        run_ref[...] = jnp.zeros(run_ref.shape, F32)

    x = x_ref[...]
    y = x * lax.rsqrt(jnp.mean(x * x, axis=-1, keepdims=True) + NORM_EPS)
    h = y * g_ref[...] * (1.0 + scale_ref[...]) + shift_ref[...]
    h_hi = h.astype(BF16)
    h_lo = (h - h_hi.astype(F32)).astype(BF16)
    bits = lax.bitcast_convert_type(h_hi.astype(F32), jnp.uint32)
    half = bits.shape[1] // 2
    h_ref[...] = (bits[:, :half] >> 16) | bits[:, half:]
    lg = _dot(h_hi, whi_ref[...]) + _dot(h_hi, wlo_ref[...]) + _dot(h_lo, whi_ref[...])
    route_ref[...] = _route_tile(lg, rb_ref[...], run_ref)
    cnt_ref[...] = run_ref[...]


def _norm_mod(xs, mods3, g3, layer, which, n_batch, tpb, router=None):
    t, d = xs.shape
    shift_k, scale_k = (0, 1) if which == 0 else (3, 4)
    nb = n_batch

    def mod_spec(k):
        return pl.BlockSpec((None, 1, d), lambda i: (layer * 8 + _row_sel(i, tpb, nb), 0, k))

    in_specs = [
        pl.BlockSpec((ROW_TILE, d), lambda i: (i, 0)),
        pl.BlockSpec((None, 1, d), lambda i: (layer * 2 + which, 0, 0)),
        mod_spec(shift_k),
        mod_spec(scale_k),
    ]
    args = [xs, g3, mods3, mods3]
    out_specs = [pl.BlockSpec((ROW_TILE, d), lambda i: (i, 0))]
    out_shape = [jax.ShapeDtypeStruct((t, d), BF16)]
    kern = _norm_mod_kernel
    scratch = []
    if router is not None:
        in_specs += [pl.BlockSpec((d, LANES), lambda i: (0, 0))] * 2 + [pl.BlockSpec((1, LANES), lambda i: (0, 0))]
        args += list(router)
        out_specs = [pl.BlockSpec((ROW_TILE, d // 2), lambda i: (i, 0)),
                     pl.BlockSpec((ROW_TILE, LANES), lambda i: (i, 0)),
                     pl.BlockSpec((1, LANES), lambda i: (0, 0))]
        out_shape = [jax.ShapeDtypeStruct((t, d // 2), jnp.uint32),
                     jax.ShapeDtypeStruct((t, LANES), F32),
                     jax.ShapeDtypeStruct((1, LANES), F32)]
        scratch = [pltpu.VMEM((1, LANES), F32)]
        kern = _norm_mod_router_kernel
    return pl.pallas_call(
        kern,
        grid=(t // ROW_TILE,),
        in_specs=in_specs,
        out_specs=out_specs,
        out_shape=out_shape,
        scratch_shapes=scratch,
        compiler_params=_cparams(("arbitrary",)),
        name="norm_mod",
    )(*args)


def _rope_store(acc, cos_ref, sin_ref, o_ref):
    lane = lax.broadcasted_iota(jnp.int32, (1, LANES), 1)
    first = (lane % 64) < 32
    cos = cos_ref[...]
    sin = sin_ref[...]
    for c in range(acc.shape[1] // LANES):
        xc = acc[:, c * LANES:(c + 1) * LANES]
        sw = jnp.where(first, pltpu.roll(xc, 96, 1), pltpu.roll(xc, 32, 1))
        o_ref[:, c * LANES:(c + 1) * LANES] = (xc * cos + sw * sin).astype(o_ref.dtype)


def _mm_kernel(x_ref, w_ref, o_ref, wb_ref, *, n_q, q_scale):
    @pl.when(pl.program_id(1) == 0)
    def _():
        wb_ref[...] = w_ref[...].astype(BF16)

    acc = _dot(x_ref[...], wb_ref[...])
    j = pl.program_id(0)

    @pl.when(j < n_q)
    def _():
        o_ref[...] = (acc * q_scale).astype(o_ref.dtype)

    @pl.when(j >= n_q)
    def _():
        o_ref[...] = acc.astype(o_ref.dtype)


def _mm_rope_kernel(x_ref, w_ref, cos_ref, sin_ref, o_ref, wb_ref, *, n_q, q_scale, n_rope):
    @pl.when(pl.program_id(1) == 0)
    def _():
        wb_ref[...] = w_ref[...].astype(BF16)

    acc = _dot(x_ref[...], wb_ref[...])
    j = pl.program_id(0)

    @pl.when(j < n_q)
    def _():
        _rope_store(acc * q_scale, cos_ref, sin_ref, o_ref)

    @pl.when((j >= n_q) & (j < n_rope))
    def _():
        _rope_store(acc, cos_ref, sin_ref, o_ref)

    @pl.when(j >= n_rope)
    def _():
        o_ref[...] = acc.astype(o_ref.dtype)


def _mm_res_kernel(x_ref, w_ref, xs_ref, ga_ref, gb_ref, o_ref, wb_ref):
    @pl.when(pl.program_id(1) == 0)
    def _():
        wb_ref[...] = w_ref[...].astype(BF16)

    acc = _dot(x_ref[...], wb_ref[...])
    half = ROW_TILE
    o_ref[:half, :] = xs_ref[:half, :] + ga_ref[...] * acc[:half, :]
    o_ref[half:, :] = xs_ref[half:, :] + gb_ref[...] * acc[half:, :]


def _matmul(x, w, q_cols=0, q_scale=1.0, rope=None, n_rope_cols=0):
    t, k = x.shape
    n = w.shape[1]
    in_specs = [
        pl.BlockSpec((MM_TM, k), lambda j, i: (i, 0)),
        pl.BlockSpec((k, MM_TN), lambda j, i: (0, j)),
    ]
    args = [x, w]
    kern = functools.partial(_mm_kernel, n_q=q_cols // MM_TN, q_scale=q_scale)
    if rope is not None:
        in_specs += [pl.BlockSpec((MM_TM, LANES), lambda j, i: (i, 0))] * 2
        args += list(rope)
        kern = functools.partial(_mm_rope_kernel, n_q=q_cols // MM_TN, q_scale=q_scale,
                                 n_rope=n_rope_cols // MM_TN)
    return pl.pallas_call(
        kern,
        grid=(n // MM_TN, t // MM_TM),
        in_specs=in_specs,
        out_specs=pl.BlockSpec((MM_TM, MM_TN), lambda j, i: (i, j)),
        out_shape=jax.ShapeDtypeStruct((t, n), BF16),
        scratch_shapes=[pltpu.VMEM((k, MM_TN), BF16)],
        compiler_params=_cparams(("arbitrary", "arbitrary")),
        name="matmul",
    )(*args)


def _matmul_residual(x, w, xs, mods3, layer, gate_k, n_batch, tpb):
    t, k = x.shape
    n = w.shape[1]
    ncol = n // MM_TN
    nb = n_batch

    def gate_spec(half):
        return pl.BlockSpec(
            (None, 1, MM_TN),
            lambda j, i: (layer * 8 + _row_sel(2 * i + half, tpb, nb), 0, gate_k * ncol + j))

    return pl.pallas_call(
        _mm_res_kernel,
        grid=(ncol, t // MM_TM),
        in_specs=[
            pl.BlockSpec((MM_TM, k), lambda j, i: (i, 0)),
            pl.BlockSpec((k, MM_TN), lambda j, i: (0, j)),
            pl.BlockSpec((MM_TM, MM_TN), lambda j, i: (i, j)),
            gate_spec(0),
            gate_spec(1),
        ],
        out_specs=pl.BlockSpec((MM_TM, MM_TN), lambda j, i: (i, j)),
        out_shape=jax.ShapeDtypeStruct((t, n), F32),
        scratch_shapes=[pltpu.VMEM((k, MM_TN), BF16)],
        input_output_aliases={2: 0},
        compiler_params=_cparams(("arbitrary", "arbitrary")),
        name="matmul_residual",
    )(x, w, xs, mods3, mods3)


def _na_kernel(q_ref, k_ref, v_ref, tab_ref, o_ref, sl_ref, sc_ref, pl_ref, pc_ref, li_ref, *, rows):
    nq = NA_GROUP * GRID_W
    nk = NA_KROWS * GRID_W
    half_win = NA_WIN_ROWS // 2
    left = lax.broadcasted_iota(jnp.int32, (1, LANES), 1) < GRID_W

    s = _dot_nt(q_ref[0:CTX_LEN, :], k_ref[0:CTX_LEN, :])
    p = jnp.exp2(s - jnp.max(s, axis=-1, keepdims=True))
    l = jnp.sum(p, axis=-1, keepdims=True)
    o_ref[0:CTX_LEN, :] = (_dot(p.astype(BF16), v_ref[0:CTX_LEN, :]) / l).astype(o_ref.dtype)

    def geometry(g):
        r0 = jnp.asarray(g, jnp.int32) * NA_GROUP
        start = jnp.clip(r0 - half_win, 0, rows - NA_KROWS)
        q0 = pl.multiple_of(CTX_LEN + r0 * GRID_W, nq)
        k0 = pl.multiple_of(CTX_LEN + start * GRID_W, GRID_W)
        return r0, start, q0, k0

    def stage_a(g, par):
        _, _, q0, k0 = geometry(g)
        q = q_ref[pl.ds(q0, nq), :]
        sl_ref[par] = _dot_nt(q, k_ref[pl.ds(k0, nk), :])
        sc_ref[par] = _dot_nt(q, k_ref[0:CTX_LEN, :])

    def stage_b(g, par):
        r0, start, _, _ = geometry(g)
        for i in range(NA_GROUP):
            r = r0 + i
            rs = jnp.clip(r - half_win, 0, rows - NA_WIN_ROWS)
            rsl = slice(i * GRID_W, (i + 1) * GRID_W)
            tiles = []
            for jp in range(NA_KROWS // 2):
                kr = start + 2 * jp
                ok0 = ((kr >= rs) & (kr < rs + NA_WIN_ROWS)).astype(jnp.int32)
                ok1 = ((kr + 1 >= rs) & (kr + 1 < rs + NA_WIN_ROWS)).astype(jnp.int32)
                e = jnp.clip(kr - r + (NA_WIN_ROWS - 1) + 2, 0, 2 * NA_WIN_ROWS + 1)
                ok = jnp.where(left, ok0, ok1) > 0
                tiles.append(jnp.where(ok, tab_ref[e], NEG_BIG))
            sl = sl_ref[par, rsl, :] + jnp.concatenate(tiles, axis=1)
            sc = sc_ref[par, rsl, :]
            m = jnp.maximum(jnp.max(sl, axis=-1, keepdims=True), jnp.max(sc, axis=-1, keepdims=True))
            pl_i = jnp.exp2(sl - m)
            pc_i = jnp.exp2(sc - m)
            l_i = jnp.sum(pl_i, axis=-1, keepdims=True) + jnp.sum(pc_i, axis=-1, keepdims=True)
            pl_ref[par, rsl, :] = pl_i.astype(BF16)
            pc_ref[par, rsl, :] = pc_i.astype(BF16)
            li_ref[par, rsl, :] = jnp.broadcast_to(1.0 / l_i, (GRID_W, LANES))

    def stage_c(g, par):
        _, _, q0, k0 = geometry(g)
        o = _dot(pl_ref[par], v_ref[pl.ds(k0, nk), :]) + _dot(pc_ref[par], v_ref[0:CTX_LEN, :])
        o_ref[pl.ds(q0, nq), :] = (o * li_ref[par]).astype(o_ref.dtype)

    n_groups = rows // NA_GROUP
    stage_a(0, 0)
    stage_a(1, 1)
    stage_b(0, 0)

    def pair(u, carry):
        t = 2 + 2 * u
        stage_a(t, 0)
        stage_b(t - 1, 1)
        stage_c(t - 2, 0)
        stage_a(t + 1, 1)
        stage_b(t, 0)
        stage_c(t - 1, 1)
        return carry

    lax.fori_loop(0, (n_groups - 2) // 2, pair, 0)
    stage_b(n_groups - 1, 1)
    stage_c(n_groups - 2, 0)
    stage_c(n_groups - 1, 1)


def _na_bias_table(rpb):
    h = rpb.shape[0]
    qc = np.arange(GRID_W)[:, None]
    kc = np.arange(GRID_W)[None, :]
    cs = np.clip(qc - NA_WIN_COLS // 2, 0, GRID_W - NA_WIN_COLS)
    col_ok = (kc >= cs) & (kc < cs + NA_WIN_COLS)
    dc = np.clip(kc - qc + (NA_WIN_COLS - 1), 0, 2 * NA_WIN_COLS - 2)
    a = jnp.take(rpb.astype(F32), jnp.asarray(dc.reshape(-1)), axis=2)
    a = a.reshape(h, 2 * NA_WIN_ROWS - 1, GRID_W, GRID_W) * LOG2E
    a = jnp.where(jnp.asarray(col_ok)[None, None], a, NEG_BIG)
    a = jnp.pad(a, ((0, 0), (2, 2), (0, 0), (0, 0)), constant_values=NEG_BIG)
    return jnp.concatenate([a[:, :-1], a[:, 1:]], axis=-1)


def _na_attention(qkv, rpb, n_batch, tb):
    t = qkv.shape[0]
    rows = (tb - CTX_LEN) // GRID_W
    assert rows % (2 * NA_GROUP) == 0 and rows >= NA_KROWS
    nq, nk = NA_GROUP * GRID_W, NA_KROWS * GRID_W
    table = _na_bias_table(rpb)
    hh = NA_HEADS
    return pl.pallas_call(
        functools.partial(_na_kernel, rows=rows),
        grid=(hh, n_batch),
        in_specs=[
            pl.BlockSpec((tb, LANES), lambda h, b: (b, h)),
            pl.BlockSpec((tb, LANES), lambda h, b: (b, hh + h)),
            pl.BlockSpec((tb, LANES), lambda h, b: (b, 2 * hh + h)),
            pl.BlockSpec((None, 2 * NA_WIN_ROWS + 2, GRID_W, LANES), lambda h, b: (h, 0, 0, 0)),
        ],
        out_specs=pl.BlockSpec((tb, LANES), lambda h, b: (b, h)),
        out_shape=jax.ShapeDtypeStruct((t, D_MODEL), BF16),
        scratch_shapes=[
            pltpu.VMEM((2, nq, nk), F32),
            pltpu.VMEM((2, nq, CTX_LEN), F32),
            pltpu.VMEM((2, nq, nk), BF16),
            pltpu.VMEM((2, nq, CTX_LEN), BF16),
            pltpu.VMEM((2, nq, LANES), F32),
        ],
        compiler_params=_cparams(("arbitrary", "arbitrary")),
        name="na_attention",
    )(qkv, qkv, qkv, table)


def _da_kernel(lam_ref, g_ref, q_ref, k_ref, v_ref, o_ref, acc_ref, m_ref, l_ref,
               s0_ref, s1_ref, p0_ref, p1_ref, a0_ref, a1_ref, qbd_ref, *, lam_init, n_kv_chunks):
    d = DIFF_HEAD_DIM
    i = pl.program_id(2)
    lam = lam_ref[...]
    lam_full = (jnp.exp(jnp.sum(lam[0:1] * lam[1:2], axis=-1, keepdims=True))
                - jnp.exp(jnp.sum(lam[2:3] * lam[3:4], axis=-1, keepdims=True)) + lam_init)
    s_bufs, p_bufs, a_bufs = (s0_ref, s1_ref), (p0_ref, p1_ref), (a0_ref, a1_ref)

    def tile(q0, nq, n_lat_chunks):
        acc_ref[:, 0:nq, :] = jnp.zeros((2, nq, 2 * d), F32)
        m_ref[:, 0:nq, :] = jnp.full((2, nq, LANES), NEG_BIG, F32)
        l_ref[:, 0:nq, :] = jnp.zeros((2, nq, LANES), F32)

        def key_rows(e):
            if isinstance(e, int):
                return (0, CTX_LEN) if e == 0 else (CTX_LEN + (e - 1) * DA_TK, DA_TK)
            return pl.multiple_of(CTX_LEN + (e - 1) * DA_TK, ROW_TILE), DA_TK

        q = q_ref[pl.ds(q0, nq), :]
        zero = jnp.zeros((nq, d), q.dtype)
        qbd_ref[0:nq, :] = jnp.concatenate([q[:, :d], zero], axis=1)
        qbd_ref[nq:2 * nq, :] = jnp.concatenate([zero, q[:, d:]], axis=1)

        def stage_a(e, par):
            k0, nk = key_rows(e)
            s = _dot_nt(qbd_ref[0:2 * nq, :], k_ref[pl.ds(k0, nk), :])
            for c in range(2):
                s_bufs[par][c, 0:nq, 0:nk] = s[c * nq:(c + 1) * nq, :]

        def stage_b(e, par):
            _, nk = key_rows(e)
            s_ref, p_ref, a_ref = s_bufs[par], p_bufs[par], a_bufs[par]
            for c in range(2):
                for sb in range(nq // DA_SUB):
                    rows = slice(sb * DA_SUB, (sb + 1) * DA_SUB)
                    blocks = [s_ref[c, rows, j * LANES:(j + 1) * LANES] for j in range(nk // LANES)]
                    m_prev = m_ref[c, rows, :]
                    m_new = jnp.maximum(
                        m_prev, jnp.max(functools.reduce(jnp.maximum, blocks), axis=-1, keepdims=True))
                    alpha = jnp.exp2(m_prev - m_new)
                    p = [jnp.exp2(blk - m_new) for blk in blocks]
                    l_ref[c, rows, :] = alpha * l_ref[c, rows, :] + functools.reduce(jnp.add, p)
                    for j, pj in enumerate(p):
                        p_ref[c, rows, j * LANES:(j + 1) * LANES] = pj.astype(BF16)
                    a_ref[c, rows, :] = alpha
                    m_ref[c, rows, :] = m_new

        def stage_c(e, par):
            k0, nk = key_rows(e)
            v = v_ref[pl.ds(k0, nk), :]
            p = jnp.concatenate([p_bufs[par][0, 0:nq, 0:nk], p_bufs[par][1, 0:nq, 0:nk]], axis=0)
            pv_all = _dot(p, v)
            for c in range(2):
                pv = pv_all[c * nq:(c + 1) * nq, :]
                alpha = a_bufs[par][c, 0:nq, :]
                acc_ref[c, 0:nq, 0:LANES] = alpha * acc_ref[c, 0:nq, 0:LANES] + pv[:, :LANES]
                acc_ref[c, 0:nq, LANES:] = alpha * acc_ref[c, 0:nq, LANES:] + pv[:, LANES:]

        n = n_lat_chunks
        if n == 0:
            stage_a(0, 0)
            stage_b(0, 0)
            stage_c(0, 0)
        else:
            assert n % 2 == 0
            stage_a(0, 0)
            stage_a(1, 1)
            stage_b(0, 0)
            stage_a(2, 0)
            stage_b(1, 1)
            stage_c(0, 0)

            def pair(u, carry):
                t = 3 + 2 * u
                stage_a(t, 1)
                stage_b(t - 1, 0)
                stage_c(t - 2, 1)
                stage_a(t + 1, 0)
                stage_b(t, 1)
                stage_c(t - 1, 0)
                return carry

            lax.fori_loop(0, (n - 2) // 2, pair, 0)
            stage_b(n, 0)
            stage_c(n - 1, 1)
            stage_c(n, 0)

        r0 = 1.0 / jnp.sum(l_ref[0, 0:nq, :], axis=-1, keepdims=True)
        r1 = lam_full / jnp.sum(l_ref[1, 0:nq, :], axis=-1, keepdims=True)
        o_a = acc_ref[0, 0:nq, 0:LANES] * r0 - acc_ref[1, 0:nq, 0:LANES] * r1
        o_b = acc_ref[0, 0:nq, LANES:] * r0 - acc_ref[1, 0:nq, LANES:] * r1
        ms = (jnp.sum(o_a * o_a, axis=-1, keepdims=True) + jnp.sum(o_b * o_b, axis=-1, keepdims=True)) / (2 * d)
        inv = lax.rsqrt(ms + NORM_EPS) * (1.0 - lam_init)
        g = g_ref[...]
        o_ref[pl.ds(q0, nq), 0:LANES] = (o_a * inv * g[:, :LANES]).astype(o_ref.dtype)
        o_ref[pl.ds(q0, nq), LANES:] = (o_b * inv * g[:, LANES:]).astype(o_ref.dtype)

    @pl.when(i == 0)
    def _():
        tile(0, CTX_LEN, 0)

    @pl.when(i > 0)
    def _():
        tile(pl.multiple_of(CTX_LEN + (i - 1) * DA_TQ, ROW_TILE), DA_TQ, n_kv_chunks)


def _diff_attention(qkv, lam, subln_g, layer_idx, n_batch, tb):
    t = qkv.shape[0]
    hh = DIFF_HEADS
    w = 2 * DIFF_HEAD_DIM
    n_lat = (tb - CTX_LEN) // DA_TQ
    lam_init = 0.8 - 0.6 * math.exp(-0.3 * layer_idx)
    return pl.pallas_call(
        functools.partial(_da_kernel, lam_init=lam_init, n_kv_chunks=(tb - CTX_LEN) // DA_TK),
        grid=(n_batch, hh, 1 + n_lat),
        in_specs=[
            pl.BlockSpec((4, DIFF_HEAD_DIM), lambda b, h, i: (0, 0)),
            pl.BlockSpec((1, w), lambda b, h, i: (0, 0)),
            pl.BlockSpec((tb, w), lambda b, h, i: (b, h)),
            pl.BlockSpec((tb, w), lambda b, h, i: (b, hh + h)),
            pl.BlockSpec((tb, w), lambda b, h, i: (b, 2 * hh + h)),
        ],
        out_specs=pl.BlockSpec((tb, w), lambda b, h, i: (b, h)),
        out_shape=jax.ShapeDtypeStruct((t, D_MODEL), BF16),
        scratch_shapes=[
            pltpu.VMEM((2, DA_TQ, w), F32),
            pltpu.VMEM((2, DA_TQ, LANES), F32),
            pltpu.VMEM((2, DA_TQ, LANES), F32),
            pltpu.VMEM((2, DA_TQ, DA_TK), F32),
            pltpu.VMEM((2, DA_TQ, DA_TK), F32),
            pltpu.VMEM((2, DA_TQ, DA_TK), BF16),
            pltpu.VMEM((2, DA_TQ, DA_TK), BF16),
            pltpu.VMEM((2, DA_TQ, LANES), F32),
            pltpu.VMEM((2, DA_TQ, LANES), F32),
            pltpu.VMEM((2 * DA_TQ, w), BF16),
        ],
        compiler_params=_cparams(("arbitrary", "arbitrary", "arbitrary")),
        name="diff_attention",
    )(lam.astype(F32), subln_g.astype(F32).reshape(1, w), qkv, qkv, qkv)


def _hg_kernel(q_ref, i_ref, ff_ref, fb_ref, lb_ref, o_ref, acc_ref, stf_ref, stb_ref, xb_ref, xk_ref, xv_ref, *,
               n_groups):
    c = HGRN_CHUNK
    rg = HG_GROUP * c
    dk = HGRN_DK
    acc_ref[...] = jnp.zeros(acc_ref.shape, F32)
    stf_ref[...] = jnp.zeros(stf_ref.shape, F32)
    stb_ref[...] = jnp.zeros(stb_ref.shape, F32)
    row = lax.broadcasted_iota(jnp.int32, (rg, rg), 0)
    col = lax.broadcasted_iota(jnp.int32, (rg, rg), 1)
    same = (row // c) == (col // c)
    lower = same & (row >= col)
    upper = same & (col >= row)
    lower_b = lower.astype(BF16)
    upper_b = upper.astype(BF16)
    chunk_of_row = lax.broadcasted_iota(jnp.int32, (rg, 1), 0) // c
    lb_all = lb_ref[...]

    def per_chunk_rows(b, r):
        return jnp.concatenate(
            [jnp.broadcast_to(b[g * c + r:g * c + r + 1, :], (c, dk)) for g in range(HG_GROUP)], axis=0)

    def intra_exact(q, k, v, bcum, forward):
        xb_ref[...] = bcum
        xk_ref[...] = k
        xv_ref[...] = v.astype(F32)
        rowi = lax.broadcasted_iota(jnp.int32, (c, 1), 0)
        outs = []
        for g in range(HG_GROUP):
            bc_g = bcum[g * c:(g + 1) * c, :]
            q_g = q[g * c:(g + 1) * c, :]

            def column(s, acc, g=g, bc_g=bc_g, q_g=q_g):
                live = (rowi >= s) if forward else (rowi <= s)
                bs = xb_ref[pl.ds(g * c + s, 1), :]
                w = jnp.where(live, jnp.exp(jnp.where(live, bc_g - bs, 0.0)), 0.0) * q_g
                a = jnp.sum(w * xk_ref[pl.ds(g * c + s, 1), :], axis=-1, keepdims=True)
                return acc + a * xv_ref[pl.ds(g * c + s, 1), :]

            outs.append(lax.fori_loop(0, c, column, jnp.zeros((c, dk), F32)))
        return jnp.concatenate(outs, axis=0)

    def group(hd, gidx, f_ref, lb_row, mask, mask_b, st_ref, end_row, order, exact):
        r0 = pl.multiple_of(gidx * rg, rg)
        cols = slice(hd * dk, (hd + 1) * dk)
        lb = lb_all[lb_row:lb_row + 1, cols]
        qr = q_ref[pl.ds(r0, rg), cols].astype(F32)
        q = qr * jax.nn.sigmoid(qr) * (dk ** -0.5)
        v = i_ref[pl.ds(r0, rg), cols]
        fr = f_ref[pl.ds(r0, rg), cols].astype(F32)
        fg = lb + (1.0 - lb) * jax.nn.sigmoid(fr)
        k = 1.0 - fg
        lf = jnp.log(fg)
        hi = lf.astype(BF16)
        rem = lf - hi.astype(F32)
        mid = rem.astype(BF16)
        lo = (rem - mid.astype(F32)).astype(BF16)
        cs = _dot(mask_b, jnp.concatenate([hi, mid, lo], axis=1))
        bcum = cs[:, 0:dk] + cs[:, dk:2 * dk] + cs[:, 2 * dk:]
        b_end = per_chunk_rows(bcum, end_row)
        if exact:
            o_intra = intra_exact(q, k, v, bcum, end_row != 0)
        else:
            b_mid = per_chunk_rows(bcum, c // 2)
            qt = (q * jnp.exp(bcum - b_mid)).astype(BF16)
            kt = (k * jnp.exp(b_mid - bcum)).astype(BF16)
            a = jnp.where(mask, _dot_nt(qt, kt), 0.0).astype(BF16)
            o_intra = _dot(a, v)
        k_end = k * jnp.exp(b_end - bcum)
        k_exp = jnp.concatenate(
            [jnp.where(chunk_of_row == g, k_end, 0.0) for g in range(HG_GROUP)], axis=1).astype(BF16)
        u = _dot_tn(v, k_exp)
        st = st_ref[hd]
        st_in = [None] * HG_GROUP
        for g in order:
            st_in[g] = st.astype(BF16)
            st = st * jnp.exp(bcum[g * c + end_row:g * c + end_row + 1, :]) + u[:, g * dk:(g + 1) * dk]
        st_ref[hd] = st
        oi = _dot_nt((q * jnp.exp(bcum)).astype(BF16), jnp.concatenate(st_in, axis=0))
        o_inter = jnp.concatenate(
            [oi[g * c:(g + 1) * c, g * dk:(g + 1) * dk] for g in range(HG_GROUP)], axis=0)
        acc_ref[pl.ds(r0, rg), cols] += o_intra + o_inter

    fwd_order = list(range(HG_GROUP))

    def scan(exact):
        def body(n, carry):
            nb = jnp.where(n == 0, 0, n_groups - n)
            for hd in range(HG_HEADS):
                group(hd, n, ff_ref, 0, lower, lower_b, stf_ref, c - 1, fwd_order, exact)
                group(hd, nb, fb_ref, 1, upper, upper_b, stb_ref, 0, fwd_order[::-1], exact)
            return carry

        lax.fori_loop(0, n_groups, body, 0)

    def min_forget(f_ref, lb_row):
        lb = lb_all[lb_row:lb_row + 1, :]
        f_min = jnp.min(f_ref[...].astype(F32), axis=0, keepdims=True)
        return jnp.min(lb + (1.0 - lb) * jax.nn.sigmoid(f_min))

    safe = jnp.minimum(min_forget(ff_ref, 0), min_forget(fb_ref, 1)) >= HG_SAFE_FORGET

    @pl.when(safe)
    def _():
        scan(False)

    @pl.when(jnp.logical_not(safe))
    def _():
        scan(True)

    o_ref[...] = acc_ref[...].astype(o_ref.dtype)


def _hgrn_scan(proj, lb2, n_batch, tb):
    t = proj.shape[0]
    hh = HGRN_HEADS // HG_HEADS
    wd = HG_HEADS * LANES
    rg = HG_GROUP * HGRN_CHUNK
    assert CTX_LEN == rg and tb % rg == 0

    def col(k):
        return pl.BlockSpec((tb, wd), lambda b, h: (b, k * hh + h))

    return pl.pallas_call(
        functools.partial(_hg_kernel, n_groups=tb // rg),
        grid=(n_batch, hh),
        in_specs=[col(0), col(1), col(3), col(4), pl.BlockSpec((2, wd), lambda b, h: (0, h))],
        out_specs=pl.BlockSpec((tb, wd), lambda b, h: (b, h)),
        out_shape=jax.ShapeDtypeStruct((t, D_MODEL), BF16),
        scratch_shapes=[pltpu.VMEM((tb, wd), F32),
                        pltpu.VMEM((HG_HEADS, LANES, LANES), F32), pltpu.VMEM((HG_HEADS, LANES, LANES), F32),
                        pltpu.VMEM((rg, LANES), F32), pltpu.VMEM((rg, LANES), F32), pltpu.VMEM((rg, LANES), F32)],
        compiler_params=_cparams(("arbitrary", "arbitrary")),
        name="hgrn_scan",
    )(proj, proj, proj, proj, lb2)


def _hg_finish_kernel(o_ref, g_ref, ng_ref, h_ref):
    o = o_ref[...].astype(F32)
    y = o * lax.rsqrt(jnp.mean(o * o, axis=-1, keepdims=True) + NORM_EPS) * ng_ref[...]
    g = g_ref[...].astype(F32)
    h_ref[...] = (y * (g * jax.nn.sigmoid(g))).astype(h_ref.dtype)


def _hgrn_finish(o, proj, norm_g):
    t, d = o.shape
    return pl.pallas_call(
        _hg_finish_kernel,
        grid=(t // ROW_TILE,),
        in_specs=[
            pl.BlockSpec((ROW_TILE, d), lambda i: (i, 0)),
            pl.BlockSpec((ROW_TILE, d), lambda i: (i, 2)),
            pl.BlockSpec((1, d), lambda i: (0, 0)),
        ],
        out_specs=pl.BlockSpec((ROW_TILE, d), lambda i: (i, 0)),
        out_shape=jax.ShapeDtypeStruct((t, d), BF16),
        compiler_params=_cparams(("arbitrary",)),
        name="hgrn_finish",
    )(o, proj, norm_g.astype(F32).reshape(1, d))


def _moe_kernel(be_ref, first_ref, nxt_ref, nu_ref, tok_ref, x_hbm, wg_hbm, wu_hbm, wd_hbm, o_ref,
                xbuf, sg_ref, su_ref, sd_ref, bg_ref, bu_ref, bd_ref, sem, xsem, *, layer):
    b = pl.program_id(0)
    n_used = nu_ref[0]

    def weight_copies(e):
        return (pltpu.make_async_copy(wg_hbm.at[layer, e], sg_ref, sem.at[0]),
                pltpu.make_async_copy(wu_hbm.at[layer, e], su_ref, sem.at[1]),
                pltpu.make_async_copy(wd_hbm.at[layer, e], sd_ref, sem.at[2]))

    def start_gather(blk, slot):
        base = blk * MOE_TM
        for r in range(MOE_TM):
            pltpu.make_async_copy(x_hbm.at[pl.ds(tok_ref[base + r], 1)], xbuf.at[slot, pl.ds(r, 1)],
                                  xsem.at[slot]).start()

    def wait_gather(slot):
        pltpu.make_async_copy(x_hbm.at[pl.ds(0, MOE_TM)], xbuf.at[slot], xsem.at[slot]).wait()

    @pl.when(b == 0)
    def _():
        for cp in weight_copies(be_ref[0]):
            cp.start()
        start_gather(0, 0)
        start_gather(1, 1)

    @pl.when(first_ref[b] == 1)
    def _():
        for cp in weight_copies(be_ref[b]):
            cp.wait()
        for src, dst in ((sg_ref, bg_ref), (su_ref, bu_ref), (sd_ref, bd_ref)):
            slab = src.shape[0] // MOE_CAST_SLABS

            def cast(r, carry, src=src, dst=dst, slab=slab):
                r0 = pl.multiple_of(r * slab, slab)
                dst[pl.ds(r0, slab), :] = src[pl.ds(r0, slab), :].astype(BF16)
                return carry

            lax.fori_loop(0, MOE_CAST_SLABS, cast, 0)

        @pl.when(nxt_ref[b] >= 0)
        def _():
            for cp in weight_copies(nxt_ref[b]):
                cp.start()

    @pl.when(b < n_used)
    def _():
        slot = b % MOE_XBUFS
        wait_gather(slot)
        xu = xbuf[slot]
        x = jnp.concatenate(
            [lax.bitcast_convert_type(xu << 16, F32).astype(BF16),
             lax.bitcast_convert_type(xu & jnp.uint32(0xFFFF0000), F32).astype(BF16)], axis=1)
        g = _dot(x, bg_ref[...])
        u = _dot(x, bu_ref[...])
        h = (g * jax.nn.sigmoid(g) * u).astype(BF16)
        ybits = lax.bitcast_convert_type(_dot(h, bd_ref[...]).astype(BF16).astype(F32), jnp.uint32)
        yh = ybits.shape[1] // 2
        o_ref[...] = (ybits[:, :yh] >> 16) | ybits[:, yh:]
        start_gather(b + 2, (b + 2) % MOE_XBUFS)

    @pl.when(b >= n_used)
    def _():
        @pl.when(b == n_used)
        def _():
            wait_gather(b % MOE_XBUFS)
            wait_gather((b + 1) % MOE_XBUFS)

        o_ref[...] = jnp.zeros(o_ref.shape, o_ref.dtype)


def _moe_experts(hp, slot_tok, blk_expert, blk_first, blk_next, n_used, w_gate, w_up, w_down, layer):
    d = 2 * hp.shape[1]
    f = w_gate.shape[3]
    n_slots = slot_tok.shape[0]
    n_blocks = n_slots // MOE_TM
    grid_spec = pltpu.PrefetchScalarGridSpec(
        num_scalar_prefetch=5,
        grid=(n_blocks,),
        in_specs=[pl.BlockSpec(memory_space=pl.ANY)] * 4,
        out_specs=pl.BlockSpec((MOE_TM, d // 2), lambda b, *_: (b, 0)),
        scratch_shapes=[
            pltpu.VMEM((MOE_XBUFS, MOE_TM, d // 2), jnp.uint32),
            pltpu.VMEM((d, f), F32), pltpu.VMEM((d, f), F32), pltpu.VMEM((f, d), F32),
            pltpu.VMEM((d, f), BF16), pltpu.VMEM((d, f), BF16), pltpu.VMEM((f, d), BF16),
            pltpu.SemaphoreType.DMA((3,)),
            pltpu.SemaphoreType.DMA((MOE_XBUFS,)),
        ],
    )
    return pl.pallas_call(
        functools.partial(_moe_kernel, layer=layer),
        grid_spec=grid_spec,
        out_shape=jax.ShapeDtypeStruct((n_slots, d // 2), jnp.uint32),
        compiler_params=_cparams(("arbitrary",)),
        name="moe_experts",
    )(blk_expert, blk_first, blk_next, n_used, slot_tok, hp, w_gate, w_up, w_down)


def _route(route, counts_row, n_blocks):
    t = route.shape[0]
    e_idx = route[:, 0:TOP_K].astype(jnp.int32)
    rank = route[:, TOP_K:2 * TOP_K].astype(jnp.int32)
    w = route[:, 2 * TOP_K:3 * TOP_K]
    counts = counts_row[0, :N_EXPERTS].astype(jnp.int32)
    pcounts = (counts + MOE_TM - 1) // MOE_TM * MOE_TM
    pends = jnp.cumsum(pcounts)
    pstarts = pends - pcounts
    e_ids = jnp.arange(N_EXPERTS, dtype=jnp.int32)
    start_of = jnp.sum(jnp.where(e_idx[..., None] == e_ids, pstarts, 0), axis=-1)
    dest = (start_of + rank).reshape(-1).astype(jnp.int32)
    flat_t = jnp.arange(t * TOP_K, dtype=jnp.int32) // TOP_K
    slot_tok = jnp.zeros((n_blocks * MOE_TM,), jnp.int32).at[dest].set(flat_t)
    blk_row0 = jnp.arange(n_blocks, dtype=jnp.int32) * MOE_TM
    blk_expert = jnp.minimum(
        jnp.sum((pends[None, :] <= blk_row0[:, None]).astype(jnp.int32), axis=1), N_EXPERTS - 1).astype(jnp.int32)
    n_used = (pends[-1] // MOE_TM).astype(jnp.int32).reshape(1)
    blk_ids = jnp.arange(n_blocks, dtype=jnp.int32)
    prev_expert = jnp.concatenate([jnp.full((1,), -1, jnp.int32), blk_expert[:-1]])
    blk_first = ((blk_ids < n_used[0]) & (blk_expert != prev_expert)).astype(jnp.int32)
    e_ids = jnp.arange(N_EXPERTS, dtype=jnp.int32)
    later_used = (e_ids[None, :] > e_ids[:, None]) & (counts[None, :] > 0)
    next_used = jnp.min(jnp.where(later_used, e_ids[None, :], N_EXPERTS), axis=1)
    next_used = jnp.where(next_used == N_EXPERTS, -1, next_used).astype(jnp.int32)
    blk_next = next_used[blk_expert]
    return w, dest.reshape(t, TOP_K), slot_tok, (blk_expert, blk_first, blk_next, n_used)


def _unpack_pairs(yu):
    return jnp.concatenate([lax.bitcast_convert_type(yu << 16, F32),
                            lax.bitcast_convert_type(yu & jnp.uint32(0xFFFF0000), F32)], axis=1)


def _moe_res_kernel(dest_ref, xs_ref, w_ref, gate_ref, g_ref, shift_ref, scale_ref, yb_hbm, o_ref, h_ref,
                    ybuf, ysem, *, n_tiles):
    i = pl.program_id(0)

    def start_gather(tile, slot):
        base = tile * (ROW_TILE * TOP_K)
        for r in range(ROW_TILE):
            for k in range(TOP_K):
                pltpu.make_async_copy(yb_hbm.at[pl.ds(dest_ref[base + r * TOP_K + k], 1)],
                                      ybuf.at[slot, pl.ds(k * ROW_TILE + r, 1)], ysem.at[slot]).start()

    def wait_gather(slot):
        pltpu.make_async_copy(yb_hbm.at[pl.ds(0, TOP_K * ROW_TILE)], ybuf.at[slot], ysem.at[slot]).wait()

    @pl.when(i == 0)
    def _():
        start_gather(0, 0)
        start_gather(1, 1)

    slot = i % MOE_XBUFS
    wait_gather(slot)
    w = w_ref[...]
    y = (w[:, 0:1] * _unpack_pairs(ybuf[slot, 0:ROW_TILE, :])
         + w[:, 1:2] * _unpack_pairs(ybuf[slot, ROW_TILE:, :]))
    x = xs_ref[...] + gate_ref[...] * y
    o_ref[...] = x
    xn = x * lax.rsqrt(jnp.mean(x * x, axis=-1, keepdims=True) + NORM_EPS)
    h_ref[...] = (xn * g_ref[...] * (1.0 + scale_ref[...]) + shift_ref[...]).astype(h_ref.dtype)
    start_gather(jnp.minimum(i + 2, n_tiles - 1), (i + 2) % MOE_XBUFS)

    @pl.when(i == n_tiles - 1)
    def _():
        wait_gather((i + 1) % MOE_XBUFS)
        wait_gather((i + 2) % MOE_XBUFS)


def _moe_res_final_kernel(xs_ref, y0_ref, y1_ref, w_ref, gate_ref, fg_ref, o_ref):
    w = w_ref[...]
    y = w[:, 0:1] * _unpack_pairs(y0_ref[...]) + w[:, 1:2] * _unpack_pairs(y1_ref[...])
    x = xs_ref[...] + gate_ref[...] * y
    o_ref[...] = x * lax.rsqrt(jnp.mean(x * x, axis=-1, keepdims=True) + NORM_EPS) * fg_ref[...]


def _moe_residual(xs, yb, dest, w, mods3, g3, layer, n_batch, tpb):
    t, d = xs.shape
    nb = n_batch
    n_tiles = t // ROW_TILE
    assert n_tiles >= 2
    row = lambda i, *_: (i, 0)

    def mod_spec(lyr, k):
        return pl.BlockSpec((None, 1, d), lambda i, *_: (lyr * 8 + _row_sel(i, tpb, nb), 0, k))

    grid_spec = pltpu.PrefetchScalarGridSpec(
        num_scalar_prefetch=1,
        grid=(n_tiles,),
        in_specs=[
            pl.BlockSpec((ROW_TILE, d), row),
            pl.BlockSpec((ROW_TILE, TOP_K), row),
            mod_spec(layer, 5),
            pl.BlockSpec((None, 1, d), lambda i, *_: ((layer + 1) * 2, 0, 0)),
            mod_spec(layer + 1, 0),
            mod_spec(layer + 1, 1),
            pl.BlockSpec(memory_space=pl.ANY),
        ],
        out_specs=[pl.BlockSpec((ROW_TILE, d), row), pl.BlockSpec((ROW_TILE, d), row)],
        scratch_shapes=[pltpu.VMEM((MOE_XBUFS, TOP_K * ROW_TILE, d // 2), jnp.uint32),
                        pltpu.SemaphoreType.DMA((MOE_XBUFS,))],
    )
    return pl.pallas_call(
        functools.partial(_moe_res_kernel, n_tiles=n_tiles),
        grid_spec=grid_spec,
        out_shape=[jax.ShapeDtypeStruct((t, d), F32), jax.ShapeDtypeStruct((t, d), BF16)],
        input_output_aliases={1: 0},
        compiler_params=_cparams(("arbitrary",)),
        name="moe_residual",
    )(dest.reshape(-1), xs, w, mods3, g3, mods3, mods3, yb)


def _moe_residual_final(xs, y0, y1, w, mods3, layer, final_g, n_batch, tpb):
    t, d = xs.shape
    lat = tpb - CTX_LEN // ROW_TILE
    src = lambda i: ((i // lat) * tpb + CTX_LEN // ROW_TILE + i % lat, 0)
    return pl.pallas_call(
        _moe_res_final_kernel,
        grid=(n_batch * lat,),
        in_specs=[
            pl.BlockSpec((ROW_TILE, d), src),
            pl.BlockSpec((ROW_TILE, d // 2), src),
            pl.BlockSpec((ROW_TILE, d // 2), src),
            pl.BlockSpec((ROW_TILE, TOP_K), src),
            pl.BlockSpec((None, 1, d), lambda i: (layer * 8 + i // lat, 0, 5)),
            pl.BlockSpec((1, d), lambda i: (0, 0)),
        ],
        out_specs=pl.BlockSpec((ROW_TILE, d), lambda i: (i, 0)),
        out_shape=jax.ShapeDtypeStruct((n_batch * lat * ROW_TILE, d), F32),
        compiler_params=_cparams(("arbitrary",)),
        name="moe_residual_final",
    )(xs, y0, y1, w, mods3, final_g.astype(F32).reshape(1, d))


def _rope_tables(n_batch, seq):
    quarter = DIFF_HEAD_DIM // 4
    tpos = jnp.arange(seq)
    rowp = (tpos // GRID_W).astype(F32)
    colp = (tpos % GRID_W).astype(F32)
    inv = ROPE_THETA ** (-jnp.arange(quarter, dtype=F32) / quarter)
    ang_r = rowp[:, None] * inv[None, :]
    ang_c = colp[:, None] * inv[None, :]
    cos = jnp.concatenate([jnp.cos(ang_r)] * 2 + [jnp.cos(ang_c)] * 2, axis=-1)
    sin = jnp.concatenate([-jnp.sin(ang_r), jnp.sin(ang_r), -jnp.sin(ang_c), jnp.sin(ang_c)], axis=-1)
    cos = jnp.concatenate([jnp.ones((CTX_LEN, LANES), F32), cos], axis=0)
    sin = jnp.concatenate([jnp.zeros((CTX_LEN, LANES), F32), sin], axis=0)
    return jnp.tile(cos, (n_batch, 1)), jnp.tile(sin, (n_batch, 1))


def kernel(x, c, ctx, c_ctx, ada_w, ada_b, norm_g, na_w_qkv, na_rpb, na_w_o, da_w_qkv, da_lambda, da_subln_g, da_w_o, hg_w_in, hg_lb, hg_norm_g, hg_w_o, moe_w_router, moe_b_router, moe_w_gate, moe_w_up, moe_w_down, final_g):
    n_batch, seq, d = x.shape
    tb = CTX_LEN + seq
    t = n_batch * tb
    tpb = tb // ROW_TILE
    assert d == D_MODEL and ctx.shape[1] == CTX_LEN and n_batch + 1 <= 8
    assert seq % DA_TQ == 0 and t % MM_TM == 0 and seq // GRID_W >= NA_WIN_ROWS

    xs = jnp.concatenate([ctx, x], axis=1).reshape(t, d).astype(F32)

    c_rows = jnp.zeros((8, d), F32).at[:n_batch].set(c.astype(F32)).at[n_batch].set(c_ctx.astype(F32))
    mods = _ada_mods(c_rows, ada_w.astype(F32), ada_b.astype(F32))
    mods3 = mods.reshape(DEPTH * 8, 1, 6 * d)
    g3 = norm_g.astype(F32).reshape(DEPTH * 2, 1, d)

    lb_p = jax.nn.softmax(hg_lb.astype(F32), axis=1)
    lb_all = jnp.cumsum(lb_p, axis=1) - lb_p[:, :1]

    wr = jnp.zeros((d, LANES), F32).at[:, :N_EXPERTS].set(moe_w_router.astype(F32))
    wr_hi = wr.astype(BF16)
    wr_lo = (wr - wr_hi.astype(F32)).astype(BF16)
    rb = jnp.zeros((1, LANES), F32).at[0, :N_EXPERTS].set(moe_b_router.astype(F32))

    n_blocks = -(-(t * TOP_K) // MOE_TM) + N_EXPERTS + (MOE_XBUFS - 1)
    rope = _rope_tables(n_batch, seq)
    wg = moe_w_gate.astype(F32)
    wu = moe_w_up.astype(F32)
    wd = moe_w_down.astype(F32)
    na_qs = NA_HEAD_DIM ** -0.5 * LOG2E
    da_qs = DIFF_HEAD_DIM ** -0.5 * LOG2E

    counters = [0, 0, 0]
    out = None
    for i in range(DEPTH):
        kind = i % N_MIXERS
        j = counters[kind]
        counters[kind] += 1

        if i == 0:
            h = _norm_mod(xs, mods3, g3, i, 0, n_batch, tpb)[0]
        if kind == 0:
            qkv = _matmul(h, na_w_qkv[j], q_cols=d, q_scale=na_qs)
            o = _na_attention(qkv, na_rpb[j], n_batch, tb)
            w_o = na_w_o[j]
        elif kind == 1:
            qkv = _matmul(h, da_w_qkv[j], q_cols=d, q_scale=da_qs, rope=rope, n_rope_cols=2 * d)
            o = _diff_attention(qkv, da_lambda[j], da_subln_g[j], i, n_batch, tb)
            w_o = da_w_o[j]
        else:
            proj = _matmul(h, hg_w_in[j])
            o32 = _hgrn_scan(proj, lb_all[:, i, :], n_batch, tb)
            o = _hgrn_finish(o32, proj, hg_norm_g[j])
            w_o = hg_w_o[j]
        xs = _matmul_residual(o, w_o, xs, mods3, i, 2, n_batch, tpb)

        hp, route, counts = _norm_mod(xs, mods3, g3, i, 1, n_batch, tpb, router=(wr_hi, wr_lo, rb))
        w, dest, slot_tok, blk_info = _route(route, counts, n_blocks)
        yb = _moe_experts(hp, slot_tok, *blk_info, wg, wu, wd, i)
        if i < DEPTH - 1:
            xs, h = _moe_residual(xs, yb, dest, w, mods3, g3, i, n_batch, tpb)
        else:
            y0 = yb.at[dest[:, 0]].get(mode="promise_in_bounds")
            y1 = yb.at[dest[:, 1]].get(mode="promise_in_bounds")
            out = _moe_residual_final(xs, y0, y1, w, mods3, i, final_g, n_batch, tpb)
    return out.reshape(n_batch, seq, d).astype(x.dtype)
```

```python
import functools
import math

import numpy as np
import jax
import jax.numpy as jnp
from jax import lax
from jax.experimental import pallas as pl
from jax.experimental.pallas import tpu as pltpu

F32 = jnp.float32
BF16 = jnp.bfloat16

D_MODEL = 2048
DEPTH = 4
GRID_W = 64
CTX_LEN = 256
N_MIXERS = 3
NORM_EPS = 1e-6
ROPE_THETA = 10000.0

NA_HEADS = 16
NA_HEAD_DIM = 128
NA_WIN_ROWS = 8
NA_WIN_COLS = 16

DIFF_HEADS = 8
DIFF_HEAD_DIM = 128

HGRN_HEADS = 16
HGRN_DK = 128
HGRN_CHUNK = 64

N_EXPERTS = 16
N_GROUPS = 4
EXPERTS_PER_GROUP = 4
TOP_K = 2
D_EXPERT = 1024

LANES = 128
ROW_TILE = 256
MM_TM = 512
MM_TN = 1024
MOE_TM = 256
DA_TQ = 512
VMEM_LIMIT = 56 * 1024 * 1024

NA_GROUP = 4
NA_KROWS = 12
DA_SUB = 64
DA_TK = 512
HG_GROUP = 4
HG_HEADS = 1
HG_SAFE_FORGET = math.exp(-2.0)
MOE_XBUFS = 3
MOE_CAST_SLABS = 8

NEG_BIG = -1e30
LOG2E = 1.4426950408889634


def _cparams(sem):
    return pltpu.CompilerParams(dimension_semantics=sem, vmem_limit_bytes=VMEM_LIMIT)


def _dot(a, b):
    return jnp.dot(a, b, preferred_element_type=F32)


def _dot_nt(a, b):
    return lax.dot_general(a, b, (((1,), (1,)), ((), ())), preferred_element_type=F32)


def _dot_tn(a, b):
    return lax.dot_general(a, b, (((0,), (0,)), ((), ())), preferred_element_type=F32)


def _row_sel(sub, tiles_per_batch, n_batch):
    return jnp.where(sub % tiles_per_batch == 0, n_batch, sub // tiles_per_batch)


def _ada_kernel(c_ref, w_ref, b_ref, o_ref):
    c = c_ref[...]
    a = c * jax.nn.sigmoid(c)
    o_ref[...] = jnp.dot(a, w_ref[...], preferred_element_type=F32,
                         precision=lax.Precision.HIGHEST) + b_ref[...]


def _ada_mods(c_rows, ada_w, ada_b):
    depth, d, n = ada_w.shape
    tn = 1024
    return pl.pallas_call(
        _ada_kernel,
        grid=(depth, n // tn),
        in_specs=[
            pl.BlockSpec((8, d), lambda l, j: (0, 0)),
            pl.BlockSpec((None, d, tn), lambda l, j: (l, 0, j)),
            pl.BlockSpec((None, 1, tn), lambda l, j: (l, 0, j)),
        ],
        out_specs=pl.BlockSpec((None, 8, tn), lambda l, j: (l, 0, j)),
        out_shape=jax.ShapeDtypeStruct((depth, 8, n), F32),
        compiler_params=_cparams(("arbitrary", "arbitrary")),
        name="ada_mods",
    )(c_rows, ada_w, ada_b.reshape(depth, 1, n))


def _norm_mod_kernel(x_ref, g_ref, shift_ref, scale_ref, h_ref):
    x = x_ref[...]
    y = x * lax.rsqrt(jnp.mean(x * x, axis=-1, keepdims=True) + NORM_EPS)
    h = y * g_ref[...] * (1.0 + scale_ref[...]) + shift_ref[...]
    h_ref[...] = h.astype(h_ref.dtype)


def _route_tile(lg, bias, run_ref):
    rows = lg.shape[0]
    z = lg.T[0:N_EXPERTS, :]
    eid = lax.broadcasted_iota(jnp.int32, (N_EXPERTS, 1), 0).astype(F32)
    neg = -jnp.inf
    far = float(LANES)
    cmax = lambda a: jnp.max(a, axis=0, keepdims=True)
    cmin = lambda a: jnp.min(a, axis=0, keepdims=True)
    csum = lambda a: jnp.sum(a, axis=0, keepdims=True)

    s = jax.nn.sigmoid(z)
    sel = s + bias
    best = None
    for g in range(N_GROUPS):
        ing = (eid >= g * EXPERTS_PER_GROUP) & (eid < (g + 1) * EXPERTS_PER_GROUP)
        v = jnp.where(ing, sel, neg)
        m1 = cmax(v)
        i1 = cmin(jnp.where(v == m1, eid, far))
        v2 = jnp.where(eid == i1, neg, v)
        m2 = cmax(v2)
        i2 = cmin(jnp.where(v2 == m2, eid, far))
        cand = (m1 + m2, i1, i2)
        if best is None:
            best = cand
        else:
            better = cand[0] > best[0]
            best = tuple(jnp.where(better, c, b) for c, b in zip(cand, best))
    _, e1, e2 = best
    is1 = eid == e1
    is2 = eid == e2
    s1 = csum(jnp.where(is1, s, 0.0))
    s2 = csum(jnp.where(is2, s, 0.0))
    w1 = s1 / (s1 + s2)
    w2 = s2 / (s1 + s2)
    onehot = jnp.where(is1 | is2, 1.0, 0.0)
    row = lax.broadcasted_iota(jnp.int32, (rows, rows), 0)
    col = lax.broadcasted_iota(jnp.int32, (rows, rows), 1)
    before = jnp.where(row < col, 1.0, 0.0).astype(BF16)
    prior = _dot(onehot.astype(BF16), before) + run_ref[...]
    r1 = csum(jnp.where(is1, prior, 0.0))
    r2 = csum(jnp.where(is2, prior, 0.0))
    run_ref[...] += jnp.sum(onehot, axis=1, keepdims=True)
    zero = jnp.zeros_like(e1)
    return jnp.concatenate([e1, e2, r1, r2, w1, w2, zero, zero], axis=0)


def _norm_mod_router_kernel(x_ref, g_ref, shift_ref, scale_ref, whi_ref, wlo_ref, rb_ref,
                            h_ref, route_ref, cnt_ref, run_ref):
    @pl.when(pl.program_id(0) == 0)
    def _():
        run_ref[...] = jnp.zeros(run_ref.shape, F32)

    x = x_ref[...]
    y = x * lax.rsqrt(jnp.mean(x * x, axis=-1, keepdims=True) + NORM_EPS)
    h = y * g_ref[...] * (1.0 + scale_ref[...]) + shift_ref[...]
    h_hi = h.astype(BF16)
    h_lo = (h - h_hi.astype(F32)).astype(BF16)
    bits = lax.bitcast_convert_type(h_hi.astype(F32), jnp.uint32)
    half = bits.shape[1] // 2
    h_ref[...] = (bits[:, :half] >> 16) | bits[:, half:]
    lg = _dot(h_hi, whi_ref[...]) + _dot(h_hi, wlo_ref[...]) + _dot(h_lo, whi_ref[...])
    route_ref[...] = _route_tile(lg, rb_ref[...], run_ref)
    cnt_ref[...] = run_ref[...]


def _norm_mod(xs, mods3, g3, layer, which, n_batch, tpb, router=None):
    t, d = xs.shape
    shift_k, scale_k = (0, 1) if which == 0 else (3, 4)
    nb = n_batch

    def mod_spec(k):
        return pl.BlockSpec((None, 1, d), lambda i: (layer * 8 + _row_sel(i, tpb, nb), 0, k))

    in_specs = [
        pl.BlockSpec((ROW_TILE, d), lambda i: (i, 0)),
        pl.BlockSpec((None, 1, d), lambda i: (layer * 2 + which, 0, 0)),
        mod_spec(shift_k),
        mod_spec(scale_k),
    ]
    args = [xs, g3, mods3, mods3]
    out_specs = [pl.BlockSpec((ROW_TILE, d), lambda i: (i, 0))]
    out_shape = [jax.ShapeDtypeStruct((t, d), BF16)]
    kern = _norm_mod_kernel
    scratch = []
    if router is not None:
        in_specs += [pl.BlockSpec((d, LANES), lambda i: (0, 0))] * 2 + [pl.BlockSpec((N_EXPERTS, 1), lambda i: (0, 0))]
        args += list(router)
        out_specs = [pl.BlockSpec((ROW_TILE, d // 2), lambda i: (i, 0)),
                     pl.BlockSpec((8, ROW_TILE), lambda i: (0, i)),
                     pl.BlockSpec((N_EXPERTS, 1), lambda i: (0, 0))]
        out_shape = [jax.ShapeDtypeStruct((t, d // 2), jnp.uint32),
                     jax.ShapeDtypeStruct((8, t), F32),
                     jax.ShapeDtypeStruct((N_EXPERTS, 1), F32)]
        scratch = [pltpu.VMEM((N_EXPERTS, 1), F32)]
        kern = _norm_mod_router_kernel
    return pl.pallas_call(
        kern,
        grid=(t // ROW_TILE,),
        in_specs=in_specs,
        out_specs=out_specs,
        out_shape=out_shape,
        scratch_shapes=scratch,
        compiler_params=_cparams(("arbitrary",)),
        name="norm_mod",
    )(*args)


def _rope_store(acc, cos_ref, sin_ref, o_ref):
    lane = lax.broadcasted_iota(jnp.int32, (1, LANES), 1)
    first = (lane % 64) < 32
    cos = cos_ref[...]
    sin = sin_ref[...]
    for c in range(acc.shape[1] // LANES):
        xc = acc[:, c * LANES:(c + 1) * LANES]
        sw = jnp.where(first, pltpu.roll(xc, 96, 1), pltpu.roll(xc, 32, 1))
        o_ref[:, c * LANES:(c + 1) * LANES] = (xc * cos + sw * sin).astype(o_ref.dtype)


def _mm_kernel(x_ref, w_ref, o_ref, wb_ref, *, n_q, q_scale):
    @pl.when(pl.program_id(1) == 0)
    def _():
        wb_ref[...] = w_ref[...].astype(BF16)

    acc = _dot(x_ref[...], wb_ref[...])
    j = pl.program_id(0)

    @pl.when(j < n_q)
    def _():
        o_ref[...] = (acc * q_scale).astype(o_ref.dtype)

    @pl.when(j >= n_q)
    def _():
        o_ref[...] = acc.astype(o_ref.dtype)


def _mm_rope_kernel(x_ref, w_ref, cos_ref, sin_ref, o_ref, wb_ref, *, n_q, q_scale, n_rope):
    @pl.when(pl.program_id(1) == 0)
    def _():
        wb_ref[...] = w_ref[...].astype(BF16)

    acc = _dot(x_ref[...], wb_ref[...])
    j = pl.program_id(0)

    @pl.when(j < n_q)
    def _():
        _rope_store(acc * q_scale, cos_ref, sin_ref, o_ref)

    @pl.when((j >= n_q) & (j < n_rope))
    def _():
        _rope_store(acc, cos_ref, sin_ref, o_ref)

    @pl.when(j >= n_rope)
    def _():
        o_ref[...] = acc.astype(o_ref.dtype)


def _mm_res_kernel(x_ref, w_ref, xs_ref, ga_ref, gb_ref, o_ref, wb_ref):
    @pl.when(pl.program_id(1) == 0)
    def _():
        wb_ref[...] = w_ref[...].astype(BF16)

    acc = _dot(x_ref[...], wb_ref[...])
    half = ROW_TILE
    o_ref[:half, :] = xs_ref[:half, :] + ga_ref[...] * acc[:half, :]
    o_ref[half:, :] = xs_ref[half:, :] + gb_ref[...] * acc[half:, :]


def _matmul(x, w, q_cols=0, q_scale=1.0, rope=None, n_rope_cols=0):
    t, k = x.shape
    n = w.shape[1]
    in_specs = [
        pl.BlockSpec((MM_TM, k), lambda j, i: (i, 0)),
        pl.BlockSpec((k, MM_TN), lambda j, i: (0, j)),
    ]
    args = [x, w]
    kern = functools.partial(_mm_kernel, n_q=q_cols // MM_TN, q_scale=q_scale)
    if rope is not None:
        in_specs += [pl.BlockSpec((MM_TM, LANES), lambda j, i: (i, 0))] * 2
        args += list(rope)
        kern = functools.partial(_mm_rope_kernel, n_q=q_cols // MM_TN, q_scale=q_scale,
                                 n_rope=n_rope_cols // MM_TN)
    return pl.pallas_call(
        kern,
        grid=(n // MM_TN, t // MM_TM),
        in_specs=in_specs,
        out_specs=pl.BlockSpec((MM_TM, MM_TN), lambda j, i: (i, j)),
        out_shape=jax.ShapeDtypeStruct((t, n), BF16),
        scratch_shapes=[pltpu.VMEM((k, MM_TN), BF16)],
        compiler_params=_cparams(("arbitrary", "arbitrary")),
        name="matmul",
    )(*args)


def _matmul_residual(x, w, xs, mods3, layer, gate_k, n_batch, tpb):
    t, k = x.shape
    n = w.shape[1]
    ncol = n // MM_TN
    nb = n_batch

    def gate_spec(half):
        return pl.BlockSpec(
            (None, 1, MM_TN),
            lambda j, i: (layer * 8 + _row_sel(2 * i + half, tpb, nb), 0, gate_k * ncol + j))

    return pl.pallas_call(
        _mm_res_kernel,
        grid=(ncol, t // MM_TM),
        in_specs=[
            pl.BlockSpec((MM_TM, k), lambda j, i: (i, 0)),
            pl.BlockSpec((k, MM_TN), lambda j, i: (0, j)),
            pl.BlockSpec((MM_TM, MM_TN), lambda j, i: (i, j)),
            gate_spec(0),
            gate_spec(1),
        ],
        out_specs=pl.BlockSpec((MM_TM, MM_TN), lambda j, i: (i, j)),
        out_shape=jax.ShapeDtypeStruct((t, n), F32),
        scratch_shapes=[pltpu.VMEM((k, MM_TN), BF16)],
        input_output_aliases={2: 0},
        compiler_params=_cparams(("arbitrary", "arbitrary")),
        name="matmul_residual",
    )(x, w, xs, mods3, mods3)


def _na_kernel(q_ref, k_ref, v_ref, tab_ref, o_ref, sl_ref, sc_ref, pl_ref, pc_ref, li_ref, *, rows):
    nq = NA_GROUP * GRID_W
    nk = NA_KROWS * GRID_W
    half_win = NA_WIN_ROWS // 2
    left = lax.broadcasted_iota(jnp.int32, (1, LANES), 1) < GRID_W

    s = _dot_nt(q_ref[0:CTX_LEN, :], k_ref[0:CTX_LEN, :])
    p = jnp.exp2(s - jnp.max(s, axis=-1, keepdims=True))
    l = jnp.sum(p, axis=-1, keepdims=True)
    o_ref[0:CTX_LEN, :] = (_dot(p.astype(BF16), v_ref[0:CTX_LEN, :]) / l).astype(o_ref.dtype)

    def geometry(g):
        r0 = jnp.asarray(g, jnp.int32) * NA_GROUP
        start = jnp.clip(r0 - half_win, 0, rows - NA_KROWS)
        q0 = pl.multiple_of(CTX_LEN + r0 * GRID_W, nq)
        k0 = pl.multiple_of(CTX_LEN + start * GRID_W, GRID_W)
        return r0, start, q0, k0

    def stage_a(g, par):
        _, _, q0, k0 = geometry(g)
        q = q_ref[pl.ds(q0, nq), :]
        sl_ref[par] = _dot_nt(q, k_ref[pl.ds(k0, nk), :])
        sc_ref[par] = _dot_nt(q, k_ref[0:CTX_LEN, :])

    def stage_b(g, par):
        r0, start, _, _ = geometry(g)
        for i in range(NA_GROUP):
            r = r0 + i
            rs = jnp.clip(r - half_win, 0, rows - NA_WIN_ROWS)
            rsl = slice(i * GRID_W, (i + 1) * GRID_W)
            tiles = []
            for jp in range(NA_KROWS // 2):
                kr = start + 2 * jp
                ok0 = ((kr >= rs) & (kr < rs + NA_WIN_ROWS)).astype(jnp.int32)
                ok1 = ((kr + 1 >= rs) & (kr + 1 < rs + NA_WIN_ROWS)).astype(jnp.int32)
                e = jnp.clip(kr - r + (NA_WIN_ROWS - 1) + 2, 0, 2 * NA_WIN_ROWS + 1)
                ok = jnp.where(left, ok0, ok1) > 0
                tiles.append(jnp.where(ok, tab_ref[e], NEG_BIG))
            sl = sl_ref[par, rsl, :] + jnp.concatenate(tiles, axis=1)
            sc = sc_ref[par, rsl, :]
            m = jnp.maximum(jnp.max(sl, axis=-1, keepdims=True), jnp.max(sc, axis=-1, keepdims=True))
            pl_i = jnp.exp2(sl - m)
            pc_i = jnp.exp2(sc - m)
            l_i = jnp.sum(pl_i, axis=-1, keepdims=True) + jnp.sum(pc_i, axis=-1, keepdims=True)
            pl_ref[par, rsl, :] = pl_i.astype(BF16)
            pc_ref[par, rsl, :] = pc_i.astype(BF16)
            li_ref[par, rsl, :] = jnp.broadcast_to(1.0 / l_i, (GRID_W, LANES))

    def stage_c(g, par):
        _, _, q0, k0 = geometry(g)
        o = _dot(pl_ref[par], v_ref[pl.ds(k0, nk), :]) + _dot(pc_ref[par], v_ref[0:CTX_LEN, :])
        o_ref[pl.ds(q0, nq), :] = (o * li_ref[par]).astype(o_ref.dtype)

    n_groups = rows // NA_GROUP
    stage_a(0, 0)
    stage_a(1, 1)
    stage_b(0, 0)

    def pair(u, carry):
        t = 2 + 2 * u
        stage_a(t, 0)
        stage_b(t - 1, 1)
        stage_c(t - 2, 0)
        stage_a(t + 1, 1)
        stage_b(t, 0)
        stage_c(t - 1, 1)
        return carry

    lax.fori_loop(0, (n_groups - 2) // 2, pair, 0)
    stage_b(n_groups - 1, 1)
    stage_c(n_groups - 2, 0)
    stage_c(n_groups - 1, 1)


def _na_bias_table(rpb):
    h = rpb.shape[0]
    qc = np.arange(GRID_W)[:, None]
    kc = np.arange(GRID_W)[None, :]
    cs = np.clip(qc - NA_WIN_COLS // 2, 0, GRID_W - NA_WIN_COLS)
    col_ok = (kc >= cs) & (kc < cs + NA_WIN_COLS)
    dc = np.clip(kc - qc + (NA_WIN_COLS - 1), 0, 2 * NA_WIN_COLS - 2)
    a = jnp.take(rpb.astype(F32), jnp.asarray(dc.reshape(-1)), axis=2)
    a = a.reshape(h, 2 * NA_WIN_ROWS - 1, GRID_W, GRID_W) * LOG2E
    a = jnp.where(jnp.asarray(col_ok)[None, None], a, NEG_BIG)
    a = jnp.pad(a, ((0, 0), (2, 2), (0, 0), (0, 0)), constant_values=NEG_BIG)
    return jnp.concatenate([a[:, :-1], a[:, 1:]], axis=-1)


def _na_attention(qkv, rpb, n_batch, tb):
    t = qkv.shape[0]
    rows = (tb - CTX_LEN) // GRID_W
    assert rows % (2 * NA_GROUP) == 0 and rows >= NA_KROWS
    nq, nk = NA_GROUP * GRID_W, NA_KROWS * GRID_W
    table = _na_bias_table(rpb)
    hh = NA_HEADS
    return pl.pallas_call(
        functools.partial(_na_kernel, rows=rows),
        grid=(hh, n_batch),
        in_specs=[
            pl.BlockSpec((tb, LANES), lambda h, b: (b, h)),
            pl.BlockSpec((tb, LANES), lambda h, b: (b, hh + h)),
            pl.BlockSpec((tb, LANES), lambda h, b: (b, 2 * hh + h)),
            pl.BlockSpec((None, 2 * NA_WIN_ROWS + 2, GRID_W, LANES), lambda h, b: (h, 0, 0, 0)),
        ],
        out_specs=pl.BlockSpec((tb, LANES), lambda h, b: (b, h)),
        out_shape=jax.ShapeDtypeStruct((t, D_MODEL), BF16),
        scratch_shapes=[
            pltpu.VMEM((2, nq, nk), F32),
            pltpu.VMEM((2, nq, CTX_LEN), F32),
            pltpu.VMEM((2, nq, nk), BF16),
            pltpu.VMEM((2, nq, CTX_LEN), BF16),
            pltpu.VMEM((2, nq, LANES), F32),
        ],
        compiler_params=_cparams(("arbitrary", "arbitrary")),
        name="na_attention",
    )(qkv, qkv, qkv, table)


def _da_kernel(lam_ref, g_ref, q_ref, k_ref, v_ref, o_ref, acc_ref, m_ref, l_ref,
               s0_ref, s1_ref, p0_ref, p1_ref, a0_ref, a1_ref, qbd_ref, *, lam_init, n_kv_chunks):
    d = DIFF_HEAD_DIM
    i = pl.program_id(2)
    lam = lam_ref[...]
    lam_full = (jnp.exp(jnp.sum(lam[0:1] * lam[1:2], axis=-1, keepdims=True))
                - jnp.exp(jnp.sum(lam[2:3] * lam[3:4], axis=-1, keepdims=True)) + lam_init)
    s_bufs, p_bufs, a_bufs = (s0_ref, s1_ref), (p0_ref, p1_ref), (a0_ref, a1_ref)

    def tile(q0, nq, n_lat_chunks):
        acc_ref[:, 0:nq, :] = jnp.zeros((2, nq, 2 * d), F32)
        m_ref[:, 0:nq, :] = jnp.full((2, nq, LANES), NEG_BIG, F32)
        l_ref[:, 0:nq, :] = jnp.zeros((2, nq, LANES), F32)

        def key_rows(e):
            if isinstance(e, int):
                return (0, CTX_LEN) if e == 0 else (CTX_LEN + (e - 1) * DA_TK, DA_TK)
            return pl.multiple_of(CTX_LEN + (e - 1) * DA_TK, ROW_TILE), DA_TK

        q = q_ref[pl.ds(q0, nq), :]
        zero = jnp.zeros((nq, d), q.dtype)
        qbd_ref[0:nq, :] = jnp.concatenate([q[:, :d], zero], axis=1)
        qbd_ref[nq:2 * nq, :] = jnp.concatenate([zero, q[:, d:]], axis=1)

        def stage_a(e, par):
            k0, nk = key_rows(e)
            s = _dot_nt(qbd_ref[0:2 * nq, :], k_ref[pl.ds(k0, nk), :])
            for c in range(2):
                s_bufs[par][c, 0:nq, 0:nk] = s[c * nq:(c + 1) * nq, :]

        def stage_b(e, par):
            _, nk = key_rows(e)
            s_ref, p_ref, a_ref = s_bufs[par], p_bufs[par], a_bufs[par]
            for c in range(2):
                for sb in range(nq // DA_SUB):
                    rows = slice(sb * DA_SUB, (sb + 1) * DA_SUB)
                    blocks = [s_ref[c, rows, j * LANES:(j + 1) * LANES] for j in range(nk // LANES)]
                    m_prev = m_ref[c, rows, :]
                    m_new = jnp.maximum(
                        m_prev, jnp.max(functools.reduce(jnp.maximum, blocks), axis=-1, keepdims=True))
                    alpha = jnp.exp2(m_prev - m_new)
                    p = [jnp.exp2(blk - m_new) for blk in blocks]
                    l_ref[c, rows, :] = alpha * l_ref[c, rows, :] + functools.reduce(jnp.add, p)
                    for j, pj in enumerate(p):
                        p_ref[c, rows, j * LANES:(j + 1) * LANES] = pj.astype(BF16)
                    a_ref[c, rows, :] = alpha
                    m_ref[c, rows, :] = m_new

        def stage_c(e, par):
            k0, nk = key_rows(e)
            v = v_ref[pl.ds(k0, nk), :]
            p = jnp.concatenate([p_bufs[par][0, 0:nq, 0:nk], p_bufs[par][1, 0:nq, 0:nk]], axis=0)
            pv_all = _dot(p, v)
            for c in range(2):
                pv = pv_all[c * nq:(c + 1) * nq, :]
                alpha = a_bufs[par][c, 0:nq, :]
                acc_ref[c, 0:nq, 0:LANES] = alpha * acc_ref[c, 0:nq, 0:LANES] + pv[:, :LANES]
                acc_ref[c, 0:nq, LANES:] = alpha * acc_ref[c, 0:nq, LANES:] + pv[:, LANES:]

        n = n_lat_chunks
        if n == 0:
            stage_a(0, 0)
            stage_b(0, 0)
            stage_c(0, 0)
        else:
            assert n % 2 == 0
            stage_a(0, 0)
            stage_a(1, 1)
            stage_b(0, 0)
            stage_a(2, 0)
            stage_b(1, 1)
            stage_c(0, 0)

            def pair(u, carry):
                t = 3 + 2 * u
                stage_a(t, 1)
                stage_b(t - 1, 0)
                stage_c(t - 2, 1)
                stage_a(t + 1, 0)
                stage_b(t, 1)
                stage_c(t - 1, 0)
                return carry

            lax.fori_loop(0, (n - 2) // 2, pair, 0)
            stage_b(n, 0)
            stage_c(n - 1, 1)
            stage_c(n, 0)

        r0 = 1.0 / jnp.sum(l_ref[0, 0:nq, :], axis=-1, keepdims=True)
        r1 = lam_full / jnp.sum(l_ref[1, 0:nq, :], axis=-1, keepdims=True)
        o_a = acc_ref[0, 0:nq, 0:LANES] * r0 - acc_ref[1, 0:nq, 0:LANES] * r1
        o_b = acc_ref[0, 0:nq, LANES:] * r0 - acc_ref[1, 0:nq, LANES:] * r1
        ms = (jnp.sum(o_a * o_a, axis=-1, keepdims=True) + jnp.sum(o_b * o_b, axis=-1, keepdims=True)) / (2 * d)
        inv = lax.rsqrt(ms + NORM_EPS) * (1.0 - lam_init)
        g = g_ref[...]
        o_ref[pl.ds(q0, nq), 0:LANES] = (o_a * inv * g[:, :LANES]).astype(o_ref.dtype)
        o_ref[pl.ds(q0, nq), LANES:] = (o_b * inv * g[:, LANES:]).astype(o_ref.dtype)

    @pl.when(i == 0)
    def _():
        tile(0, CTX_LEN, 0)

    @pl.when(i > 0)
    def _():
        tile(pl.multiple_of(CTX_LEN + (i - 1) * DA_TQ, ROW_TILE), DA_TQ, n_kv_chunks)


def _diff_attention(qkv, lam, subln_g, layer_idx, n_batch, tb):
    t = qkv.shape[0]
    hh = DIFF_HEADS
    w = 2 * DIFF_HEAD_DIM
    n_lat = (tb - CTX_LEN) // DA_TQ
    lam_init = 0.8 - 0.6 * math.exp(-0.3 * layer_idx)
    return pl.pallas_call(
        functools.partial(_da_kernel, lam_init=lam_init, n_kv_chunks=(tb - CTX_LEN) // DA_TK),
        grid=(n_batch, hh, 1 + n_lat),
        in_specs=[
            pl.BlockSpec((4, DIFF_HEAD_DIM), lambda b, h, i: (0, 0)),
            pl.BlockSpec((1, w), lambda b, h, i: (0, 0)),
            pl.BlockSpec((tb, w), lambda b, h, i: (b, h)),
            pl.BlockSpec((tb, w), lambda b, h, i: (b, hh + h)),
            pl.BlockSpec((tb, w), lambda b, h, i: (b, 2 * hh + h)),
        ],
        out_specs=pl.BlockSpec((tb, w), lambda b, h, i: (b, h)),
        out_shape=jax.ShapeDtypeStruct((t, D_MODEL), BF16),
        scratch_shapes=[
            pltpu.VMEM((2, DA_TQ, w), F32),
            pltpu.VMEM((2, DA_TQ, LANES), F32),
            pltpu.VMEM((2, DA_TQ, LANES), F32),
            pltpu.VMEM((2, DA_TQ, DA_TK), F32),
            pltpu.VMEM((2, DA_TQ, DA_TK), F32),
            pltpu.VMEM((2, DA_TQ, DA_TK), BF16),
            pltpu.VMEM((2, DA_TQ, DA_TK), BF16),
            pltpu.VMEM((2, DA_TQ, LANES), F32),
            pltpu.VMEM((2, DA_TQ, LANES), F32),
            pltpu.VMEM((2 * DA_TQ, w), BF16),
        ],
        compiler_params=_cparams(("arbitrary", "arbitrary", "arbitrary")),
        name="diff_attention",
    )(lam.astype(F32), subln_g.astype(F32).reshape(1, w), qkv, qkv, qkv)


def _hg_kernel(q_ref, i_ref, ff_ref, fb_ref, lb_ref, o_ref, acc_ref, stf_ref, stb_ref, xb_ref, xk_ref, xv_ref, *,
               n_groups):
    c = HGRN_CHUNK
    rg = HG_GROUP * c
    dk = HGRN_DK
    acc_ref[...] = jnp.zeros(acc_ref.shape, F32)
    stf_ref[...] = jnp.zeros(stf_ref.shape, F32)
    stb_ref[...] = jnp.zeros(stb_ref.shape, F32)
    row = lax.broadcasted_iota(jnp.int32, (rg, rg), 0)
    col = lax.broadcasted_iota(jnp.int32, (rg, rg), 1)
    same = (row // c) == (col // c)
    lower = same & (row >= col)
    upper = same & (col >= row)
    lower_b = lower.astype(BF16)
    upper_b = upper.astype(BF16)
    chunk_of_row = lax.broadcasted_iota(jnp.int32, (rg, 1), 0) // c
    lb_all = lb_ref[...]

    def per_chunk_rows(b, r):
        return jnp.concatenate(
            [jnp.broadcast_to(b[g * c + r:g * c + r + 1, :], (c, dk)) for g in range(HG_GROUP)], axis=0)

    def intra_exact(q, k, v, bcum, forward):
        xb_ref[...] = bcum
        xk_ref[...] = k
        xv_ref[...] = v.astype(F32)
        rowi = lax.broadcasted_iota(jnp.int32, (c, 1), 0)
        outs = []
        for g in range(HG_GROUP):
            bc_g = bcum[g * c:(g + 1) * c, :]
            q_g = q[g * c:(g + 1) * c, :]

            def column(s, acc, g=g, bc_g=bc_g, q_g=q_g):
                live = (rowi >= s) if forward else (rowi <= s)
                bs = xb_ref[pl.ds(g * c + s, 1), :]
                w = jnp.where(live, jnp.exp(jnp.where(live, bc_g - bs, 0.0)), 0.0) * q_g
                a = jnp.sum(w * xk_ref[pl.ds(g * c + s, 1), :], axis=-1, keepdims=True)
                return acc + a * xv_ref[pl.ds(g * c + s, 1), :]

            outs.append(lax.fori_loop(0, c, column, jnp.zeros((c, dk), F32)))
        return jnp.concatenate(outs, axis=0)

    def group(hd, gidx, f_ref, lb_row, mask, mask_b, st_ref, end_row, order, exact):
        r0 = pl.multiple_of(gidx * rg, rg)
        cols = slice(hd * dk, (hd + 1) * dk)
        lb = lb_all[lb_row:lb_row + 1, cols]
        qr = q_ref[pl.ds(r0, rg), cols].astype(F32)
        q = qr * jax.nn.sigmoid(qr) * (dk ** -0.5)
        v = i_ref[pl.ds(r0, rg), cols]
        fr = f_ref[pl.ds(r0, rg), cols].astype(F32)
        fg = lb + (1.0 - lb) * jax.nn.sigmoid(fr)
        k = 1.0 - fg
        lf = jnp.log(fg)
        hi = lf.astype(BF16)
        rem = lf - hi.astype(F32)
        mid = rem.astype(BF16)
        lo = (rem - mid.astype(F32)).astype(BF16)
        cs = _dot(mask_b, jnp.concatenate([hi, mid, lo], axis=1))
        bcum = cs[:, 0:dk] + cs[:, dk:2 * dk] + cs[:, 2 * dk:]
        b_end = per_chunk_rows(bcum, end_row)
        if exact:
            o_intra = intra_exact(q, k, v, bcum, end_row != 0)
        else:
            b_mid = per_chunk_rows(bcum, c // 2)
            qt = (q * jnp.exp(bcum - b_mid)).astype(BF16)
            kt = (k * jnp.exp(b_mid - bcum)).astype(BF16)
            a = jnp.where(mask, _dot_nt(qt, kt), 0.0).astype(BF16)
            o_intra = _dot(a, v)
        k_end = k * jnp.exp(b_end - bcum)
        k_exp = jnp.concatenate(
            [jnp.where(chunk_of_row == g, k_end, 0.0) for g in range(HG_GROUP)], axis=1).astype(BF16)
        u = _dot_tn(v, k_exp)
        st = st_ref[hd]
        st_in = [None] * HG_GROUP
        for g in order:
            st_in[g] = st.astype(BF16)
            st = st * jnp.exp(bcum[g * c + end_row:g * c + end_row + 1, :]) + u[:, g * dk:(g + 1) * dk]
        st_ref[hd] = st
        oi = _dot_nt((q * jnp.exp(bcum)).astype(BF16), jnp.concatenate(st_in, axis=0))
        o_inter = jnp.concatenate(
            [oi[g * c:(g + 1) * c, g * dk:(g + 1) * dk] for g in range(HG_GROUP)], axis=0)
        acc_ref[pl.ds(r0, rg), cols] += o_intra + o_inter

    fwd_order = list(range(HG_GROUP))

    def scan(exact):
        def body(n, carry):
            nb = jnp.where(n == 0, 0, n_groups - n)
            for hd in range(HG_HEADS):
                group(hd, n, ff_ref, 0, lower, lower_b, stf_ref, c - 1, fwd_order, exact)
                group(hd, nb, fb_ref, 1, upper, upper_b, stb_ref, 0, fwd_order[::-1], exact)
            return carry

        lax.fori_loop(0, n_groups, body, 0)

    def min_forget(f_ref, lb_row):
        lb = lb_all[lb_row:lb_row + 1, :]
        f_min = jnp.min(f_ref[...].astype(F32), axis=0, keepdims=True)
        return jnp.min(lb + (1.0 - lb) * jax.nn.sigmoid(f_min))

    safe = jnp.minimum(min_forget(ff_ref, 0), min_forget(fb_ref, 1)) >= HG_SAFE_FORGET

    @pl.when(safe)
    def _():
        scan(False)

    @pl.when(jnp.logical_not(safe))
    def _():
        scan(True)

    o_ref[...] = acc_ref[...].astype(o_ref.dtype)


def _hgrn_scan(proj, lb2, n_batch, tb):
    t = proj.shape[0]
    hh = HGRN_HEADS // HG_HEADS
    wd = HG_HEADS * LANES
    rg = HG_GROUP * HGRN_CHUNK
    assert CTX_LEN == rg and tb % rg == 0

    def col(k):
        return pl.BlockSpec((tb, wd), lambda b, h: (b, k * hh + h))

    return pl.pallas_call(
        functools.partial(_hg_kernel, n_groups=tb // rg),
        grid=(n_batch, hh),
        in_specs=[col(0), col(1), col(3), col(4), pl.BlockSpec((2, wd), lambda b, h: (0, h))],
        out_specs=pl.BlockSpec((tb, wd), lambda b, h: (b, h)),
        out_shape=jax.ShapeDtypeStruct((t, D_MODEL), BF16),
        scratch_shapes=[pltpu.VMEM((tb, wd), F32),
                        pltpu.VMEM((HG_HEADS, LANES, LANES), F32), pltpu.VMEM((HG_HEADS, LANES, LANES), F32),
                        pltpu.VMEM((rg, LANES), F32), pltpu.VMEM((rg, LANES), F32), pltpu.VMEM((rg, LANES), F32)],
        compiler_params=_cparams(("arbitrary", "arbitrary")),
        name="hgrn_scan",
    )(proj, proj, proj, proj, lb2)


def _hg_finish_kernel(o_ref, g_ref, ng_ref, h_ref):
    o = o_ref[...].astype(F32)
    y = o * lax.rsqrt(jnp.mean(o * o, axis=-1, keepdims=True) + NORM_EPS) * ng_ref[...]
    g = g_ref[...].astype(F32)
    h_ref[...] = (y * (g * jax.nn.sigmoid(g))).astype(h_ref.dtype)


def _hgrn_finish(o, proj, norm_g):
    t, d = o.shape
    return pl.pallas_call(
        _hg_finish_kernel,
        grid=(t // ROW_TILE,),
        in_specs=[
            pl.BlockSpec((ROW_TILE, d), lambda i: (i, 0)),
            pl.BlockSpec((ROW_TILE, d), lambda i: (i, 2)),
            pl.BlockSpec((1, d), lambda i: (0, 0)),
        ],
        out_specs=pl.BlockSpec((ROW_TILE, d), lambda i: (i, 0)),
        out_shape=jax.ShapeDtypeStruct((t, d), BF16),
        compiler_params=_cparams(("arbitrary",)),
        name="hgrn_finish",
    )(o, proj, norm_g.astype(F32).reshape(1, d))


def _moe_kernel(be_ref, first_ref, nxt_ref, nu_ref, tok_ref, x_hbm, wg_hbm, wu_hbm, wd_hbm, o_ref,
                xbuf, sg_ref, su_ref, sd_ref, bg_ref, bu_ref, bd_ref, sem, xsem, *, layer):
    b = pl.program_id(0)
    n_used = nu_ref[0]

    def weight_copies(e):
        return (pltpu.make_async_copy(wg_hbm.at[layer, e], sg_ref, sem.at[0]),
                pltpu.make_async_copy(wu_hbm.at[layer, e], su_ref, sem.at[1]),
                pltpu.make_async_copy(wd_hbm.at[layer, e], sd_ref, sem.at[2]))

    def start_gather(blk, slot):
        base = blk * MOE_TM
        for r in range(MOE_TM):
            pltpu.make_async_copy(x_hbm.at[pl.ds(tok_ref[base + r], 1)], xbuf.at[slot, pl.ds(r, 1)],
                                  xsem.at[slot]).start()

    def wait_gather(slot):
        pltpu.make_async_copy(x_hbm.at[pl.ds(0, MOE_TM)], xbuf.at[slot], xsem.at[slot]).wait()

    @pl.when(b == 0)
    def _():
        for cp in weight_copies(be_ref[0]):
            cp.start()
        start_gather(0, 0)
        start_gather(1, 1)

    @pl.when(first_ref[b] == 1)
    def _():
        for cp in weight_copies(be_ref[b]):
            cp.wait()
        for src, dst in ((sg_ref, bg_ref), (su_ref, bu_ref), (sd_ref, bd_ref)):
            slab = src.shape[0] // MOE_CAST_SLABS

            def cast(r, carry, src=src, dst=dst, slab=slab):
                r0 = pl.multiple_of(r * slab, slab)
                dst[pl.ds(r0, slab), :] = src[pl.ds(r0, slab), :].astype(BF16)
                return carry

            lax.fori_loop(0, MOE_CAST_SLABS, cast, 0)

        @pl.when(nxt_ref[b] >= 0)
        def _():
            for cp in weight_copies(nxt_ref[b]):
                cp.start()

    @pl.when(b < n_used)
    def _():
        slot = b % MOE_XBUFS
        wait_gather(slot)
        xu = xbuf[slot]
        x = jnp.concatenate(
            [lax.bitcast_convert_type(xu << 16, F32).astype(BF16),
             lax.bitcast_convert_type(xu & jnp.uint32(0xFFFF0000), F32).astype(BF16)], axis=1)
        g = _dot(x, bg_ref[...])
        u = _dot(x, bu_ref[...])
        h = (g * jax.nn.sigmoid(g) * u).astype(BF16)
        ybits = lax.bitcast_convert_type(_dot(h, bd_ref[...]).astype(BF16).astype(F32), jnp.uint32)
        yh = ybits.shape[1] // 2
        o_ref[...] = (ybits[:, :yh] >> 16) | ybits[:, yh:]
        start_gather(b + 2, (b + 2) % MOE_XBUFS)

    @pl.when(b >= n_used)
    def _():
        @pl.when(b == n_used)
        def _():
            wait_gather(b % MOE_XBUFS)
            wait_gather((b + 1) % MOE_XBUFS)

        o_ref[...] = jnp.zeros(o_ref.shape, o_ref.dtype)


def _moe_experts(hp, slot_tok, blk_expert, blk_first, blk_next, n_used, w_gate, w_up, w_down, layer):
    d = 2 * hp.shape[1]
    f = w_gate.shape[3]
    n_slots = slot_tok.shape[0]
    n_blocks = n_slots // MOE_TM
    grid_spec = pltpu.PrefetchScalarGridSpec(
        num_scalar_prefetch=5,
        grid=(n_blocks,),
        in_specs=[pl.BlockSpec(memory_space=pl.ANY)] * 4,
        out_specs=pl.BlockSpec((MOE_TM, d // 2), lambda b, *_: (b, 0)),
        scratch_shapes=[
            pltpu.VMEM((MOE_XBUFS, MOE_TM, d // 2), jnp.uint32),
            pltpu.VMEM((d, f), F32), pltpu.VMEM((d, f), F32), pltpu.VMEM((f, d), F32),
            pltpu.VMEM((d, f), BF16), pltpu.VMEM((d, f), BF16), pltpu.VMEM((f, d), BF16),
            pltpu.SemaphoreType.DMA((3,)),
            pltpu.SemaphoreType.DMA((MOE_XBUFS,)),
        ],
    )
    return pl.pallas_call(
        functools.partial(_moe_kernel, layer=layer),
        grid_spec=grid_spec,
        out_shape=jax.ShapeDtypeStruct((n_slots, d // 2), jnp.uint32),
        compiler_params=_cparams(("arbitrary",)),
        name="moe_experts",
    )(blk_expert, blk_first, blk_next, n_used, slot_tok, hp, w_gate, w_up, w_down)


def _route(route, counts_col, n_blocks):
    t = route.shape[1]
    e_idx = route[0:TOP_K].T.astype(jnp.int32)
    rank = route[TOP_K:2 * TOP_K].T.astype(jnp.int32)
    w = route[2 * TOP_K:3 * TOP_K].T
    counts = counts_col[:, 0].astype(jnp.int32)
    pcounts = (counts + MOE_TM - 1) // MOE_TM * MOE_TM
    pends = jnp.cumsum(pcounts)
    pstarts = pends - pcounts
    e_ids = jnp.arange(N_EXPERTS, dtype=jnp.int32)
    start_of = jnp.sum(jnp.where(e_idx[..., None] == e_ids, pstarts, 0), axis=-1)
    dest = (start_of + rank).reshape(-1).astype(jnp.int32)
    flat_t = jnp.arange(t * TOP_K, dtype=jnp.int32) // TOP_K
    slot_tok = jnp.zeros((n_blocks * MOE_TM,), jnp.int32).at[dest].set(flat_t)
    blk_row0 = jnp.arange(n_blocks, dtype=jnp.int32) * MOE_TM
    blk_expert = jnp.minimum(
        jnp.sum((pends[None, :] <= blk_row0[:, None]).astype(jnp.int32), axis=1), N_EXPERTS - 1).astype(jnp.int32)
    n_used = (pends[-1] // MOE_TM).astype(jnp.int32).reshape(1)
    blk_ids = jnp.arange(n_blocks, dtype=jnp.int32)
    prev_expert = jnp.concatenate([jnp.full((1,), -1, jnp.int32), blk_expert[:-1]])
    blk_first = ((blk_ids < n_used[0]) & (blk_expert != prev_expert)).astype(jnp.int32)
    e_ids = jnp.arange(N_EXPERTS, dtype=jnp.int32)
    later_used = (e_ids[None, :] > e_ids[:, None]) & (counts[None, :] > 0)
    next_used = jnp.min(jnp.where(later_used, e_ids[None, :], N_EXPERTS), axis=1)
    next_used = jnp.where(next_used == N_EXPERTS, -1, next_used).astype(jnp.int32)
    blk_next = next_used[blk_expert]
    return w, dest.reshape(t, TOP_K), slot_tok, (blk_expert, blk_first, blk_next, n_used)


def _unpack_pairs(yu):
    return jnp.concatenate([lax.bitcast_convert_type(yu << 16, F32),
                            lax.bitcast_convert_type(yu & jnp.uint32(0xFFFF0000), F32)], axis=1)


def _moe_res_kernel(dest_ref, xs_ref, w_ref, gate_ref, g_ref, shift_ref, scale_ref, yb_hbm, o_ref, h_ref,
                    ybuf, ysem, *, n_tiles):
    i = pl.program_id(0)

    def start_gather(tile, slot):
        base = tile * (ROW_TILE * TOP_K)
        for r in range(ROW_TILE):
            for k in range(TOP_K):
                pltpu.make_async_copy(yb_hbm.at[pl.ds(dest_ref[base + r * TOP_K + k], 1)],
                                      ybuf.at[slot, pl.ds(k * ROW_TILE + r, 1)], ysem.at[slot]).start()

    def wait_gather(slot):
        pltpu.make_async_copy(yb_hbm.at[pl.ds(0, TOP_K * ROW_TILE)], ybuf.at[slot], ysem.at[slot]).wait()

    @pl.when(i == 0)
    def _():
        start_gather(0, 0)
        start_gather(1, 1)

    slot = i % MOE_XBUFS
    wait_gather(slot)
    w = w_ref[...]
    y = (w[:, 0:1] * _unpack_pairs(ybuf[slot, 0:ROW_TILE, :])
         + w[:, 1:2] * _unpack_pairs(ybuf[slot, ROW_TILE:, :]))
    x = xs_ref[...] + gate_ref[...] * y
    o_ref[...] = x
    xn = x * lax.rsqrt(jnp.mean(x * x, axis=-1, keepdims=True) + NORM_EPS)
    h_ref[...] = (xn * g_ref[...] * (1.0 + scale_ref[...]) + shift_ref[...]).astype(h_ref.dtype)
    start_gather(jnp.minimum(i + 2, n_tiles - 1), (i + 2) % MOE_XBUFS)

    @pl.when(i == n_tiles - 1)
    def _():
        wait_gather((i + 1) % MOE_XBUFS)
        wait_gather((i + 2) % MOE_XBUFS)


def _moe_res_final_kernel(xs_ref, y0_ref, y1_ref, w_ref, gate_ref, fg_ref, o_ref):
    w = w_ref[...]
    y = w[:, 0:1] * _unpack_pairs(y0_ref[...]) + w[:, 1:2] * _unpack_pairs(y1_ref[...])
    x = xs_ref[...] + gate_ref[...] * y
    o_ref[...] = x * lax.rsqrt(jnp.mean(x * x, axis=-1, keepdims=True) + NORM_EPS) * fg_ref[...]


def _moe_residual(xs, yb, dest, w, mods3, g3, layer, n_batch, tpb):
    t, d = xs.shape
    nb = n_batch
    n_tiles = t // ROW_TILE
    assert n_tiles >= 2
    row = lambda i, *_: (i, 0)

    def mod_spec(lyr, k):
        return pl.BlockSpec((None, 1, d), lambda i, *_: (lyr * 8 + _row_sel(i, tpb, nb), 0, k))

    grid_spec = pltpu.PrefetchScalarGridSpec(
        num_scalar_prefetch=1,
        grid=(n_tiles,),
        in_specs=[
            pl.BlockSpec((ROW_TILE, d), row),
            pl.BlockSpec((ROW_TILE, TOP_K), row),
            mod_spec(layer, 5),
            pl.BlockSpec((None, 1, d), lambda i, *_: ((layer + 1) * 2, 0, 0)),
            mod_spec(layer + 1, 0),
            mod_spec(layer + 1, 1),
            pl.BlockSpec(memory_space=pl.ANY),
        ],
        out_specs=[pl.BlockSpec((ROW_TILE, d), row), pl.BlockSpec((ROW_TILE, d), row)],
        scratch_shapes=[pltpu.VMEM((MOE_XBUFS, TOP_K * ROW_TILE, d // 2), jnp.uint32),
                        pltpu.SemaphoreType.DMA((MOE_XBUFS,))],
    )
    return pl.pallas_call(
        functools.partial(_moe_res_kernel, n_tiles=n_tiles),
        grid_spec=grid_spec,
        out_shape=[jax.ShapeDtypeStruct((t, d), F32), jax.ShapeDtypeStruct((t, d), BF16)],
        input_output_aliases={1: 0},
        compiler_params=_cparams(("arbitrary",)),
        name="moe_residual",
    )(dest.reshape(-1), xs, w, mods3, g3, mods3, mods3, yb)


def _moe_residual_final(xs, y0, y1, w, mods3, layer, final_g, n_batch, tpb):
    t, d = xs.shape
    lat = tpb - CTX_LEN // ROW_TILE
    src = lambda i: ((i // lat) * tpb + CTX_LEN // ROW_TILE + i % lat, 0)
    return pl.pallas_call(
        _moe_res_final_kernel,
        grid=(n_batch * lat,),
        in_specs=[
            pl.BlockSpec((ROW_TILE, d), src),
            pl.BlockSpec((ROW_TILE, d // 2), src),
            pl.BlockSpec((ROW_TILE, d // 2), src),
            pl.BlockSpec((ROW_TILE, TOP_K), src),
            pl.BlockSpec((None, 1, d), lambda i: (layer * 8 + i // lat, 0, 5)),
            pl.BlockSpec((1, d), lambda i: (0, 0)),
        ],
        out_specs=pl.BlockSpec((ROW_TILE, d), lambda i: (i, 0)),
        out_shape=jax.ShapeDtypeStruct((n_batch * lat * ROW_TILE, d), F32),
        compiler_params=_cparams(("arbitrary",)),
        name="moe_residual_final",
    )(xs, y0, y1, w, mods3, final_g.astype(F32).reshape(1, d))


def _rope_tables(n_batch, seq):
    quarter = DIFF_HEAD_DIM // 4
    tpos = jnp.arange(seq)
    rowp = (tpos // GRID_W).astype(F32)
    colp = (tpos % GRID_W).astype(F32)
    inv = ROPE_THETA ** (-jnp.arange(quarter, dtype=F32) / quarter)
    ang_r = rowp[:, None] * inv[None, :]
    ang_c = colp[:, None] * inv[None, :]
    cos = jnp.concatenate([jnp.cos(ang_r)] * 2 + [jnp.cos(ang_c)] * 2, axis=-1)
    sin = jnp.concatenate([-jnp.sin(ang_r), jnp.sin(ang_r), -jnp.sin(ang_c), jnp.sin(ang_c)], axis=-1)
    cos = jnp.concatenate([jnp.ones((CTX_LEN, LANES), F32), cos], axis=0)
    sin = jnp.concatenate([jnp.zeros((CTX_LEN, LANES), F32), sin], axis=0)
    return jnp.tile(cos, (n_batch, 1)), jnp.tile(sin, (n_batch, 1))


def kernel(x, c, ctx, c_ctx, ada_w, ada_b, norm_g, na_w_qkv, na_rpb, na_w_o, da_w_qkv, da_lambda, da_subln_g, da_w_o, hg_w_in, hg_lb, hg_norm_g, hg_w_o, moe_w_router, moe_b_router, moe_w_gate, moe_w_up, moe_w_down, final_g):
    n_batch, seq, d = x.shape
    tb = CTX_LEN + seq
    t = n_batch * tb
    tpb = tb // ROW_TILE
    assert d == D_MODEL and ctx.shape[1] == CTX_LEN and n_batch + 1 <= 8
    assert seq % DA_TQ == 0 and t % MM_TM == 0 and seq // GRID_W >= NA_WIN_ROWS

    xs = jnp.concatenate([ctx, x], axis=1).reshape(t, d).astype(F32)

    c_rows = jnp.zeros((8, d), F32).at[:n_batch].set(c.astype(F32)).at[n_batch].set(c_ctx.astype(F32))
    mods = _ada_mods(c_rows, ada_w.astype(F32), ada_b.astype(F32))
    mods3 = mods.reshape(DEPTH * 8, 1, 6 * d)
    g3 = norm_g.astype(F32).reshape(DEPTH * 2, 1, d)

    lb_p = jax.nn.softmax(hg_lb.astype(F32), axis=1)
    lb_all = jnp.cumsum(lb_p, axis=1) - lb_p[:, :1]

    wr = jnp.zeros((d, LANES), F32).at[:, :N_EXPERTS].set(moe_w_router.astype(F32))
    wr_hi = wr.astype(BF16)
    wr_lo = (wr - wr_hi.astype(F32)).astype(BF16)
    rb = moe_b_router.astype(F32).reshape(N_EXPERTS, 1)

    n_blocks = -(-(t * TOP_K) // MOE_TM) + N_EXPERTS + (MOE_XBUFS - 1)
    rope = _rope_tables(n_batch, seq)
    wg = moe_w_gate.astype(F32)
    wu = moe_w_up.astype(F32)
    wd = moe_w_down.astype(F32)
    na_qs = NA_HEAD_DIM ** -0.5 * LOG2E
    da_qs = DIFF_HEAD_DIM ** -0.5 * LOG2E

    counters = [0, 0, 0]
    out = None
    for i in range(DEPTH):
        kind = i % N_MIXERS
        j = counters[kind]
        counters[kind] += 1

        if i == 0:
            h = _norm_mod(xs, mods3, g3, i, 0, n_batch, tpb)[0]
        if kind == 0:
            qkv = _matmul(h, na_w_qkv[j], q_cols=d, q_scale=na_qs)
            o = _na_attention(qkv, na_rpb[j], n_batch, tb)
            w_o = na_w_o[j]
        elif kind == 1:
            qkv = _matmul(h, da_w_qkv[j], q_cols=d, q_scale=da_qs, rope=rope, n_rope_cols=2 * d)
            o = _diff_attention(qkv, da_lambda[j], da_subln_g[j], i, n_batch, tb)
            w_o = da_w_o[j]
        else:
            proj = _matmul(h, hg_w_in[j])
            o32 = _hgrn_scan(proj, lb_all[:, i, :], n_batch, tb)
            o = _hgrn_finish(o32, proj, hg_norm_g[j])
            w_o = hg_w_o[j]
        xs = _matmul_residual(o, w_o, xs, mods3, i, 2, n_batch, tpb)

        hp, route, counts = _norm_mod(xs, mods3, g3, i, 1, n_batch, tpb, router=(wr_hi, wr_lo, rb))
        w, dest, slot_tok, blk_info = _route(route, counts, n_blocks)
        yb = _moe_experts(hp, slot_tok, *blk_info, wg, wu, wd, i)
        if i < DEPTH - 1:
            xs, h = _moe_residual(xs, yb, dest, w, mods3, g3, i, n_batch, tpb)
        else:
            y0 = yb.at[dest[:, 0]].get(mode="promise_in_bounds")
            y1 = yb.at[dest[:, 1]].get(mode="promise_in_bounds")
            out = _moe_residual_final(xs, y0, y1, w, mods3, i, final_g, n_batch, tpb)
    return out.reshape(n_batch, seq, d).astype(x.dtype)
```

```python
import functools
import math

import numpy as np
import jax
import jax.numpy as jnp
from jax import lax
from jax.experimental import pallas as pl
from jax.experimental.pallas import tpu as pltpu

F32 = jnp.float32
BF16 = jnp.bfloat16

D_MODEL = 2048
DEPTH = 4
GRID_W = 64
CTX_LEN = 256
N_MIXERS = 3
NORM_EPS = 1e-6
ROPE_THETA = 10000.0

NA_HEADS = 16
NA_HEAD_DIM = 128
NA_WIN_ROWS = 8
NA_WIN_COLS = 16

DIFF_HEADS = 8
DIFF_HEAD_DIM = 128

HGRN_HEADS = 16
HGRN_DK = 128
HGRN_CHUNK = 64

N_EXPERTS = 16
N_GROUPS = 4
EXPERTS_PER_GROUP = 4
TOP_K = 2
D_EXPERT = 1024

LANES = 128
ROW_TILE = 256
MM_TM = 512
MM_TN = 1024
MOE_TM = 256
DA_TQ = 512
VMEM_LIMIT = 56 * 1024 * 1024

NA_GROUP = 4
NA_KROWS = 12
DA_SUB = 64
DA_TK = 512
HG_GROUP = 4
HG_HEADS = 1
HG_SAFE_FORGET = math.exp(-2.0)
MOE_XBUFS = 3
MOE_CAST_SLABS = 8

NEG_BIG = -1e30
LOG2E = 1.4426950408889634


def _cparams(sem):
    return pltpu.CompilerParams(dimension_semantics=sem, vmem_limit_bytes=VMEM_LIMIT)


def _dot(a, b):
    return jnp.dot(a, b, preferred_element_type=F32)


def _dot_nt(a, b):
    return lax.dot_general(a, b, (((1,), (1,)), ((), ())), preferred_element_type=F32)


def _dot_tn(a, b):
    return lax.dot_general(a, b, (((0,), (0,)), ((), ())), preferred_element_type=F32)


def _row_sel(sub, tiles_per_batch, n_batch):
    return jnp.where(sub % tiles_per_batch == 0, n_batch, sub // tiles_per_batch)


def _ada_kernel(c_ref, w_ref, b_ref, o_ref):
    c = c_ref[...]
    a = c * jax.nn.sigmoid(c)
    o_ref[...] = jnp.dot(a, w_ref[...], preferred_element_type=F32,
                         precision=lax.Precision.HIGHEST) + b_ref[...]


def _ada_mods(c_rows, ada_w, ada_b):
    depth, d, n = ada_w.shape
    tn = 1024
    return pl.pallas_call(
        _ada_kernel,
        grid=(depth, n // tn),
        in_specs=[
            pl.BlockSpec((8, d), lambda l, j: (0, 0)),
            pl.BlockSpec((None, d, tn), lambda l, j: (l, 0, j)),
            pl.BlockSpec((None, 1, tn), lambda l, j: (l, 0, j)),
        ],
        out_specs=pl.BlockSpec((None, 8, tn), lambda l, j: (l, 0, j)),
        out_shape=jax.ShapeDtypeStruct((depth, 8, n), F32),
        compiler_params=_cparams(("arbitrary", "arbitrary")),
        name="ada_mods",
    )(c_rows, ada_w, ada_b.reshape(depth, 1, n))


def _norm_mod_kernel(x_ref, g_ref, shift_ref, scale_ref, h_ref):
    x = x_ref[...]
    y = x * lax.rsqrt(jnp.mean(x * x, axis=-1, keepdims=True) + NORM_EPS)
    h = y * g_ref[...] * (1.0 + scale_ref[...]) + shift_ref[...]
    h_ref[...] = h.astype(h_ref.dtype)


def _route_tile(lg, bias, run_ref):
    rows = lg.shape[0]
    z = lg.T[0:N_EXPERTS, :]
    eid = lax.broadcasted_iota(jnp.int32, (N_EXPERTS, 1), 0).astype(F32)
    neg = -jnp.inf
    far = float(LANES)
    cmax = lambda a: jnp.max(a, axis=0, keepdims=True)
    cmin = lambda a: jnp.min(a, axis=0, keepdims=True)
    csum = lambda a: jnp.sum(a, axis=0, keepdims=True)

    s = jax.nn.sigmoid(z)
    sel = s + bias
    best = None
    for g in range(N_GROUPS):
        ing = (eid >= g * EXPERTS_PER_GROUP) & (eid < (g + 1) * EXPERTS_PER_GROUP)
        v = jnp.where(ing, sel, neg)
        m1 = cmax(v)
        i1 = cmin(jnp.where(v == m1, eid, far))
        v2 = jnp.where(eid == i1, neg, v)
        m2 = cmax(v2)
        i2 = cmin(jnp.where(v2 == m2, eid, far))
        cand = (m1 + m2, i1, i2)
        if best is None:
            best = cand
        else:
            better = cand[0] > best[0]
            best = tuple(jnp.where(better, c, b) for c, b in zip(cand, best))
    _, e1, e2 = best
    is1 = eid == e1
    is2 = eid == e2
    s1 = csum(jnp.where(is1, s, 0.0))
    s2 = csum(jnp.where(is2, s, 0.0))
    w1 = s1 / (s1 + s2)
    w2 = s2 / (s1 + s2)
    onehot = jnp.where(is1 | is2, 1.0, 0.0)
    row = lax.broadcasted_iota(jnp.int32, (rows, rows), 0)
    col = lax.broadcasted_iota(jnp.int32, (rows, rows), 1)
    before = jnp.where(row < col, 1.0, 0.0).astype(BF16)
    prior = _dot(onehot.astype(BF16), before) + run_ref[...]
    r1 = csum(jnp.where(is1, prior, 0.0))
    r2 = csum(jnp.where(is2, prior, 0.0))
    run_ref[...] += jnp.sum(onehot, axis=1, keepdims=True)
    zero = jnp.zeros_like(e1)
    return jnp.concatenate([e1, e2, r1, r2, w1, w2, zero, zero], axis=0)


def _norm_mod_router_kernel(x_ref, g_ref, shift_ref, scale_ref, whi_ref, wlo_ref, rb_ref,
                            h_ref, route_ref, cnt_ref, run_ref):
    @pl.when(pl.program_id(0) == 0)
    def _():
        run_ref[...] = jnp.zeros(run_ref.shape, F32)

    x = x_ref[...]
    y = x * lax.rsqrt(jnp.mean(x * x, axis=-1, keepdims=True) + NORM_EPS)
    h = y * g_ref[...] * (1.0 + scale_ref[...]) + shift_ref[...]
    h_hi = h.astype(BF16)
    h_lo = (h - h_hi.astype(F32)).astype(BF16)
    bits = lax.bitcast_convert_type(h_hi.astype(F32), jnp.uint32)
    half = bits.shape[1] // 2
    h_ref[...] = (bits[:, :half] >> 16) | bits[:, half:]
    lg = _dot(h_hi, whi_ref[...]) + _dot(h_hi, wlo_ref[...]) + _dot(h_lo, whi_ref[...])
    route_ref[...] = _route_tile(lg, rb_ref[...], run_ref)
    cnt_ref[...] = run_ref[...]


def _norm_mod(xs, mods3, g3, layer, which, n_batch, tpb, router=None):
    t, d = xs.shape
    shift_k, scale_k = (0, 1) if which == 0 else (3, 4)
    nb = n_batch

    def mod_spec(k):
        return pl.BlockSpec((None, 1, d), lambda i: (layer * 8 + _row_sel(i, tpb, nb), 0, k))

    in_specs = [
        pl.BlockSpec((ROW_TILE, d), lambda i: (i, 0)),
        pl.BlockSpec((None, 1, d), lambda i: (layer * 2 + which, 0, 0)),
        mod_spec(shift_k),
        mod_spec(scale_k),
    ]
    args = [xs, g3, mods3, mods3]
    out_specs = [pl.BlockSpec((ROW_TILE, d), lambda i: (i, 0))]
    out_shape = [jax.ShapeDtypeStruct((t, d), BF16)]
    kern = _norm_mod_kernel
    scratch = []
    if router is not None:
        in_specs += [pl.BlockSpec((d, LANES), lambda i: (0, 0))] * 2 + [pl.BlockSpec((N_EXPERTS, 1), lambda i: (0, 0))]
        args += list(router)
        out_specs = [pl.BlockSpec((ROW_TILE, d // 2), lambda i: (i, 0)),
                     pl.BlockSpec((8, ROW_TILE), lambda i: (0, i)),
                     pl.BlockSpec((N_EXPERTS, 1), lambda i: (0, 0))]
        out_shape = [jax.ShapeDtypeStruct((t, d // 2), jnp.uint32),
                     jax.ShapeDtypeStruct((8, t), F32),
                     jax.ShapeDtypeStruct((N_EXPERTS, 1), F32)]
        scratch = [pltpu.VMEM((N_EXPERTS, 1), F32)]
        kern = _norm_mod_router_kernel
    return pl.pallas_call(
        kern,
        grid=(t // ROW_TILE,),
        in_specs=in_specs,
        out_specs=out_specs,
        out_shape=out_shape,
        scratch_shapes=scratch,
        compiler_params=_cparams(("arbitrary",)),
        name="norm_mod",
    )(*args)


def _rope_store(acc, cos_ref, sin_ref, o_ref):
    lane = lax.broadcasted_iota(jnp.int32, (1, LANES), 1)
    first = (lane % 64) < 32
    cos = cos_ref[...]
    sin = sin_ref[...]
    for c in range(acc.shape[1] // LANES):
        xc = acc[:, c * LANES:(c + 1) * LANES]
        sw = jnp.where(first, pltpu.roll(xc, 96, 1), pltpu.roll(xc, 32, 1))
        o_ref[:, c * LANES:(c + 1) * LANES] = (xc * cos + sw * sin).astype(o_ref.dtype)


def _mm_kernel(x_ref, w_ref, o_ref, wb_ref, *, n_q, q_scale):
    @pl.when(pl.program_id(1) == 0)
    def _():
        wb_ref[...] = w_ref[...].astype(BF16)

    acc = _dot(x_ref[...], wb_ref[...])
    j = pl.program_id(0)

    @pl.when(j < n_q)
    def _():
        o_ref[...] = (acc * q_scale).astype(o_ref.dtype)

    @pl.when(j >= n_q)
    def _():
        o_ref[...] = acc.astype(o_ref.dtype)


def _mm_rope_kernel(x_ref, w_ref, cos_ref, sin_ref, o_ref, wb_ref, *, n_q, q_scale, n_rope):
    @pl.when(pl.program_id(1) == 0)
    def _():
        wb_ref[...] = w_ref[...].astype(BF16)

    acc = _dot(x_ref[...], wb_ref[...])
    j = pl.program_id(0)

    @pl.when(j < n_q)
    def _():
        _rope_store(acc * q_scale, cos_ref, sin_ref, o_ref)

    @pl.when((j >= n_q) & (j < n_rope))
    def _():
        _rope_store(acc, cos_ref, sin_ref, o_ref)

    @pl.when(j >= n_rope)
    def _():
        o_ref[...] = acc.astype(o_ref.dtype)


def _mm_res_kernel(x_ref, w_ref, xs_ref, ga_ref, gb_ref, o_ref, wb_ref):
    @pl.when(pl.program_id(1) == 0)
    def _():
        wb_ref[...] = w_ref[...].astype(BF16)

    acc = _dot(x_ref[...], wb_ref[...])
    half = ROW_TILE
    o_ref[:half, :] = xs_ref[:half, :] + ga_ref[...] * acc[:half, :]
    o_ref[half:, :] = xs_ref[half:, :] + gb_ref[...] * acc[half:, :]


def _matmul(x, w, q_cols=0, q_scale=1.0, rope=None, n_rope_cols=0):
    t, k = x.shape
    n = w.shape[1]
    in_specs = [
        pl.BlockSpec((MM_TM, k), lambda j, i: (i, 0)),
        pl.BlockSpec((k, MM_TN), lambda j, i: (0, j)),
    ]
    args = [x, w]
    kern = functools.partial(_mm_kernel, n_q=q_cols // MM_TN, q_scale=q_scale)
    if rope is not None:
        in_specs += [pl.BlockSpec((MM_TM, LANES), lambda j, i: (i, 0))] * 2
        args += list(rope)
        kern = functools.partial(_mm_rope_kernel, n_q=q_cols // MM_TN, q_scale=q_scale,
                                 n_rope=n_rope_cols // MM_TN)
    return pl.pallas_call(
        kern,
        grid=(n // MM_TN, t // MM_TM),
        in_specs=in_specs,
        out_specs=pl.BlockSpec((MM_TM, MM_TN), lambda j, i: (i, j)),
        out_shape=jax.ShapeDtypeStruct((t, n), BF16),
        scratch_shapes=[pltpu.VMEM((k, MM_TN), BF16)],
        compiler_params=_cparams(("arbitrary", "arbitrary")),
        name="matmul",
    )(*args)


def _matmul_residual(x, w, xs, mods3, layer, gate_k, n_batch, tpb):
    t, k = x.shape
    n = w.shape[1]
    ncol = n // MM_TN
    nb = n_batch

    def gate_spec(half):
        return pl.BlockSpec(
            (None, 1, MM_TN),
            lambda j, i: (layer * 8 + _row_sel(2 * i + half, tpb, nb), 0, gate_k * ncol + j))

    return pl.pallas_call(
        _mm_res_kernel,
        grid=(ncol, t // MM_TM),
        in_specs=[
            pl.BlockSpec((MM_TM, k), lambda j, i: (i, 0)),
            pl.BlockSpec((k, MM_TN), lambda j, i: (0, j)),
            pl.BlockSpec((MM_TM, MM_TN), lambda j, i: (i, j)),
            gate_spec(0),
            gate_spec(1),
        ],
        out_specs=pl.BlockSpec((MM_TM, MM_TN), lambda j, i: (i, j)),
        out_shape=jax.ShapeDtypeStruct((t, n), F32),
        scratch_shapes=[pltpu.VMEM((k, MM_TN), BF16)],
        input_output_aliases={2: 0},
        compiler_params=_cparams(("arbitrary", "arbitrary")),
        name="matmul_residual",
    )(x, w, xs, mods3, mods3)


def _na_kernel(q_ref, k_ref, v_ref, tab_ref, o_ref, sl_ref, sc_ref, pl_ref, pc_ref, li_ref, *, rows):
    nq = NA_GROUP * GRID_W
    nk = NA_KROWS * GRID_W
    half_win = NA_WIN_ROWS // 2
    left = lax.broadcasted_iota(jnp.int32, (1, LANES), 1) < GRID_W

    s = _dot_nt(q_ref[0:CTX_LEN, :], k_ref[0:CTX_LEN, :])
    p = jnp.exp2(s - jnp.max(s, axis=-1, keepdims=True))
    l = jnp.sum(p, axis=-1, keepdims=True)
    o_ref[0:CTX_LEN, :] = (_dot(p.astype(BF16), v_ref[0:CTX_LEN, :]) / l).astype(o_ref.dtype)

    def geometry(g):
        r0 = jnp.asarray(g, jnp.int32) * NA_GROUP
        start = jnp.clip(r0 - half_win, 0, rows - NA_KROWS)
        q0 = pl.multiple_of(CTX_LEN + r0 * GRID_W, nq)
        k0 = pl.multiple_of(CTX_LEN + start * GRID_W, GRID_W)
        return r0, start, q0, k0

    def stage_a(g, par):
        _, _, q0, k0 = geometry(g)
        q = q_ref[pl.ds(q0, nq), :]
        sl_ref[par] = _dot_nt(q, k_ref[pl.ds(k0, nk), :])
        sc_ref[par] = _dot_nt(q, k_ref[0:CTX_LEN, :])

    def stage_b(g, par):
        r0, start, _, _ = geometry(g)
        for i in range(NA_GROUP):
            r = r0 + i
            rs = jnp.clip(r - half_win, 0, rows - NA_WIN_ROWS)
            rsl = slice(i * GRID_W, (i + 1) * GRID_W)
            tiles = []
            for jp in range(NA_KROWS // 2):
                kr = start + 2 * jp
                ok0 = ((kr >= rs) & (kr < rs + NA_WIN_ROWS)).astype(jnp.int32)
                ok1 = ((kr + 1 >= rs) & (kr + 1 < rs + NA_WIN_ROWS)).astype(jnp.int32)
                e = jnp.clip(kr - r + (NA_WIN_ROWS - 1) + 2, 0, 2 * NA_WIN_ROWS + 1)
                ok = jnp.where(left, ok0, ok1) > 0
                tiles.append(jnp.where(ok, tab_ref[e], NEG_BIG))
            sl = sl_ref[par, rsl, :] + jnp.concatenate(tiles, axis=1)
            sc = sc_ref[par, rsl, :]
            m = jnp.maximum(jnp.max(sl, axis=-1, keepdims=True), jnp.max(sc, axis=-1, keepdims=True))
            pl_i = jnp.exp2(sl - m)
            pc_i = jnp.exp2(sc - m)
            l_i = jnp.sum(pl_i, axis=-1, keepdims=True) + jnp.sum(pc_i, axis=-1, keepdims=True)
            pl_ref[par, rsl, :] = pl_i.astype(BF16)
            pc_ref[par, rsl, :] = pc_i.astype(BF16)
            li_ref[par, rsl, :] = jnp.broadcast_to(1.0 / l_i, (GRID_W, LANES))

    def stage_c(g, par):
        _, _, q0, k0 = geometry(g)
        o = _dot(pl_ref[par], v_ref[pl.ds(k0, nk), :]) + _dot(pc_ref[par], v_ref[0:CTX_LEN, :])
        o_ref[pl.ds(q0, nq), :] = (o * li_ref[par]).astype(o_ref.dtype)

    n_groups = rows // NA_GROUP
    stage_a(0, 0)
    stage_a(1, 1)
    stage_b(0, 0)

    def pair(u, carry):
        t = 2 + 2 * u
        stage_a(t, 0)
        stage_b(t - 1, 1)
        stage_c(t - 2, 0)
        stage_a(t + 1, 1)
        stage_b(t, 0)
        stage_c(t - 1, 1)
        return carry

    lax.fori_loop(0, (n_groups - 2) // 2, pair, 0)
    stage_b(n_groups - 1, 1)
    stage_c(n_groups - 2, 0)
    stage_c(n_groups - 1, 1)


def _na_bias_table(rpb):
    h = rpb.shape[0]
    qc = np.arange(GRID_W)[:, None]
    kc = np.arange(GRID_W)[None, :]
    cs = np.clip(qc - NA_WIN_COLS // 2, 0, GRID_W - NA_WIN_COLS)
    col_ok = (kc >= cs) & (kc < cs + NA_WIN_COLS)
    dc = np.clip(kc - qc + (NA_WIN_COLS - 1), 0, 2 * NA_WIN_COLS - 2)
    a = jnp.take(rpb.astype(F32), jnp.asarray(dc.reshape(-1)), axis=2)
    a = a.reshape(h, 2 * NA_WIN_ROWS - 1, GRID_W, GRID_W) * LOG2E
    a = jnp.where(jnp.asarray(col_ok)[None, None], a, NEG_BIG)
    a = jnp.pad(a, ((0, 0), (2, 2), (0, 0), (0, 0)), constant_values=NEG_BIG)
    return jnp.concatenate([a[:, :-1], a[:, 1:]], axis=-1)


def _na_attention(qkv, rpb, n_batch, tb):
    t = qkv.shape[0]
    rows = (tb - CTX_LEN) // GRID_W
    assert rows % (2 * NA_GROUP) == 0 and rows >= NA_KROWS
    nq, nk = NA_GROUP * GRID_W, NA_KROWS * GRID_W
    table = _na_bias_table(rpb)
    hh = NA_HEADS
    return pl.pallas_call(
        functools.partial(_na_kernel, rows=rows),
        grid=(hh, n_batch),
        in_specs=[
            pl.BlockSpec((tb, LANES), lambda h, b: (b, h)),
            pl.BlockSpec((tb, LANES), lambda h, b: (b, hh + h)),
            pl.BlockSpec((tb, LANES), lambda h, b: (b, 2 * hh + h)),
            pl.BlockSpec((None, 2 * NA_WIN_ROWS + 2, GRID_W, LANES), lambda h, b: (h, 0, 0, 0)),
        ],
        out_specs=pl.BlockSpec((tb, LANES), lambda h, b: (b, h)),
        out_shape=jax.ShapeDtypeStruct((t, D_MODEL), BF16),
        scratch_shapes=[
            pltpu.VMEM((2, nq, nk), F32),
            pltpu.VMEM((2, nq, CTX_LEN), F32),
            pltpu.VMEM((2, nq, nk), BF16),
            pltpu.VMEM((2, nq, CTX_LEN), BF16),
            pltpu.VMEM((2, nq, LANES), F32),
        ],
        compiler_params=_cparams(("arbitrary", "arbitrary")),
        name="na_attention",
    )(qkv, qkv, qkv, table)


def _da_kernel(lam_ref, g_ref, q_ref, k_ref, v_ref, o_ref, acc_ref, m_ref, l_ref,
               s0_ref, s1_ref, p0_ref, p1_ref, a0_ref, a1_ref, qbd_ref, *, lam_init, n_kv_chunks):
    d = DIFF_HEAD_DIM
    i = pl.program_id(2)
    lam = lam_ref[...]
    lam_full = (jnp.exp(jnp.sum(lam[0:1] * lam[1:2], axis=-1, keepdims=True))
                - jnp.exp(jnp.sum(lam[2:3] * lam[3:4], axis=-1, keepdims=True)) + lam_init)
    s_bufs, p_bufs, a_bufs = (s0_ref, s1_ref), (p0_ref, p1_ref), (a0_ref, a1_ref)

    def tile(q0, nq, n_lat_chunks):
        acc_ref[:, 0:nq, :] = jnp.zeros((2, nq, 2 * d), F32)
        m_ref[:, 0:nq, :] = jnp.full((2, nq, LANES), NEG_BIG, F32)
        l_ref[:, 0:nq, :] = jnp.zeros((2, nq, LANES), F32)

        def key_rows(e):
            if isinstance(e, int):
                return (0, CTX_LEN) if e == 0 else (CTX_LEN + (e - 1) * DA_TK, DA_TK)
            return pl.multiple_of(CTX_LEN + (e - 1) * DA_TK, ROW_TILE), DA_TK

        q = q_ref[pl.ds(q0, nq), :]
        zero = jnp.zeros((nq, d), q.dtype)
        qbd_ref[0:nq, :] = jnp.concatenate([q[:, :d], zero], axis=1)
        qbd_ref[nq:2 * nq, :] = jnp.concatenate([zero, q[:, d:]], axis=1)

        def stage_a(e, par):
            k0, nk = key_rows(e)
            s = _dot_nt(qbd_ref[0:2 * nq, :], k_ref[pl.ds(k0, nk), :])
            for c in range(2):
                s_bufs[par][c, 0:nq, 0:nk] = s[c * nq:(c + 1) * nq, :]

        def stage_b(e, par):
            _, nk = key_rows(e)
            s_ref, p_ref, a_ref = s_bufs[par], p_bufs[par], a_bufs[par]
            for c in range(2):
                for sb in range(nq // DA_SUB):
                    rows = slice(sb * DA_SUB, (sb + 1) * DA_SUB)
                    blocks = [s_ref[c, rows, j * LANES:(j + 1) * LANES] for j in range(nk // LANES)]
                    m_prev = m_ref[c, rows, :]
                    m_new = jnp.maximum(
                        m_prev, jnp.max(functools.reduce(jnp.maximum, blocks), axis=-1, keepdims=True))
                    alpha = jnp.exp2(m_prev - m_new)
                    p = [jnp.exp2(blk - m_new) for blk in blocks]
                    l_ref[c, rows, :] = alpha * l_ref[c, rows, :] + functools.reduce(jnp.add, p)
                    for j, pj in enumerate(p):
                        p_ref[c, rows, j * LANES:(j + 1) * LANES] = pj.astype(BF16)
                    a_ref[c, rows, :] = alpha
                    m_ref[c, rows, :] = m_new

        def stage_c(e, par):
            k0, nk = key_rows(e)
            v = v_ref[pl.ds(k0, nk), :]
            p = jnp.concatenate([p_bufs[par][0, 0:nq, 0:nk], p_bufs[par][1, 0:nq, 0:nk]], axis=0)
            pv_all = _dot(p, v)
            for c in range(2):
                pv = pv_all[c * nq:(c + 1) * nq, :]
                alpha = a_bufs[par][c, 0:nq, :]
                acc_ref[c, 0:nq, 0:LANES] = alpha * acc_ref[c, 0:nq, 0:LANES] + pv[:, :LANES]
                acc_ref[c, 0:nq, LANES:] = alpha * acc_ref[c, 0:nq, LANES:] + pv[:, LANES:]

        n = n_lat_chunks
        if n == 0:
            stage_a(0, 0)
            stage_b(0, 0)
            stage_c(0, 0)
        else:
            assert n % 2 == 0
            stage_a(0, 0)
            stage_a(1, 1)
            stage_b(0, 0)
            stage_a(2, 0)
            stage_b(1, 1)
            stage_c(0, 0)

            def pair(u, carry):
                t = 3 + 2 * u
                stage_a(t, 1)
                stage_b(t - 1, 0)
                stage_c(t - 2, 1)
                stage_a(t + 1, 0)
                stage_b(t, 1)
                stage_c(t - 1, 0)
                return carry

            lax.fori_loop(0, (n - 2) // 2, pair, 0)
            stage_b(n, 0)
            stage_c(n - 1, 1)
            stage_c(n, 0)

        r0 = 1.0 / jnp.sum(l_ref[0, 0:nq, :], axis=-1, keepdims=True)
        r1 = lam_full / jnp.sum(l_ref[1, 0:nq, :], axis=-1, keepdims=True)
        o_a = acc_ref[0, 0:nq, 0:LANES] * r0 - acc_ref[1, 0:nq, 0:LANES] * r1
        o_b = acc_ref[0, 0:nq, LANES:] * r0 - acc_ref[1, 0:nq, LANES:] * r1
        ms = (jnp.sum(o_a * o_a, axis=-1, keepdims=True) + jnp.sum(o_b * o_b, axis=-1, keepdims=True)) / (2 * d)
        inv = lax.rsqrt(ms + NORM_EPS) * (1.0 - lam_init)
        g = g_ref[...]
        o_ref[pl.ds(q0, nq), 0:LANES] = (o_a * inv * g[:, :LANES]).astype(o_ref.dtype)
        o_ref[pl.ds(q0, nq), LANES:] = (o_b * inv * g[:, LANES:]).astype(o_ref.dtype)

    @pl.when(i == 0)
    def _():
        tile(0, CTX_LEN, 0)

    @pl.when(i > 0)
    def _():
        tile(pl.multiple_of(CTX_LEN + (i - 1) * DA_TQ, ROW_TILE), DA_TQ, n_kv_chunks)


def _diff_attention(qkv, lam, subln_g, layer_idx, n_batch, tb):
    t = qkv.shape[0]
    hh = DIFF_HEADS
    w = 2 * DIFF_HEAD_DIM
    n_lat = (tb - CTX_LEN) // DA_TQ
    lam_init = 0.8 - 0.6 * math.exp(-0.3 * layer_idx)
    return pl.pallas_call(
        functools.partial(_da_kernel, lam_init=lam_init, n_kv_chunks=(tb - CTX_LEN) // DA_TK),
        grid=(n_batch, hh, 1 + n_lat),
        in_specs=[
            pl.BlockSpec((4, DIFF_HEAD_DIM), lambda b, h, i: (0, 0)),
            pl.BlockSpec((1, w), lambda b, h, i: (0, 0)),
            pl.BlockSpec((tb, w), lambda b, h, i: (b, h)),
            pl.BlockSpec((tb, w), lambda b, h, i: (b, hh + h)),
            pl.BlockSpec((tb, w), lambda b, h, i: (b, 2 * hh + h)),
        ],
        out_specs=pl.BlockSpec((tb, w), lambda b, h, i: (b, h)),
        out_shape=jax.ShapeDtypeStruct((t, D_MODEL), BF16),
        scratch_shapes=[
            pltpu.VMEM((2, DA_TQ, w), F32),
            pltpu.VMEM((2, DA_TQ, LANES), F32),
            pltpu.VMEM((2, DA_TQ, LANES), F32),
            pltpu.VMEM((2, DA_TQ, DA_TK), F32),
            pltpu.VMEM((2, DA_TQ, DA_TK), F32),
            pltpu.VMEM((2, DA_TQ, DA_TK), BF16),
            pltpu.VMEM((2, DA_TQ, DA_TK), BF16),
            pltpu.VMEM((2, DA_TQ, LANES), F32),
            pltpu.VMEM((2, DA_TQ, LANES), F32),
            pltpu.VMEM((2 * DA_TQ, w), BF16),
        ],
        compiler_params=_cparams(("arbitrary", "arbitrary", "arbitrary")),
        name="diff_attention",
    )(lam.astype(F32), subln_g.astype(F32).reshape(1, w), qkv, qkv, qkv)


def _hg_kernel(q_ref, i_ref, ff_ref, fb_ref, lb_ref, o_ref, acc_ref, stf_ref, stb_ref, xb_ref, xk_ref, xv_ref, *,
               n_groups):
    c = HGRN_CHUNK
    rg = HG_GROUP * c
    dk = HGRN_DK
    acc_ref[...] = jnp.zeros(acc_ref.shape, F32)
    stf_ref[...] = jnp.zeros(stf_ref.shape, F32)
    stb_ref[...] = jnp.zeros(stb_ref.shape, F32)
    row = lax.broadcasted_iota(jnp.int32, (rg, rg), 0)
    col = lax.broadcasted_iota(jnp.int32, (rg, rg), 1)
    same = (row // c) == (col // c)
    lower = same & (row >= col)
    upper = same & (col >= row)
    lower_b = lower.astype(BF16)
    upper_b = upper.astype(BF16)
    chunk_of_row = lax.broadcasted_iota(jnp.int32, (rg, 1), 0) // c
    lb_all = lb_ref[...]

    def per_chunk_rows(b, r):
        return jnp.concatenate(
            [jnp.broadcast_to(b[g * c + r:g * c + r + 1, :], (c, dk)) for g in range(HG_GROUP)], axis=0)

    def intra_exact(q, k, v, bcum, forward):
        xb_ref[...] = bcum
        xk_ref[...] = k
        xv_ref[...] = v.astype(F32)
        rowi = lax.broadcasted_iota(jnp.int32, (c, 1), 0)
        outs = []
        for g in range(HG_GROUP):
            bc_g = bcum[g * c:(g + 1) * c, :]
            q_g = q[g * c:(g + 1) * c, :]

            def column(s, acc, g=g, bc_g=bc_g, q_g=q_g):
                live = (rowi >= s) if forward else (rowi <= s)
                bs = xb_ref[pl.ds(g * c + s, 1), :]
                w = jnp.where(live, jnp.exp(jnp.where(live, bc_g - bs, 0.0)), 0.0) * q_g
                a = jnp.sum(w * xk_ref[pl.ds(g * c + s, 1), :], axis=-1, keepdims=True)
                return acc + a * xv_ref[pl.ds(g * c + s, 1), :]

            outs.append(lax.fori_loop(0, c, column, jnp.zeros((c, dk), F32)))
        return jnp.concatenate(outs, axis=0)

    def group(hd, gidx, f_ref, lb_row, mask, mask_b, st_ref, end_row, order, exact):
        r0 = pl.multiple_of(gidx * rg, rg)
        cols = slice(hd * dk, (hd + 1) * dk)
        lb = lb_all[lb_row:lb_row + 1, cols]
        qr = q_ref[pl.ds(r0, rg), cols].astype(F32)
        q = qr * jax.nn.sigmoid(qr) * (dk ** -0.5)
        v = i_ref[pl.ds(r0, rg), cols]
        fr = f_ref[pl.ds(r0, rg), cols].astype(F32)
        fg = lb + (1.0 - lb) * jax.nn.sigmoid(fr)
        k = 1.0 - fg
        lf = jnp.log(fg)
        hi = lf.astype(BF16)
        rem = lf - hi.astype(F32)
        mid = rem.astype(BF16)
        lo = (rem - mid.astype(F32)).astype(BF16)
        cs = _dot(mask_b, jnp.concatenate([hi, mid, lo], axis=1))
        bcum = cs[:, 0:dk] + cs[:, dk:2 * dk] + cs[:, 2 * dk:]
        b_end = per_chunk_rows(bcum, end_row)
        if exact:
            o_intra = intra_exact(q, k, v, bcum, end_row != 0)
        else:
            b_mid = per_chunk_rows(bcum, c // 2)
            qt = (q * jnp.exp(bcum - b_mid)).astype(BF16)
            kt = (k * jnp.exp(b_mid - bcum)).astype(BF16)
            a = jnp.where(mask, _dot_nt(qt, kt), 0.0).astype(BF16)
            o_intra = _dot(a, v)
        k_end = k * jnp.exp(b_end - bcum)
        k_exp = jnp.concatenate(
            [jnp.where(chunk_of_row == g, k_end, 0.0) for g in range(HG_GROUP)], axis=1).astype(BF16)
        u = _dot_tn(v, k_exp)
        st = st_ref[hd]
        st_in = [None] * HG_GROUP
        for g in order:
            st_in[g] = st.astype(BF16)
            st = st * jnp.exp(bcum[g * c + end_row:g * c + end_row + 1, :]) + u[:, g * dk:(g + 1) * dk]
        st_ref[hd] = st
        oi = _dot_nt((q * jnp.exp(bcum)).astype(BF16), jnp.concatenate(st_in, axis=0))
        o_inter = jnp.concatenate(
            [oi[g * c:(g + 1) * c, g * dk:(g + 1) * dk] for g in range(HG_GROUP)], axis=0)
        acc_ref[pl.ds(r0, rg), cols] += o_intra + o_inter

    fwd_order = list(range(HG_GROUP))

    def scan(exact):
        def body(n, carry):
            nb = jnp.where(n == 0, 0, n_groups - n)
            for hd in range(HG_HEADS):
                group(hd, n, ff_ref, 0, lower, lower_b, stf_ref, c - 1, fwd_order, exact)
                group(hd, nb, fb_ref, 1, upper, upper_b, stb_ref, 0, fwd_order[::-1], exact)
            return carry

        lax.fori_loop(0, n_groups, body, 0)

    def min_forget(f_ref, lb_row):
        lb = lb_all[lb_row:lb_row + 1, :]
        f_min = jnp.min(f_ref[...].astype(F32), axis=0, keepdims=True)
        return jnp.min(lb + (1.0 - lb) * jax.nn.sigmoid(f_min))

    safe = jnp.minimum(min_forget(ff_ref, 0), min_forget(fb_ref, 1)) >= HG_SAFE_FORGET

    @pl.when(safe)
    def _():
        scan(False)

    @pl.when(jnp.logical_not(safe))
    def _():
        scan(True)

    o_ref[...] = acc_ref[...].astype(o_ref.dtype)


def _hgrn_scan(proj, lb2, n_batch, tb):
    t = proj.shape[0]
    hh = HGRN_HEADS // HG_HEADS
    wd = HG_HEADS * LANES
    rg = HG_GROUP * HGRN_CHUNK
    assert CTX_LEN == rg and tb % rg == 0

    def col(k):
        return pl.BlockSpec((tb, wd), lambda b, h: (b, k * hh + h))

    return pl.pallas_call(
        functools.partial(_hg_kernel, n_groups=tb // rg),
        grid=(n_batch, hh),
        in_specs=[col(0), col(1), col(3), col(4), pl.BlockSpec((2, wd), lambda b, h: (0, h))],
        out_specs=pl.BlockSpec((tb, wd), lambda b, h: (b, h)),
        out_shape=jax.ShapeDtypeStruct((t, D_MODEL), BF16),
        scratch_shapes=[pltpu.VMEM((tb, wd), F32),
                        pltpu.VMEM((HG_HEADS, LANES, LANES), F32), pltpu.VMEM((HG_HEADS, LANES, LANES), F32),
                        pltpu.VMEM((rg, LANES), F32), pltpu.VMEM((rg, LANES), F32), pltpu.VMEM((rg, LANES), F32)],
        compiler_params=_cparams(("arbitrary", "arbitrary")),
        name="hgrn_scan",
    )(proj, proj, proj, proj, lb2)


def _hg_finish_kernel(o_ref, g_ref, ng_ref, h_ref):
    o = o_ref[...].astype(F32)
    y = o * lax.rsqrt(jnp.mean(o * o, axis=-1, keepdims=True) + NORM_EPS) * ng_ref[...]
    g = g_ref[...].astype(F32)
    h_ref[...] = (y * (g * jax.nn.sigmoid(g))).astype(h_ref.dtype)


def _hgrn_finish(o, proj, norm_g):
    t, d = o.shape
    return pl.pallas_call(
        _hg_finish_kernel,
        grid=(t // ROW_TILE,),
        in_specs=[
            pl.BlockSpec((ROW_TILE, d), lambda i: (i, 0)),
            pl.BlockSpec((ROW_TILE, d), lambda i: (i, 2)),
            pl.BlockSpec((1, d), lambda i: (0, 0)),
        ],
        out_specs=pl.BlockSpec((ROW_TILE, d), lambda i: (i, 0)),
        out_shape=jax.ShapeDtypeStruct((t, d), BF16),
        compiler_params=_cparams(("arbitrary",)),
        name="hgrn_finish",
    )(o, proj, norm_g.astype(F32).reshape(1, d))


def _moe_kernel(be_ref, first_ref, nxt_ref, nu_ref, tok_ref, x_hbm, wg_hbm, wu_hbm, wd_hbm, o_ref,
                xbuf, sg_ref, su_ref, sd_ref, bg_ref, bu_ref, bd_ref, sem, xsem, *, layer):
    b = pl.program_id(0)
    n_used = nu_ref[0]

    def weight_copies(e):
        return (pltpu.make_async_copy(wg_hbm.at[layer, e], sg_ref, sem.at[0]),
                pltpu.make_async_copy(wu_hbm.at[layer, e], su_ref, sem.at[1]),
                pltpu.make_async_copy(wd_hbm.at[layer, e], sd_ref, sem.at[2]))

    def start_gather(blk, slot):
        base = blk * MOE_TM
        for r in range(MOE_TM):
            pltpu.make_async_copy(x_hbm.at[pl.ds(tok_ref[base + r], 1)], xbuf.at[slot, pl.ds(r, 1)],
                                  xsem.at[slot]).start()

    def wait_gather(slot):
        pltpu.make_async_copy(x_hbm.at[pl.ds(0, MOE_TM)], xbuf.at[slot], xsem.at[slot]).wait()

    @pl.when(b == 0)
    def _():
        for cp in weight_copies(be_ref[0]):
            cp.start()
        start_gather(0, 0)
        start_gather(1, 1)

    @pl.when(first_ref[b] == 1)
    def _():
        for cp in weight_copies(be_ref[b]):
            cp.wait()
        for src, dst in ((sg_ref, bg_ref), (su_ref, bu_ref), (sd_ref, bd_ref)):
            slab = src.shape[0] // MOE_CAST_SLABS

            def cast(r, carry, src=src, dst=dst, slab=slab):
                r0 = pl.multiple_of(r * slab, slab)
                dst[pl.ds(r0, slab), :] = src[pl.ds(r0, slab), :].astype(BF16)
                return carry

            lax.fori_loop(0, MOE_CAST_SLABS, cast, 0)

        @pl.when(nxt_ref[b] >= 0)
        def _():
            for cp in weight_copies(nxt_ref[b]):
                cp.start()

    @pl.when(b < n_used)
    def _():
        slot = b % MOE_XBUFS
        wait_gather(slot)
        xu = xbuf[slot]
        x = jnp.concatenate(
            [lax.bitcast_convert_type(xu << 16, F32).astype(BF16),
             lax.bitcast_convert_type(xu & jnp.uint32(0xFFFF0000), F32).astype(BF16)], axis=1)
        g = _dot(x, bg_ref[...])
        u = _dot(x, bu_ref[...])
        h = (g * jax.nn.sigmoid(g) * u).astype(BF16)
        ybits = lax.bitcast_convert_type(_dot(h, bd_ref[...]).astype(BF16).astype(F32), jnp.uint32)
        yh = ybits.shape[1] // 2
        o_ref[...] = (ybits[:, :yh] >> 16) | ybits[:, yh:]
        start_gather(b + 2, (b + 2) % MOE_XBUFS)

    @pl.when(b >= n_used)
    def _():
        @pl.when(b == n_used)
        def _():
            wait_gather(b % MOE_XBUFS)
            wait_gather((b + 1) % MOE_XBUFS)

        o_ref[...] = jnp.zeros(o_ref.shape, o_ref.dtype)


def _moe_experts(hp, slot_tok, blk_expert, blk_first, blk_next, n_used, w_gate, w_up, w_down, layer):
    d = 2 * hp.shape[1]
    f = w_gate.shape[3]
    n_slots = slot_tok.shape[0]
    n_blocks = n_slots // MOE_TM
    grid_spec = pltpu.PrefetchScalarGridSpec(
        num_scalar_prefetch=5,
        grid=(n_blocks,),
        in_specs=[pl.BlockSpec(memory_space=pl.ANY)] * 4,
        out_specs=pl.BlockSpec((MOE_TM, d // 2), lambda b, *_: (b, 0)),
        scratch_shapes=[
            pltpu.VMEM((MOE_XBUFS, MOE_TM, d // 2), jnp.uint32),
            pltpu.VMEM((d, f), F32), pltpu.VMEM((d, f), F32), pltpu.VMEM((f, d), F32),
            pltpu.VMEM((d, f), BF16), pltpu.VMEM((d, f), BF16), pltpu.VMEM((f, d), BF16),
            pltpu.SemaphoreType.DMA((3,)),
            pltpu.SemaphoreType.DMA((MOE_XBUFS,)),
        ],
    )
    return pl.pallas_call(
        functools.partial(_moe_kernel, layer=layer),
        grid_spec=grid_spec,
        out_shape=jax.ShapeDtypeStruct((n_slots, d // 2), jnp.uint32),
        compiler_params=_cparams(("arbitrary",)),
        name="moe_experts",
    )(blk_expert, blk_first, blk_next, n_used, slot_tok, hp, w_gate, w_up, w_down)


def _route(route, counts_col, n_blocks):
    t = route.shape[1]
    e_idx = route[0:TOP_K].T.astype(jnp.int32)
    rank = route[TOP_K:2 * TOP_K].T.astype(jnp.int32)
    w = route[2 * TOP_K:3 * TOP_K].T
    counts = counts_col[:, 0].astype(jnp.int32)
    pcounts = (counts + MOE_TM - 1) // MOE_TM * MOE_TM
    pends = jnp.cumsum(pcounts)
    pstarts = pends - pcounts
    e_ids = jnp.arange(N_EXPERTS, dtype=jnp.int32)
    start_of = jnp.sum(jnp.where(e_idx[..., None] == e_ids, pstarts, 0), axis=-1)
    dest = (start_of + rank).reshape(-1).astype(jnp.int32)
    flat_t = jnp.arange(t * TOP_K, dtype=jnp.int32) // TOP_K
    slot_tok = jnp.zeros((n_blocks * MOE_TM,), jnp.int32).at[dest].set(flat_t)
    blk_row0 = jnp.arange(n_blocks, dtype=jnp.int32) * MOE_TM
    blk_expert = jnp.minimum(
        jnp.sum((pends[None, :] <= blk_row0[:, None]).astype(jnp.int32), axis=1), N_EXPERTS - 1).astype(jnp.int32)
    n_used = (pends[-1] // MOE_TM).astype(jnp.int32).reshape(1)
    blk_ids = jnp.arange(n_blocks, dtype=jnp.int32)
    prev_expert = jnp.concatenate([jnp.full((1,), -1, jnp.int32), blk_expert[:-1]])
    blk_first = ((blk_ids < n_used[0]) & (blk_expert != prev_expert)).astype(jnp.int32)
    e_ids = jnp.arange(N_EXPERTS, dtype=jnp.int32)
    later_used = (e_ids[None, :] > e_ids[:, None]) & (counts[None, :] > 0)
    next_used = jnp.min(jnp.where(later_used, e_ids[None, :], N_EXPERTS), axis=1)
    next_used = jnp.where(next_used == N_EXPERTS, -1, next_used).astype(jnp.int32)
    blk_next = next_used[blk_expert]
    return w, dest.reshape(t, TOP_K), slot_tok, (blk_expert, blk_first, blk_next, n_used)


def _unpack_pairs(yu):
    return jnp.concatenate([lax.bitcast_convert_type(yu << 16, F32),
                            lax.bitcast_convert_type(yu & jnp.uint32(0xFFFF0000), F32)], axis=1)


def _combine_rows(dest_ref, w_ref, yb_hbm, ybuf, ysem, n_steps, tile_of):
    i = pl.program_id(0)

    def start_gather(step, slot):
        base = tile_of(step) * (ROW_TILE * TOP_K)
        for r in range(ROW_TILE):
            for k in range(TOP_K):
                pltpu.make_async_copy(yb_hbm.at[pl.ds(dest_ref[base + r * TOP_K + k], 1)],
                                      ybuf.at[slot, pl.ds(k * ROW_TILE + r, 1)], ysem.at[slot]).start()

    def wait_gather(slot):
        pltpu.make_async_copy(yb_hbm.at[pl.ds(0, TOP_K * ROW_TILE)], ybuf.at[slot], ysem.at[slot]).wait()

    @pl.when(i == 0)
    def _():
        start_gather(0, 0)
        start_gather(1, 1)

    slot = i % MOE_XBUFS
    wait_gather(slot)
    w = w_ref[...]
    y = (w[:, 0:1] * _unpack_pairs(ybuf[slot, 0:ROW_TILE, :])
         + w[:, 1:2] * _unpack_pairs(ybuf[slot, ROW_TILE:, :]))

    def finish():
        start_gather(jnp.minimum(i + 2, n_steps - 1), (i + 2) % MOE_XBUFS)

        @pl.when(i == n_steps - 1)
        def _():
            wait_gather((i + 1) % MOE_XBUFS)
            wait_gather((i + 2) % MOE_XBUFS)

    return y, finish


def _moe_res_kernel(dest_ref, xs_ref, w_ref, gate_ref, g_ref, shift_ref, scale_ref, yb_hbm, o_ref, h_ref,
                    ybuf, ysem, *, n_steps, tile_of):
    y, finish = _combine_rows(dest_ref, w_ref, yb_hbm, ybuf, ysem, n_steps, tile_of)
    x = xs_ref[...] + gate_ref[...] * y
    o_ref[...] = x
    xn = x * lax.rsqrt(jnp.mean(x * x, axis=-1, keepdims=True) + NORM_EPS)
    h_ref[...] = (xn * g_ref[...] * (1.0 + scale_ref[...]) + shift_ref[...]).astype(h_ref.dtype)
    finish()


def _moe_res_final_kernel(dest_ref, xs_ref, w_ref, gate_ref, fg_ref, yb_hbm, o_ref, ybuf, ysem, *, n_steps, tile_of):
    y, finish = _combine_rows(dest_ref, w_ref, yb_hbm, ybuf, ysem, n_steps, tile_of)
    x = xs_ref[...] + gate_ref[...] * y
    o_ref[...] = x * lax.rsqrt(jnp.mean(x * x, axis=-1, keepdims=True) + NORM_EPS) * fg_ref[...]
    finish()


def _moe_residual(xs, yb, dest, w, mods3, g3, layer, n_batch, tpb):
    t, d = xs.shape
    nb = n_batch
    n_tiles = t // ROW_TILE
    assert n_tiles >= 2
    row = lambda i, *_: (i, 0)

    def mod_spec(lyr, k):
        return pl.BlockSpec((None, 1, d), lambda i, *_: (lyr * 8 + _row_sel(i, tpb, nb), 0, k))

    grid_spec = pltpu.PrefetchScalarGridSpec(
        num_scalar_prefetch=1,
        grid=(n_tiles,),
        in_specs=[
            pl.BlockSpec((ROW_TILE, d), row),
            pl.BlockSpec((ROW_TILE, TOP_K), row),
            mod_spec(layer, 5),
            pl.BlockSpec((None, 1, d), lambda i, *_: ((layer + 1) * 2, 0, 0)),
            mod_spec(layer + 1, 0),
            mod_spec(layer + 1, 1),
            pl.BlockSpec(memory_space=pl.ANY),
        ],
        out_specs=[pl.BlockSpec((ROW_TILE, d), row), pl.BlockSpec((ROW_TILE, d), row)],
        scratch_shapes=[pltpu.VMEM((MOE_XBUFS, TOP_K * ROW_TILE, d // 2), jnp.uint32),
                        pltpu.SemaphoreType.DMA((MOE_XBUFS,))],
    )
    return pl.pallas_call(
        functools.partial(_moe_res_kernel, n_steps=n_tiles, tile_of=lambda step: step),
        grid_spec=grid_spec,
        out_shape=[jax.ShapeDtypeStruct((t, d), F32), jax.ShapeDtypeStruct((t, d), BF16)],
        input_output_aliases={1: 0},
        compiler_params=_cparams(("arbitrary",)),
        name="moe_residual",
    )(dest.reshape(-1), xs, w, mods3, g3, mods3, mods3, yb)


def _moe_residual_final(xs, yb, dest, w, mods3, layer, final_g, n_batch, tpb):
    t, d = xs.shape
    lat = tpb - CTX_LEN // ROW_TILE
    n_steps = n_batch * lat
    assert n_steps >= 2
    tile_of = lambda step: (step // lat) * tpb + CTX_LEN // ROW_TILE + step % lat
    src = lambda i, *_: (tile_of(i), 0)
    grid_spec = pltpu.PrefetchScalarGridSpec(
        num_scalar_prefetch=1,
        grid=(n_steps,),
        in_specs=[
            pl.BlockSpec((ROW_TILE, d), src),
            pl.BlockSpec((ROW_TILE, TOP_K), src),
            pl.BlockSpec((None, 1, d), lambda i, *_: (layer * 8 + i // lat, 0, 5)),
            pl.BlockSpec((1, d), lambda i, *_: (0, 0)),
            pl.BlockSpec(memory_space=pl.ANY),
        ],
        out_specs=pl.BlockSpec((ROW_TILE, d), lambda i, *_: (i, 0)),
        scratch_shapes=[pltpu.VMEM((MOE_XBUFS, TOP_K * ROW_TILE, d // 2), jnp.uint32),
                        pltpu.SemaphoreType.DMA((MOE_XBUFS,))],
    )
    return pl.pallas_call(
        functools.partial(_moe_res_final_kernel, n_steps=n_steps, tile_of=tile_of),
        grid_spec=grid_spec,
        out_shape=jax.ShapeDtypeStruct((n_steps * ROW_TILE, d), F32),
        compiler_params=_cparams(("arbitrary",)),
        name="moe_residual_final",
    )(dest.reshape(-1), xs, w, mods3, final_g.astype(F32).reshape(1, d), yb)


def _rope_tables(n_batch, seq):
    quarter = DIFF_HEAD_DIM // 4
    tpos = jnp.arange(seq)
    rowp = (tpos // GRID_W).astype(F32)
    colp = (tpos % GRID_W).astype(F32)
    inv = ROPE_THETA ** (-jnp.arange(quarter, dtype=F32) / quarter)
    ang_r = rowp[:, None] * inv[None, :]
    ang_c = colp[:, None] * inv[None, :]
    cos = jnp.concatenate([jnp.cos(ang_r)] * 2 + [jnp.cos(ang_c)] * 2, axis=-1)
    sin = jnp.concatenate([-jnp.sin(ang_r), jnp.sin(ang_r), -jnp.sin(ang_c), jnp.sin(ang_c)], axis=-1)
    cos = jnp.concatenate([jnp.ones((CTX_LEN, LANES), F32), cos], axis=0)
    sin = jnp.concatenate([jnp.zeros((CTX_LEN, LANES), F32), sin], axis=0)
    return jnp.tile(cos, (n_batch, 1)), jnp.tile(sin, (n_batch, 1))


def kernel(x, c, ctx, c_ctx, ada_w, ada_b, norm_g, na_w_qkv, na_rpb, na_w_o, da_w_qkv, da_lambda, da_subln_g, da_w_o, hg_w_in, hg_lb, hg_norm_g, hg_w_o, moe_w_router, moe_b_router, moe_w_gate, moe_w_up, moe_w_down, final_g):
    n_batch, seq, d = x.shape
    tb = CTX_LEN + seq
    t = n_batch * tb
    tpb = tb // ROW_TILE
    assert d == D_MODEL and ctx.shape[1] == CTX_LEN and n_batch + 1 <= 8
    assert seq % DA_TQ == 0 and t % MM_TM == 0 and seq // GRID_W >= NA_WIN_ROWS

    xs = jnp.concatenate([ctx, x], axis=1).reshape(t, d).astype(F32)

    c_rows = jnp.zeros((8, d), F32).at[:n_batch].set(c.astype(F32)).at[n_batch].set(c_ctx.astype(F32))
    mods = _ada_mods(c_rows, ada_w.astype(F32), ada_b.astype(F32))
    mods3 = mods.reshape(DEPTH * 8, 1, 6 * d)
    g3 = norm_g.astype(F32).reshape(DEPTH * 2, 1, d)

    lb_p = jax.nn.softmax(hg_lb.astype(F32), axis=1)
    lb_all = jnp.cumsum(lb_p, axis=1) - lb_p[:, :1]

    wr = jnp.zeros((d, LANES), F32).at[:, :N_EXPERTS].set(moe_w_router.astype(F32))
    wr_hi = wr.astype(BF16)
    wr_lo = (wr - wr_hi.astype(F32)).astype(BF16)
    rb = moe_b_router.astype(F32).reshape(N_EXPERTS, 1)

    n_blocks = -(-(t * TOP_K) // MOE_TM) + N_EXPERTS + (MOE_XBUFS - 1)
    rope = _rope_tables(n_batch, seq)
    wg = moe_w_gate.astype(F32)
    wu = moe_w_up.astype(F32)
    wd = moe_w_down.astype(F32)
    na_qs = NA_HEAD_DIM ** -0.5 * LOG2E
    da_qs = DIFF_HEAD_DIM ** -0.5 * LOG2E

    counters = [0, 0, 0]
    out = None
    for i in range(DEPTH):
        kind = i % N_MIXERS
        j = counters[kind]
        counters[kind] += 1

        if i == 0:
            h = _norm_mod(xs, mods3, g3, i, 0, n_batch, tpb)[0]
        if kind == 0:
            qkv = _matmul(h, na_w_qkv[j], q_cols=d, q_scale=na_qs)
            o = _na_attention(qkv, na_rpb[j], n_batch, tb)
            w_o = na_w_o[j]
        elif kind == 1:
            qkv = _matmul(h, da_w_qkv[j], q_cols=d, q_scale=da_qs, rope=rope, n_rope_cols=2 * d)
            o = _diff_attention(qkv, da_lambda[j], da_subln_g[j], i, n_batch, tb)
            w_o = da_w_o[j]
        else:
            proj = _matmul(h, hg_w_in[j])
            o32 = _hgrn_scan(proj, lb_all[:, i, :], n_batch, tb)
            o = _hgrn_finish(o32, proj, hg_norm_g[j])
            w_o = hg_w_o[j]
        xs = _matmul_residual(o, w_o, xs, mods3, i, 2, n_batch, tpb)

        hp, route, counts = _norm_mod(xs, mods3, g3, i, 1, n_batch, tpb, router=(wr_hi, wr_lo, rb))
        w, dest, slot_tok, blk_info = _route(route, counts, n_blocks)
        yb = _moe_experts(hp, slot_tok, *blk_info, wg, wu, wd, i)
        if i < DEPTH - 1:
            xs, h = _moe_residual(xs, yb, dest, w, mods3, g3, i, n_batch, tpb)
        else:
            out = _moe_residual_final(xs, yb, dest, w, mods3, i, final_g, n_batch, tpb)
    return out.reshape(n_batch, seq, d).astype(x.dtype)
```
